```python
import jax, jax.numpy as jnp
from jax import lax
import numpy as np

D_MODEL = 1024
BATCH = 4
SEQ = 4096
DEPTH = 2
DEC_BATCH = 2
DEC_SEQ = 8192
PAST_LEN = 128

DIL_CONFIGS = ((128, 1), (512, 4), (2048, 16))
N_DIL_GROUPS = len(DIL_CONFIGS)
HEADS_PER_GROUP = 8
HEAD_DIM = 64
N_SIDE = 64
MAX_REACH = max(w // 2 for w, _ in DIL_CONFIGS)
Q_BLOCK = 128
ATTN_QKV_DIM = 3 * N_DIL_GROUPS * HEADS_PER_GROUP * HEAD_DIM
ATTN_OUT_DIM = HEADS_PER_GROUP * HEAD_DIM
SGU_CHUNK = 128
SGU_HALF = 3 * D_MODEL
SGU_GROUPS = 8
SGU_GROUP_DIM = SGU_HALF // SGU_GROUPS
N_EXPERTS = 32
TOP_K = 4
D_EXPERT = D_MODEL
SWIGLU_ALPHA = 1.702
SWIGLU_LIMIT = 7.0
ROW_BLOCK = 128
LN_EPS = 1e-5
DEEPNORM_ALPHA = (2 * DEPTH) ** 0.25
DEEPNORM_BETA = (8 * DEPTH) ** -0.25
N_MIXERS = 2
N_ATTN_LAYERS = (DEPTH + 1) // 2
N_SGU_LAYERS = DEPTH // 2

kernel_name = 'hybrid_dilated_sgu_moe_encoder'


def _layer_norm(x, g, b):
    xf = x.astype(jnp.float32)
    mu = jnp.mean(xf, axis=-1, keepdims=True)
    var = jnp.mean(jnp.square(xf - mu), axis=-1, keepdims=True)
    y = (xf - mu) * lax.rsqrt(var + LN_EPS) * g.astype(jnp.float32) + b.astype(jnp.float32)
    return y.astype(x.dtype)


def _alibi_slopes():
    n = N_DIL_GROUPS * HEADS_PER_GROUP
    s = 2.0 ** (-8.0 * jnp.arange(1, n + 1, dtype=jnp.float32) / n)
    return s.reshape(N_DIL_GROUPS, HEADS_PER_GROUP)


def _dilated_attention(x, w_qkv, w_o, b_o):
    B, S, _ = x.shape
    qkv = (x @ w_qkv).reshape(B, S, 3, N_DIL_GROUPS, HEADS_PER_GROUP, HEAD_DIM)
    q = qkv[:, :, 0] * (HEAD_DIM ** -0.5)
    kv = jnp.pad(qkv[:, :, 1:], ((0, 0), (MAX_REACH, MAX_REACH), (0, 0), (0, 0), (0, 0), (0, 0)))
    kv_groups = [kv[:, :, :, g] for g in range(N_DIL_GROUPS)]
    slopes = _alibi_slopes()
    steps = jnp.arange(-N_SIDE, N_SIDE + 1)

    def block(q0):
        qb = lax.dynamic_slice_in_dim(q, q0, Q_BLOCK, axis=1)
        qpos = q0 + jnp.arange(Q_BLOCK)
        outs, lses = [], []
        for g, (_, dil) in enumerate(DIL_CONFIGS):
            offs = steps * dil
            kpos = qpos[:, None] + offs[None, :]
            kvg = jnp.take(kv_groups[g], kpos + MAX_REACH, axis=1)
            s = jnp.einsum('bqhd,bqkhd->bhqk', qb[:, :, g], kvg[:, :, :, 0]).astype(jnp.float32)
            s = s - slopes[g][:, None, None] * jnp.abs(offs).astype(jnp.float32)[None, None, :]
            valid = (kpos >= 0) & (kpos < S)
            s = jnp.where(valid[None, None], s, -jnp.inf)
            lse = jax.nn.logsumexp(s, axis=-1)
            p = jnp.exp(s - lse[..., None]).astype(x.dtype)
            outs.append(jnp.einsum('bhqk,bqkhd->bqhd', p, kvg[:, :, :, 1]))
            lses.append(jnp.transpose(lse, (0, 2, 1)))
        wgt = jax.nn.softmax(jnp.stack(lses), axis=0).astype(x.dtype)
        return jnp.einsum('gbqh,gbqhd->bqhd', wgt, jnp.stack(outs))

    out = lax.map(block, jnp.arange(S // Q_BLOCK) * Q_BLOCK)
    out = jnp.transpose(out, (1, 0, 2, 3, 4)).reshape(B, S, ATTN_OUT_DIM)
    return out @ w_o + b_o


def _spatial_gating(x, w_in, b_in, norm_g, norm_b, w_s, b_s, w_out, b_out):
    B, S, _ = x.shape
    z = jax.nn.gelu(x @ w_in + b_in)
    u, v = z[..., :SGU_HALF], z[..., SGU_HALF:]
    v = _layer_norm(v, norm_g, norm_b)
    v = v.reshape(B, S // SGU_CHUNK, SGU_CHUNK, SGU_GROUPS, SGU_GROUP_DIM)
    v = jnp.einsum('gts,bcsgf->bctgf', w_s, v) + jnp.transpose(b_s)[None, None, :, :, None]
    return (u * v.reshape(B, S, SGU_HALF)) @ w_out + b_out


def _moe(x, router_w, router_b, w1, b1, w2, b2):
    B, S, D = x.shape
    T = B * S
    xt = x.reshape(T, D)
    logits = (xt @ router_w).astype(jnp.float32) + router_b.astype(jnp.float32)
    top_v, top_e = lax.top_k(logits, TOP_K)
    gates = jax.nn.softmax(top_v, axis=-1)
    A = T * TOP_K
    e_flat = top_e.reshape(A)
    tok_flat = jnp.arange(A, dtype=jnp.int32) // TOP_K
    g_flat = gates.reshape(A)
    order = jnp.argsort(e_flat)
    e_sorted = e_flat[order]
    counts = jnp.bincount(e_flat, length=N_EXPERTS)
    starts = jnp.cumsum(counts) - counts
    padded = (counts + ROW_BLOCK - 1) // ROW_BLOCK * ROW_BLOCK
    pends = jnp.cumsum(padded)
    pstarts = pends - padded
    dest = pstarts[e_sorted] + jnp.arange(A) - starts[e_sorted]
    R = ((A + ROW_BLOCK - 1) // ROW_BLOCK + N_EXPERTS) * ROW_BLOCK
    n_blk = R // ROW_BLOCK
    row_tok = jnp.full((R,), T, jnp.int32).at[dest].set(tok_flat[order])
    row_gate = jnp.zeros((R,), jnp.float32).at[dest].set(g_flat[order])
    blk_exp = jnp.minimum(jnp.searchsorted(pends, jnp.arange(n_blk) * ROW_BLOCK, side='right'), N_EXPERTS - 1)
    x_pad = jnp.concatenate([xt, jnp.zeros((1, D), xt.dtype)], axis=0)

    def expert_rows(args):
        e, toks, g = args
        h = x_pad[toks] @ w1[e] + b1[e]
        h_glu = jnp.minimum(h[:, 0::2], SWIGLU_LIMIT)
        h_lin = jnp.clip(h[:, 1::2], -SWIGLU_LIMIT, SWIGLU_LIMIT)
        a = h_glu * jax.nn.sigmoid(SWIGLU_ALPHA * h_glu) * (h_lin + 1.0)
        return (a @ w2[e] + b2[e]) * g[:, None]

    y = lax.map(expert_rows, (blk_exp, row_tok.reshape(n_blk, ROW_BLOCK),
                              row_gate.reshape(n_blk, ROW_BLOCK).astype(x.dtype)))
    out = jnp.zeros((T + 1, D), x.dtype).at[row_tok].add(y.reshape(R, D))[:T]
    return out.reshape(B, S, D)


def _trunk(x, attn_w_qkv, attn_w_o, attn_b_o, sgu_w_in, sgu_b_in, sgu_norm_g, sgu_norm_b,
           sgu_w_s, sgu_b_s, sgu_w_out, sgu_b_out, router_w, router_b, exp_w1, exp_b1,
           exp_w2, exp_b2, ln_mix_g, ln_mix_b, ln_ffn_g, ln_ffn_b):
    for i in range(DEPTH):
        j = i // N_MIXERS
        if i % N_MIXERS == 0:
            h = _dilated_attention(x, attn_w_qkv[j], attn_w_o[j], attn_b_o[j])
        else:
            h = _spatial_gating(x, sgu_w_in[j], sgu_b_in[j], sgu_norm_g[j], sgu_norm_b[j],
                                sgu_w_s[j], sgu_b_s[j], sgu_w_out[j], sgu_b_out[j])
        x = _layer_norm(DEEPNORM_ALPHA * x + h, ln_mix_g[i], ln_mix_b[i])
        h = _moe(x, router_w[i], router_b[i], exp_w1[i], exp_b1[i], exp_w2[i], exp_b2[i])
        x = _layer_norm(DEEPNORM_ALPHA * x + h, ln_ffn_g[i], ln_ffn_b[i])
    return x


def setup_inputs(seed: int = 0) -> dict:
    key = jax.random.key(seed)
    ks = jax.random.split(key, 24)
    D = D_MODEL

    def nrm(k, shape, scale):
        return jax.random.normal(k, shape, jnp.float32) * scale

    return {
        'x_prompt': nrm(ks[0], (BATCH, SEQ, D), 1.0),
        'x_sample': nrm(ks[1], (DEC_BATCH, DEC_SEQ, D), 1.0),
        'attn_w_qkv': nrm(ks[2], (N_ATTN_LAYERS, D, ATTN_QKV_DIM), D ** -0.5),
        'attn_w_o': nrm(ks[3], (N_ATTN_LAYERS, ATTN_OUT_DIM, D), DEEPNORM_BETA * ATTN_OUT_DIM ** -0.5),
        'attn_b_o': nrm(ks[4], (N_ATTN_LAYERS, D), 0.02),
        'sgu_w_in': nrm(ks[5], (N_SGU_LAYERS, D, 2 * SGU_HALF), D ** -0.5),
        'sgu_b_in': nrm(ks[6], (N_SGU_LAYERS, 2 * SGU_HALF), 0.02),
        'sgu_norm_g': 1.0 + nrm(ks[7], (N_SGU_LAYERS, SGU_HALF), 0.02),
        'sgu_norm_b': nrm(ks[8], (N_SGU_LAYERS, SGU_HALF), 0.02),
        'sgu_w_s': nrm(ks[9], (N_SGU_LAYERS, SGU_GROUPS, SGU_CHUNK, SGU_CHUNK), SGU_CHUNK ** -0.5),
        'sgu_b_s': 1.0 + nrm(ks[10], (N_SGU_LAYERS, SGU_GROUPS, SGU_CHUNK), 0.02),
        'sgu_w_out': nrm(ks[11], (N_SGU_LAYERS, SGU_HALF, D), DEEPNORM_BETA * SGU_HALF ** -0.5),
        'sgu_b_out': nrm(ks[12], (N_SGU_LAYERS, D), 0.02),
        'router_w': nrm(ks[13], (DEPTH, D, N_EXPERTS), D ** -0.5),
        'router_b': nrm(ks[14], (DEPTH, N_EXPERTS), 0.01),
        'exp_w1': nrm(ks[15], (DEPTH, N_EXPERTS, D, 2 * D_EXPERT), D ** -0.5),
        'exp_b1': nrm(ks[16], (DEPTH, N_EXPERTS, 2 * D_EXPERT), 0.02),
        'exp_w2': nrm(ks[17], (DEPTH, N_EXPERTS, D_EXPERT, D), DEEPNORM_BETA * D_EXPERT ** -0.5),
        'exp_b2': nrm(ks[18], (DEPTH, N_EXPERTS, D), 0.02),
        'ln_mix_g': 1.0 + nrm(ks[19], (DEPTH, D), 0.02),
        'ln_mix_b': nrm(ks[20], (DEPTH, D), 0.02),
        'ln_ffn_g': 1.0 + nrm(ks[21], (DEPTH, D), 0.02),
        'ln_ffn_b': nrm(ks[22], (DEPTH, D), 0.02),
    }


def reference(x_prompt, x_sample, attn_w_qkv, attn_w_o, attn_b_o, sgu_w_in, sgu_b_in, sgu_norm_g,
              sgu_norm_b, sgu_w_s, sgu_b_s, sgu_w_out, sgu_b_out, router_w, router_b, exp_w1, exp_b1,
              exp_w2, exp_b2, ln_mix_g, ln_mix_b, ln_ffn_g, ln_ffn_b):
    y_prompt = _trunk(x_prompt, attn_w_qkv, attn_w_o, attn_b_o, sgu_w_in, sgu_b_in, sgu_norm_g,
                      sgu_norm_b, sgu_w_s, sgu_b_s, sgu_w_out, sgu_b_out, router_w, router_b,
                      exp_w1, exp_b1, exp_w2, exp_b2, ln_mix_g, ln_mix_b, ln_ffn_g, ln_ffn_b)
    y_sample = _trunk(x_sample, attn_w_qkv, attn_w_o, attn_b_o, sgu_w_in, sgu_b_in, sgu_norm_g,
                      sgu_norm_b, sgu_w_s, sgu_b_s, sgu_w_out, sgu_b_out, router_w, router_b,
                      exp_w1, exp_b1, exp_w2, exp_b2, ln_mix_g, ln_mix_b, ln_ffn_g, ln_ffn_b)
    return (y_prompt, y_sample)
```

```python
import functools

import jax
import jax.numpy as jnp
from jax import lax
from jax.experimental import pallas as pl
from jax.experimental.pallas import tpu as pltpu

F32 = jnp.float32
BF16 = jnp.bfloat16

LANES = 128
SUBLANES = 8
VMEM_LIMIT_CAP = 56 * 1024 * 1024

D_MODEL = 1024
DIL_CONFIGS = ((128, 1), (512, 4), (2048, 16))
N_GROUPS = len(DIL_CONFIGS)
HEADS = 8
HEAD_DIM = 64
GROUP_DIM = HEADS * HEAD_DIM
N_SIDE = 64
Q_BLOCK = 128
KV_BLOCK = 64
KV_PIECES = (Q_BLOCK + 2 * N_SIDE) // KV_BLOCK
KV_WIN = KV_PIECES * KV_BLOCK
SGU_CHUNK = 128
SGU_HALF = 3 * D_MODEL
SGU_GROUPS = 8
SGU_GROUP_DIM = SGU_HALF // SGU_GROUPS
N_EXPERTS = 32
TOP_K = 4
D_EXPERT = D_MODEL
SWIGLU_ALPHA = 1.702
SWIGLU_LIMIT = 7.0
LN_EPS = 1e-5
DEPTH = 2
DEEPNORM_ALPHA = (2 * DEPTH) ** 0.25
NEG = -1e30

ROW_TILE = 256
EXPERT_ROWS = 256


def _cparams(semantics, vmem_bytes):
    return pltpu.CompilerParams(dimension_semantics=semantics,
                                vmem_limit_bytes=min(int(vmem_bytes), VMEM_LIMIT_CAP))


def _layer_norm(y, g, b):
    mu = jnp.mean(y, axis=-1, keepdims=True)
    yc = y - mu
    var = jnp.mean(yc * yc, axis=-1, keepdims=True)
    return yc * lax.rsqrt(var + LN_EPS) * g + b


def _qkv_kernel(x_ref, w_ref, o_ref):
    o_ref[...] = jnp.dot(x_ref[...].astype(BF16), w_ref[...],
                         preferred_element_type=F32).astype(BF16)


def _qkv_project(x, w, dil):
    t, d = x.shape
    n = t // dil
    tm = min(512, n)
    width = w.shape[1]
    xv = x.reshape(n, dil * d)
    return pl.pallas_call(
        _qkv_kernel,
        grid=(dil, n // tm),
        in_specs=[pl.BlockSpec((tm, d), lambda r, i: (i, r)),
                  pl.BlockSpec((d, width), lambda r, i: (0, 0))],
        out_specs=pl.BlockSpec((None, tm, width), lambda r, i: (r, i, 0)),
        out_shape=jax.ShapeDtypeStruct((dil, n, width), BF16),
        compiler_params=_cparams(("parallel", "parallel"),
                                 2 * (tm * d * 4 + d * width * 2 + tm * width * 2) + tm * width * 4
                                 + (8 << 20)),
        name=f"qkv_project_d{dil}",
    )(xv, w)


def _attn_kernel(q_ref, *refs, segs):
    k_refs = refs[:KV_PIECES]
    v_refs = refs[KV_PIECES:2 * KV_PIECES]
    bias_ref, o_ref, lse_ref = refs[2 * KV_PIECES:]
    (n_a, len_a), (_, len_b) = segs
    row0 = pl.program_id(1) * Q_BLOCK
    in_a = row0 < n_a
    lo = jnp.where(in_a, (row0 // len_a) * len_a, n_a + ((row0 - n_a) // len_b) * len_b)
    hi = lo + jnp.where(in_a, len_a, len_b)
    krow = lax.broadcasted_iota(jnp.int32, (1, KV_WIN), 1) + (row0 - N_SIDE)
    col_bias = jnp.where((krow >= lo) & (krow < hi), 0.0, NEG).astype(F32)
    lane = lax.broadcasted_iota(jnp.int32, (1, LANES), 1)
    first_head = lane < HEAD_DIM
    lse_tile = jnp.zeros((Q_BLOCK, LANES), F32)
    nt = (((1,), (1,)), ((), ()))
    for p in range(HEADS // 2):
        sl = slice(p * LANES, (p + 1) * LANES)
        q = q_ref[:, sl]
        k = jnp.concatenate([r[:, sl] for r in k_refs], axis=0)
        v = jnp.concatenate([r[:, sl] for r in v_refs], axis=0)
        zero = jnp.zeros_like(q)
        outs, inv_ls = [], []
        for j, qh in enumerate((jnp.where(first_head, q, zero), jnp.where(first_head, zero, q))):
            h = 2 * p + j
            s = lax.dot_general(qh, k, nt, preferred_element_type=F32)
            s = s + bias_ref[h] + col_bias
            m = jnp.max(s, axis=-1, keepdims=True)
            e = jnp.exp(s - m)
            l = jnp.sum(e, axis=-1, keepdims=True)
            outs.append(jnp.dot(e.astype(BF16), v, preferred_element_type=F32))
            inv_ls.append(1.0 / l)
            lse_tile = jnp.where(lane == h, m + jnp.log(l), lse_tile)
        o = jnp.where(first_head, outs[0] * inv_ls[0], outs[1] * inv_ls[1])
        o_ref[:, sl] = o.astype(BF16)
    lse_ref[...] = lse_tile


def _band_bias(slopes, dil):
    qi = jnp.arange(Q_BLOCK)[:, None]
    kc = jnp.arange(KV_WIN)[None, :]
    dist = jnp.abs(kc - N_SIDE - qi)
    pen = -slopes[:, None, None] * (dist * dil).astype(F32)[None]
    return jnp.where((dist <= N_SIDE)[None], pen, NEG).astype(F32)


def _attention(qkv, bias, dil, segs):
    _, n, _ = qkv.shape
    nkb = n // KV_BLOCK

    def kv_spec(c, col):
        def imap(r, i):
            return (r, jnp.clip(2 * i - 1 + c, 0, nkb - 1), col)
        return pl.BlockSpec((None, KV_BLOCK, GROUP_DIM), imap)

    in_specs = ([pl.BlockSpec((None, Q_BLOCK, GROUP_DIM), lambda r, i: (r, i, 0))]
                + [kv_spec(c, 1) for c in range(KV_PIECES)]
                + [kv_spec(c, 2) for c in range(KV_PIECES)]
                + [pl.BlockSpec((HEADS, Q_BLOCK, KV_WIN), lambda r, i: (0, 0, 0))])
    return pl.pallas_call(
        functools.partial(_attn_kernel, segs=segs),
        grid=(dil, n // Q_BLOCK),
        in_specs=in_specs,
        out_specs=[pl.BlockSpec((Q_BLOCK, GROUP_DIM), lambda r, i: (i, r)),
                   pl.BlockSpec((Q_BLOCK, LANES), lambda r, i: (i, r))],
        out_shape=[jax.ShapeDtypeStruct((n, dil * GROUP_DIM), BF16),
                   jax.ShapeDtypeStruct((n, dil * LANES), F32)],
        compiler_params=_cparams(("parallel", "parallel"), 32 << 20),
        name=f"band_attention_d{dil}",
    )(qkv, *([qkv] * (2 * KV_PIECES)), bias)


def _merge_kernel(o0, o1, o2, l0, l1, l2, x_ref, wo_ref, bo_ref, g_ref, b_ref, y_ref):
    lses = [l0[...], l1[...], l2[...]]
    m = jnp.maximum(jnp.maximum(lses[0], lses[1]), lses[2])
    es = [jnp.exp(l - m) for l in lses]
    inv = 1.0 / (es[0] + es[1] + es[2])
    ws = [e * inv for e in es]
    lane = lax.broadcasted_iota(jnp.int32, (1, LANES), 1)
    first_head = lane < HEAD_DIM
    parts = []
    for p in range(HEADS // 2):
        sl = slice(p * LANES, (p + 1) * LANES)
        acc = None
        for w, o in zip(ws, (o0, o1, o2)):
            wp = jnp.where(first_head, w[:, 2 * p:2 * p + 1], w[:, 2 * p + 1:2 * p + 2])
            term = wp * o[:, sl].astype(F32)
            acc = term if acc is None else acc + term
        parts.append(acc)
    merged = jnp.concatenate(parts, axis=-1).astype(BF16)
    h = jnp.dot(merged, wo_ref[...], preferred_element_type=F32) + bo_ref[...]
    y_ref[...] = _layer_norm(DEEPNORM_ALPHA * x_ref[...] + h, g_ref[...], b_ref[...])


def _merge_project(os_, lses, x, w_o, b_o, ln_g, ln_b):
    t, d = x.shape
    tm = ROW_TILE
    row = lambda i: (i, 0)
    const = lambda i: (0, 0)
    vec = pl.BlockSpec((1, d), const)
    return pl.pallas_call(
        _merge_kernel,
        grid=(t // tm,),
        in_specs=([pl.BlockSpec((tm, GROUP_DIM), row)] * 3 + [pl.BlockSpec((tm, LANES), row)] * 3
                  + [pl.BlockSpec((tm, d), row), pl.BlockSpec((GROUP_DIM, d), const), vec, vec, vec]),
        out_specs=pl.BlockSpec((tm, d), row),
        out_shape=jax.ShapeDtypeStruct((t, d), F32),
        compiler_params=_cparams(("parallel",), 32 << 20),
        name="merge_outproj_ln",
    )(*os_, *lses, x, w_o, b_o, ln_g, ln_b)


def _split_bf16(a):
    hi = a.astype(BF16)
    return hi, (a - hi.astype(F32)).astype(BF16)


def _router_kernel(x_ref, w_ref, b_ref, e_ref, g_ref, r_ref, cnt_ref, carry_ref):
    tm = x_ref.shape[0]

    @pl.when(pl.program_id(0) == 0)
    def _():
        carry_ref[...] = jnp.zeros_like(carry_ref)

    xh, xl = _split_bf16(x_ref[...])
    wh, wl = _split_bf16(w_ref[...])
    dot = functools.partial(jnp.dot, preferred_element_type=F32)
    logits = dot(xh, wh) + (dot(xl, wh) + dot(xh, wl)) + b_ref[...]
    lane = lax.broadcasted_iota(jnp.int32, logits.shape, 1)
    col4 = lax.broadcasted_iota(jnp.int32, (tm, TOP_K), 1)
    work = logits
    picks, vals, idxs = [], [], []
    for _k in range(TOP_K):
        mx = jnp.max(work, axis=-1, keepdims=True)
        idx = jnp.min(jnp.where(work == mx, lane, N_EXPERTS), axis=-1, keepdims=True)
        pick = lane == idx
        work = jnp.where(pick, -jnp.inf, work)
        picks.append(pick)
        vals.append(mx)
        idxs.append(idx)
    ex = [jnp.exp(v - vals[0]) for v in vals]
    inv = 1.0 / (ex[0] + ex[1] + ex[2] + ex[3])
    chosen = jnp.zeros(logits.shape, F32)
    for pick in picks:
        chosen = chosen + pick.astype(F32)
    r_i = lax.broadcasted_iota(jnp.int32, (tm, tm), 0)
    c_i = lax.broadcasted_iota(jnp.int32, (tm, tm), 1)
    tri = (c_i < r_i).astype(BF16)
    before = dot(tri, chosen.astype(BF16)) + carry_ref[...]
    e_out = jnp.zeros((tm, TOP_K), jnp.int32)
    g_out = jnp.zeros((tm, TOP_K), F32)
    r_out = jnp.zeros((tm, TOP_K), jnp.int32)
    for k in range(TOP_K):
        rank = jnp.sum(jnp.where(picks[k], before, 0.0), axis=-1, keepdims=True)
        e_out = jnp.where(col4 == k, idxs[k], e_out)
        g_out = jnp.where(col4 == k, ex[k] * inv, g_out)
        r_out = jnp.where(col4 == k, rank.astype(jnp.int32), r_out)
    e_ref[...] = e_out
    g_ref[...] = g_out
    r_ref[...] = r_out
    carry_ref[...] += jnp.sum(chosen, axis=0, keepdims=True)
    cnt_ref[...] = carry_ref[...]


def _router(x, w, b):
    t, d = x.shape
    tm = ROW_TILE
    row = lambda i: (i, 0)
    const = lambda i: (0, 0)
    k_spec = pl.BlockSpec((tm, TOP_K), row)
    return pl.pallas_call(
        _router_kernel,
        grid=(t // tm,),
        in_specs=[pl.BlockSpec((tm, d), row), pl.BlockSpec((d, N_EXPERTS), const),
                  pl.BlockSpec((1, N_EXPERTS), const)],
        out_specs=[k_spec, k_spec, k_spec, pl.BlockSpec((1, N_EXPERTS), const)],
        out_shape=[jax.ShapeDtypeStruct((t, TOP_K), jnp.int32), jax.ShapeDtypeStruct((t, TOP_K), F32),
                   jax.ShapeDtypeStruct((t, TOP_K), jnp.int32), jax.ShapeDtypeStruct((1, N_EXPERTS), F32)],
        scratch_shapes=[pltpu.VMEM((1, N_EXPERTS), F32)],
        compiler_params=_cparams(("arbitrary",), 32 << 20),
        name="router_topk",
    )(x, w, b)


def _expert_kernel(be_ref, nused_ref, tok_hbm, dst_hbm, x_hbm, w1_ref, b1g_ref, b1l_ref, w2_ref, b2_ref,
                   y_hbm, tok_s, dst_s, xbuf, ybuf, w1g, w1l, sems):
    i = pl.program_id(0)
    bm = xbuf.shape[0]

    @pl.when(i == 0)
    def _():
        ybuf[...] = jnp.zeros_like(ybuf)
        cp = pltpu.make_async_copy(ybuf, y_hbm.at[pl.ds(y_hbm.shape[0] - bm, bm), :], sems.at[3])
        cp.start()
        cp.wait()

    @pl.when(i < nused_ref[0])
    def _():
        idx_cp = [pltpu.make_async_copy(tok_hbm.at[i], tok_s, sems.at[0]),
                  pltpu.make_async_copy(dst_hbm.at[i], dst_s, sems.at[1])]
        for cp in idx_cp:
            cp.start()
        for cp in idx_cp:
            cp.wait()

        def gather(j, c):
            pltpu.make_async_copy(x_hbm.at[pl.ds(tok_s[j], 1), :], xbuf.at[pl.ds(j, 1), :], sems.at[2]).start()
            return c
        lax.fori_loop(0, bm, gather, 0)

        changed = jnp.logical_or(i == 0, be_ref[i] != be_ref[jnp.maximum(i - 1, 0)])

        @pl.when(changed)
        def _():
            packed = w1_ref[...]
            w1g[...] = lax.bitcast_convert_type(packed << 16, F32).astype(BF16)
            w1l[...] = lax.bitcast_convert_type(packed & jnp.uint32(0xFFFF0000), F32).astype(BF16)

        pltpu.make_async_copy(x_hbm.at[pl.ds(0, bm), :], xbuf, sems.at[2]).wait()
        x = xbuf[...].astype(BF16)
        hg = jnp.dot(x, w1g[...], preferred_element_type=F32) + b1g_ref[...]
        hl = jnp.dot(x, w1l[...], preferred_element_type=F32) + b1l_ref[...]
        hg = jnp.minimum(hg, SWIGLU_LIMIT)
        hl = jnp.clip(hl, -SWIGLU_LIMIT, SWIGLU_LIMIT)
        act = hg * (1.0 / (1.0 + jnp.exp(-SWIGLU_ALPHA * hg))) * (hl + 1.0)
        ybuf[...] = jnp.dot(act.astype(BF16), w2_ref[...], preferred_element_type=F32) + b2_ref[...]

        def scatter(j, c):
            pltpu.make_async_copy(ybuf.at[pl.ds(j, 1), :], y_hbm.at[pl.ds(dst_s[j], 1), :], sems.at[3]).start()
            return c
        lax.fori_loop(0, bm, scatter, 0)
        pltpu.make_async_copy(ybuf, y_hbm.at[pl.ds(0, bm), :], sems.at[3]).wait()


def _experts(x, blk_exp, n_used, row_tok, row_dst, w1p, b1g, b1l, w2, b2):
    t, d = x.shape
    nb, bm = row_tok.shape
    de = w2.shape[1]
    wmap = lambda i, be, nu: (be[i], 0, 0)
    any_spec = pl.BlockSpec(memory_space=pl.ANY)
    grid_spec = pltpu.PrefetchScalarGridSpec(
        num_scalar_prefetch=2,
        grid=(nb,),
        in_specs=[any_spec, any_spec, any_spec,
                  pl.BlockSpec((None, d, de), wmap), pl.BlockSpec((None, 1, de), wmap),
                  pl.BlockSpec((None, 1, de), wmap), pl.BlockSpec((None, de, d), wmap),
                  pl.BlockSpec((None, 1, d), wmap)],
        out_specs=any_spec,
        scratch_shapes=[pltpu.SMEM((bm,), jnp.int32), pltpu.SMEM((bm,), jnp.int32),
                        pltpu.VMEM((bm, d), F32), pltpu.VMEM((bm, d), F32),
                        pltpu.VMEM((d, de), BF16), pltpu.VMEM((d, de), BF16),
                        pltpu.SemaphoreType.DMA((4,))])
    return pl.pallas_call(
        _expert_kernel,
        grid_spec=grid_spec,
        out_shape=jax.ShapeDtypeStruct((TOP_K * t + bm, d), F32),
        compiler_params=_cparams(("arbitrary",), 48 << 20),
        name="moe_experts",
    )(blk_exp, n_used, row_tok, row_dst, x, w1p, b1g, b1l, w2, b2)


def _route_plan(e, rank, counts, bm):
    t = e.shape[0]
    a = TOP_K * t
    nb = a // bm + N_EXPERTS
    counts = counts.reshape(N_EXPERTS).astype(jnp.int32)
    padded = (counts + bm - 1) // bm * bm
    pend = jnp.cumsum(padded)
    pstart = pend - padded
    onehot = e[..., None] == jnp.arange(N_EXPERTS, dtype=jnp.int32)
    dest = jnp.sum(jnp.where(onehot, pstart, 0), axis=-1) + rank
    aid = jnp.arange(TOP_K, dtype=jnp.int32)[None, :] * t + jnp.arange(t, dtype=jnp.int32)[:, None]
    inv = jnp.full((nb * bm,), a, jnp.int32).at[dest.reshape(-1)].set(aid.reshape(-1), unique_indices=True)
    r = jnp.arange(nb * bm, dtype=jnp.int32)
    is_pad = inv == a
    row_dst = jnp.where(is_pad, a + r % bm, inv).reshape(nb, bm)
    row_tok = jnp.where(is_pad, 0, inv % t).reshape(nb, bm)
    blk_exp = jnp.minimum(jnp.searchsorted(pend, jnp.arange(nb, dtype=jnp.int32) * bm, side="right"),
                          N_EXPERTS - 1).astype(jnp.int32)
    n_used = (pend[-1:] // bm).astype(jnp.int32)
    return blk_exp, n_used, row_tok, row_dst


def _combine_kernel(y0, y1, y2, y3, gate_ref, x_ref, g_ref, b_ref, o_ref):
    gate = gate_ref[...]
    h = None
    for k, y in enumerate((y0, y1, y2, y3)):
        term = gate[:, k:k + 1] * y[...]
        h = term if h is None else h + term
    o_ref[...] = _layer_norm(DEEPNORM_ALPHA * x_ref[...] + h, g_ref[...], b_ref[...])


def _combine(y, gate, x, ln_g, ln_b, row_lo, rows):
    t, d = x.shape
    tm = ROW_TILE
    off = row_lo // tm
    nblk = t // tm
    const = lambda i: (0, 0)
    vec = pl.BlockSpec((1, d), const)

    def y_spec(k):
        return pl.BlockSpec((tm, d), lambda i: (k * nblk + off + i, 0))

    return pl.pallas_call(
        _combine_kernel,
        grid=(rows // tm,),
        in_specs=[y_spec(k) for k in range(TOP_K)]
                 + [pl.BlockSpec((tm, TOP_K), lambda i: (off + i, 0)),
                    pl.BlockSpec((tm, d), lambda i: (off + i, 0)), vec, vec],
        out_specs=pl.BlockSpec((tm, d), lambda i: (i, 0)),
        out_shape=jax.ShapeDtypeStruct((rows, d), F32),
        compiler_params=_cparams(("parallel",), 32 << 20),
        name="moe_combine_ln",
    )(y, y, y, y, gate, x, ln_g, ln_b)


def _moe_block(x, router_w, router_b, w1p, b1g, b1l, w2, b2):
    e, gate, rank, counts = _router(x, router_w, router_b)
    plan = _route_plan(e, rank, counts, EXPERT_ROWS)
    y = _experts(x, *plan, w1p, b1g, b1l, w2, b2)
    return y, gate


def _gelu(z):
    return 0.5 * z * (1.0 + jnp.tanh(0.7978845608028654 * (z + 0.044715 * (z * z * z))))


def _sgu_kernel(x_ref, win_ref, bin_ref, ng_ref, nb_ref, ws_ref, bs_ref, wout_ref, bout_ref,
                g_ref, b_ref, y_ref):
    tm = x_ref.shape[0]
    x = x_ref[...]
    xb = x.astype(BF16)
    dot = functools.partial(jnp.dot, preferred_element_type=F32)
    v = _gelu(dot(xb, win_ref[:, SGU_HALF:]) + bin_ref[:, SGU_HALF:])
    v = _layer_norm(v, ng_ref[...], nb_ref[...]).astype(BF16)
    acc = jnp.zeros((tm, D_MODEL), F32)
    for g in range(SGU_GROUPS):
        cols = slice(g * SGU_GROUP_DIM, (g + 1) * SGU_GROUP_DIM)
        u = _gelu(dot(xb, win_ref[:, cols]) + bin_ref[:, cols])
        mixed = [dot(ws_ref[g], v[c * SGU_CHUNK:(c + 1) * SGU_CHUNK, cols]) + bs_ref[:, g:g + 1]
                 for c in range(tm // SGU_CHUNK)]
        mixed = jnp.concatenate(mixed, axis=0)
        acc = acc + dot((u * mixed).astype(BF16), wout_ref[cols, :])
    h = acc + bout_ref[...]
    y_ref[...] = _layer_norm(DEEPNORM_ALPHA * x + h, g_ref[...], b_ref[...])


def _sgu_block(x, w_in, b_in, norm_g, norm_b, w_s, b_s_t, w_out, b_out, ln_g, ln_b):
    t, d = x.shape
    tm = ROW_TILE
    row = lambda i: (i, 0)
    c2 = lambda i: (0, 0)
    c3 = lambda i: (0, 0, 0)
    once = dict(pipeline_mode=pl.Buffered(1))
    vec = pl.BlockSpec((1, d), c2)
    return pl.pallas_call(
        _sgu_kernel,
        grid=(t // tm,),
        in_specs=[pl.BlockSpec((tm, d), row),
                  pl.BlockSpec((d, 2 * SGU_HALF), c2, **once), pl.BlockSpec((1, 2 * SGU_HALF), c2),
                  pl.BlockSpec((1, SGU_HALF), c2), pl.BlockSpec((1, SGU_HALF), c2),
                  pl.BlockSpec((SGU_GROUPS, SGU_CHUNK, SGU_CHUNK), c3),
                  pl.BlockSpec((SGU_CHUNK, SGU_GROUPS), c2),
                  pl.BlockSpec((SGU_HALF, d), c2, **once), vec, vec, vec],
        out_specs=pl.BlockSpec((tm, d), row),
        out_shape=jax.ShapeDtypeStruct((t, d), F32),
        compiler_params=_cparams(("parallel",), 52 << 20),
        name="sgu_block",
    )(x, w_in, b_in, norm_g, norm_b, w_s, b_s_t, w_out, b_out, ln_g, ln_b)


def _attention_block(x, w_qkv, w_o, b_o, ln_g, ln_b, seq_rows):
    n_groups = N_GROUPS * HEADS
    slopes = (2.0 ** (-8.0 * jnp.arange(1, n_groups + 1, dtype=F32) / n_groups)).reshape(N_GROUPS, HEADS)
    scale = jnp.concatenate([jnp.full((GROUP_DIM,), HEAD_DIM ** -0.5, F32), jnp.ones((2 * GROUP_DIM,), F32)])
    os_, lses = [], []
    for g, (_, dil) in enumerate(DIL_CONFIGS):
        cols = [w_qkv[:, (j * N_GROUPS + g) * GROUP_DIM:(j * N_GROUPS + g + 1) * GROUP_DIM] for j in range(3)]
        w_g = (jnp.concatenate(cols, axis=1) * scale).astype(BF16)
        qkv = _qkv_project(x, w_g, dil)
        segs = tuple((n // dil, length // dil) for n, length in seq_rows)
        o, lse = _attention(qkv, _band_bias(slopes[g], dil), dil, segs)
        t = x.shape[0]
        os_.append(o.reshape(t, GROUP_DIM))
        lses.append(lse.reshape(t, LANES))
    return _merge_project(os_, lses, x, w_o.astype(BF16), b_o[None], ln_g[None], ln_b[None])


def kernel(x_prompt, x_sample, attn_w_qkv, attn_w_o, attn_b_o, sgu_w_in, sgu_b_in, sgu_norm_g, sgu_norm_b,
           sgu_w_s, sgu_b_s, sgu_w_out, sgu_b_out, router_w, router_b, exp_w1, exp_b1, exp_w2, exp_b2,
           ln_mix_g, ln_mix_b, ln_ffn_g, ln_ffn_b):
    d = x_prompt.shape[-1]
    tp = x_prompt.shape[0] * x_prompt.shape[1]
    ts = x_sample.shape[0] * x_sample.shape[1]
    seq_rows = ((tp, x_prompt.shape[1]), (ts, x_sample.shape[1]))
    x = jnp.concatenate([x_prompt.reshape(tp, d), x_sample.reshape(ts, d)], axis=0)

    n_layers = exp_w1.shape[0]
    w1p = lax.bitcast_convert_type(exp_w1.astype(BF16).reshape(n_layers, N_EXPERTS, d, D_EXPERT, 2), jnp.uint32)
    b1g = exp_b1[:, :, None, 0::2]
    b1l = exp_b1[:, :, None, 1::2]
    w2 = exp_w2.astype(BF16)
    b2 = exp_b2[:, :, None, :]

    outs = None
    for i in range(DEPTH):
        j = i // 2
        if i % 2 == 0:
            x = _attention_block(x, attn_w_qkv[j], attn_w_o[j], attn_b_o[j], ln_mix_g[i], ln_mix_b[i], seq_rows)
        else:
            x = _sgu_block(x, sgu_w_in[j].astype(BF16), sgu_b_in[j][None], sgu_norm_g[j][None], sgu_norm_b[j][None],
                           sgu_w_s[j].astype(BF16), jnp.transpose(sgu_b_s[j]), sgu_w_out[j].astype(BF16),
                           sgu_b_out[j][None], ln_mix_g[i][None], ln_mix_b[i][None])
        y, gate = _moe_block(x, router_w[i], router_b[i][None], w1p[i], b1g[i], b1l[i], w2[i], b2[i])
        g_ln, b_ln = ln_ffn_g[i][None], ln_ffn_b[i][None]
        if i + 1 < DEPTH:
            x = _combine(y, gate, x, g_ln, b_ln, 0, tp + ts)
        else:
            outs = (_combine(y, gate, x, g_ln, b_ln, 0, tp).reshape(x_prompt.shape),
                    _combine(y, gate, x, g_ln, b_ln, tp, ts).reshape(x_sample.shape))
    return outs
```

```python
import functools

import jax
import jax.numpy as jnp
from jax import lax
from jax.experimental import pallas as pl
from jax.experimental.pallas import tpu as pltpu

F32 = jnp.float32
BF16 = jnp.bfloat16
U32 = jnp.uint32

LANES = 128
SUBLANES = 8
VMEM_LIMIT_CAP = 56 * 1024 * 1024

D_MODEL = 1024
DIL_CONFIGS = ((128, 1), (512, 4), (2048, 16))
N_GROUPS = len(DIL_CONFIGS)
HEADS = 8
HEAD_DIM = 64
GROUP_DIM = HEADS * HEAD_DIM
N_SIDE = 64
Q_BLOCK = 128
KV_BLOCK = 64
KV_PIECES = (Q_BLOCK + 2 * N_SIDE) // KV_BLOCK
KV_WIN = KV_PIECES * KV_BLOCK
SGU_CHUNK = 128
SGU_HALF = 3 * D_MODEL
SGU_GROUPS = 8
SGU_GROUP_DIM = SGU_HALF // SGU_GROUPS
N_EXPERTS = 32
TOP_K = 4
D_EXPERT = D_MODEL
SWIGLU_ALPHA = 1.702
SWIGLU_LIMIT = 7.0
LN_EPS = 1e-5
DEPTH = 2
DEEPNORM_ALPHA = (2 * DEPTH) ** 0.25
NEG = -1e30

ROW_TILE = 256
EXPERT_ROWS = 256
DMA_UNROLL = 8


def _cparams(semantics, vmem_bytes):
    return pltpu.CompilerParams(dimension_semantics=semantics,
                                vmem_limit_bytes=min(int(vmem_bytes), VMEM_LIMIT_CAP))


def _layer_norm(y, g, b):
    mu = jnp.mean(y, axis=-1, keepdims=True)
    yc = y - mu
    var = jnp.mean(yc * yc, axis=-1, keepdims=True)
    return yc * lax.rsqrt(var + LN_EPS) * g + b


def _qkv_kernel(x_ref, w_ref, o_ref):
    o_ref[...] = jnp.dot(x_ref[...].astype(BF16), w_ref[...],
                         preferred_element_type=F32).astype(BF16)


def _qkv_project(x, w, dil):
    t, d = x.shape
    n = t // dil
    tm = min(512, n)
    width = w.shape[1]
    xv = x.reshape(n, dil * d)
    return pl.pallas_call(
        _qkv_kernel,
        grid=(dil, n // tm),
        in_specs=[pl.BlockSpec((tm, d), lambda r, i: (i, r)),
                  pl.BlockSpec((d, width), lambda r, i: (0, 0))],
        out_specs=pl.BlockSpec((None, tm, width), lambda r, i: (r, i, 0)),
        out_shape=jax.ShapeDtypeStruct((dil, n, width), BF16),
        compiler_params=_cparams(("parallel", "parallel"),
                                 2 * (tm * d * 4 + d * width * 2 + tm * width * 2) + tm * width * 4
                                 + (8 << 20)),
        name=f"qkv_project_d{dil}",
    )(xv, w)


def _attn_kernel(q_ref, *refs, segs):
    k_refs = refs[:KV_PIECES]
    v_refs = refs[KV_PIECES:2 * KV_PIECES]
    bias_ref, o_ref, lse_ref = refs[2 * KV_PIECES:]
    (n_a, len_a), (_, len_b) = segs
    row0 = pl.program_id(1) * Q_BLOCK
    in_a = row0 < n_a
    lo = jnp.where(in_a, (row0 // len_a) * len_a, n_a + ((row0 - n_a) // len_b) * len_b)
    hi = lo + jnp.where(in_a, len_a, len_b)
    krow = lax.broadcasted_iota(jnp.int32, (1, KV_WIN), 1) + (row0 - N_SIDE)
    col_bias = jnp.where((krow >= lo) & (krow < hi), 0.0, NEG).astype(F32)
    lane = lax.broadcasted_iota(jnp.int32, (1, LANES), 1)
    first_head = lane < HEAD_DIM
    lse_tile = jnp.zeros((Q_BLOCK, LANES), F32)
    nt = (((1,), (1,)), ((), ()))
    for p in range(HEADS // 2):
        sl = slice(p * LANES, (p + 1) * LANES)
        q = q_ref[:, sl]
        k = jnp.concatenate([r[:, sl] for r in k_refs], axis=0)
        v = jnp.concatenate([r[:, sl] for r in v_refs], axis=0)
        zero = jnp.zeros_like(q)
        outs, inv_ls = [], []
        for j, qh in enumerate((jnp.where(first_head, q, zero), jnp.where(first_head, zero, q))):
            h = 2 * p + j
            s = lax.dot_general(qh, k, nt, preferred_element_type=F32)
            s = s + bias_ref[h] + col_bias
            m = jnp.max(s, axis=-1, keepdims=True)
            e = jnp.exp(s - m)
            l = jnp.sum(e, axis=-1, keepdims=True)
            outs.append(jnp.dot(e.astype(BF16), v, preferred_element_type=F32))
            inv_ls.append(1.0 / l)
            lse_tile = jnp.where(lane == h, m + jnp.log(l), lse_tile)
        o = jnp.where(first_head, outs[0] * inv_ls[0], outs[1] * inv_ls[1])
        o_ref[:, sl] = o.astype(BF16)
    lse_ref[...] = lse_tile


def _band_bias(slopes, dil):
    qi = jnp.arange(Q_BLOCK)[:, None]
    kc = jnp.arange(KV_WIN)[None, :]
    dist = jnp.abs(kc - N_SIDE - qi)
    pen = -slopes[:, None, None] * (dist * dil).astype(F32)[None]
    return jnp.where((dist <= N_SIDE)[None], pen, NEG).astype(F32)


def _attention(qkv, bias, dil, segs):
    _, n, _ = qkv.shape
    nkb = n // KV_BLOCK

    def kv_spec(c, col):
        def imap(r, i):
            return (r, jnp.clip(2 * i - 1 + c, 0, nkb - 1), col)
        return pl.BlockSpec((None, KV_BLOCK, GROUP_DIM), imap)

    in_specs = ([pl.BlockSpec((None, Q_BLOCK, GROUP_DIM), lambda r, i: (r, i, 0))]
                + [kv_spec(c, 1) for c in range(KV_PIECES)]
                + [kv_spec(c, 2) for c in range(KV_PIECES)]
                + [pl.BlockSpec((HEADS, Q_BLOCK, KV_WIN), lambda r, i: (0, 0, 0))])
    return pl.pallas_call(
        functools.partial(_attn_kernel, segs=segs),
        grid=(dil, n // Q_BLOCK),
        in_specs=in_specs,
        out_specs=[pl.BlockSpec((Q_BLOCK, GROUP_DIM), lambda r, i: (i, r)),
                   pl.BlockSpec((Q_BLOCK, LANES), lambda r, i: (i, r))],
        out_shape=[jax.ShapeDtypeStruct((n, dil * GROUP_DIM), BF16),
                   jax.ShapeDtypeStruct((n, dil * LANES), F32)],
        compiler_params=_cparams(("parallel", "parallel"), 32 << 20),
        name=f"band_attention_d{dil}",
    )(qkv, *([qkv] * (2 * KV_PIECES)), bias)


def _merge_kernel(o0, o1, o2, l0, l1, l2, x_ref, wo_ref, bo_ref, g_ref, b_ref, y_ref):
    lses = [l0[...], l1[...], l2[...]]
    m = jnp.maximum(jnp.maximum(lses[0], lses[1]), lses[2])
    es = [jnp.exp(l - m) for l in lses]
    inv = 1.0 / (es[0] + es[1] + es[2])
    ws = [e * inv for e in es]
    lane = lax.broadcasted_iota(jnp.int32, (1, LANES), 1)
    first_head = lane < HEAD_DIM
    parts = []
    for p in range(HEADS // 2):
        sl = slice(p * LANES, (p + 1) * LANES)
        acc = None
        for w, o in zip(ws, (o0, o1, o2)):
            wp = jnp.where(first_head, w[:, 2 * p:2 * p + 1], w[:, 2 * p + 1:2 * p + 2])
            term = wp * o[:, sl].astype(F32)
            acc = term if acc is None else acc + term
        parts.append(acc)
    merged = jnp.concatenate(parts, axis=-1).astype(BF16)
    h = jnp.dot(merged, wo_ref[...], preferred_element_type=F32) + bo_ref[...]
    y_ref[...] = _layer_norm(DEEPNORM_ALPHA * x_ref[...] + h, g_ref[...], b_ref[...])


def _merge_project(os_, lses, x, w_o, b_o, ln_g, ln_b):
    t, d = x.shape
    tm = ROW_TILE
    row = lambda i: (i, 0)
    const = lambda i: (0, 0)
    vec = pl.BlockSpec((1, d), const)
    return pl.pallas_call(
        _merge_kernel,
        grid=(t // tm,),
        in_specs=([pl.BlockSpec((tm, GROUP_DIM), row)] * 3 + [pl.BlockSpec((tm, LANES), row)] * 3
                  + [pl.BlockSpec((tm, d), row), pl.BlockSpec((GROUP_DIM, d), const), vec, vec, vec]),
        out_specs=pl.BlockSpec((tm, d), row),
        out_shape=jax.ShapeDtypeStruct((t, d), F32),
        compiler_params=_cparams(("parallel",), 32 << 20),
        name="merge_outproj_ln",
    )(*os_, *lses, x, w_o, b_o, ln_g, ln_b)


def _split_bf16(a):
    hi = a.astype(BF16)
    return hi, (a - hi.astype(F32)).astype(BF16)


def _router_kernel(x_ref, w_ref, b_ref, e_ref, g_ref, r_ref, cnt_ref, carry_ref):
    tm = x_ref.shape[0]

    @pl.when(pl.program_id(0) == 0)
    def _():
        carry_ref[...] = jnp.zeros_like(carry_ref)

    xh, xl = _split_bf16(x_ref[...])
    wh, wl = _split_bf16(w_ref[...])
    dot = functools.partial(jnp.dot, preferred_element_type=F32)
    logits = dot(xh, wh) + (dot(xl, wh) + dot(xh, wl)) + b_ref[...]
    lane = lax.broadcasted_iota(jnp.int32, logits.shape, 1)
    col4 = lax.broadcasted_iota(jnp.int32, (tm, TOP_K), 1)
    work = logits
    picks, vals, idxs = [], [], []
    for _k in range(TOP_K):
        mx = jnp.max(work, axis=-1, keepdims=True)
        idx = jnp.min(jnp.where(work == mx, lane, N_EXPERTS), axis=-1, keepdims=True)
        pick = lane == idx
        work = jnp.where(pick, -jnp.inf, work)
        picks.append(pick)
        vals.append(mx)
        idxs.append(idx)
    ex = [jnp.exp(v - vals[0]) for v in vals]
    inv = 1.0 / (ex[0] + ex[1] + ex[2] + ex[3])
    chosen = jnp.zeros(logits.shape, F32)
    for pick in picks:
        chosen = chosen + pick.astype(F32)
    r_i = lax.broadcasted_iota(jnp.int32, (tm, tm), 0)
    c_i = lax.broadcasted_iota(jnp.int32, (tm, tm), 1)
    tri = (c_i < r_i).astype(BF16)
    before = dot(tri, chosen.astype(BF16)) + carry_ref[...]
    e_out = jnp.zeros((tm, TOP_K), jnp.int32)
    g_out = jnp.zeros((tm, TOP_K), F32)
    r_out = jnp.zeros((tm, TOP_K), jnp.int32)
    for k in range(TOP_K):
        rank = jnp.sum(jnp.where(picks[k], before, 0.0), axis=-1, keepdims=True)
        e_out = jnp.where(col4 == k, idxs[k], e_out)
        g_out = jnp.where(col4 == k, ex[k] * inv, g_out)
        r_out = jnp.where(col4 == k, rank.astype(jnp.int32), r_out)
    e_ref[...] = e_out
    g_ref[...] = g_out
    r_ref[...] = r_out
    carry_ref[...] += jnp.sum(chosen, axis=0, keepdims=True)
    cnt_ref[...] = carry_ref[...]


def _router(x, w, b):
    t, d = x.shape
    tm = ROW_TILE
    row = lambda i: (i, 0)
    const = lambda i: (0, 0)
    k_spec = pl.BlockSpec((tm, TOP_K), row)
    return pl.pallas_call(
        _router_kernel,
        grid=(t // tm,),
        in_specs=[pl.BlockSpec((tm, d), row), pl.BlockSpec((d, N_EXPERTS), const),
                  pl.BlockSpec((1, N_EXPERTS), const)],
        out_specs=[k_spec, k_spec, k_spec, pl.BlockSpec((1, N_EXPERTS), const)],
        out_shape=[jax.ShapeDtypeStruct((t, TOP_K), jnp.int32), jax.ShapeDtypeStruct((t, TOP_K), F32),
                   jax.ShapeDtypeStruct((t, TOP_K), jnp.int32), jax.ShapeDtypeStruct((1, N_EXPERTS), F32)],
        scratch_shapes=[pltpu.VMEM((1, N_EXPERTS), F32)],
        compiler_params=_cparams(("arbitrary",), 32 << 20),
        name="router_topk",
    )(x, w, b)


def _route_plan(e, rank, counts, bm, tm):
    t = e.shape[0]
    nb = TOP_K * t // bm + N_EXPERTS
    counts = counts.reshape(N_EXPERTS).astype(jnp.int32)
    padded = (counts + bm - 1) // bm * bm
    pend = jnp.cumsum(padded)
    pstart = pend - padded
    onehot = e[..., None] == jnp.arange(N_EXPERTS, dtype=jnp.int32)
    dest = jnp.sum(jnp.where(onehot, pstart, 0), axis=-1) + rank
    blk_row0 = jnp.arange(nb, dtype=jnp.int32) * bm
    blk_exp = jnp.minimum(jnp.sum((pend[None, :] <= blk_row0[:, None]).astype(jnp.int32), axis=1),
                          N_EXPERTS - 1)
    n_used = pend[-1:] // bm
    return dest.reshape(t // tm, tm * TOP_K), blk_exp, n_used, pstart + counts, pend


ROW_SUBLANES = D_MODEL // LANES


def _store_row_tiles(ref, base, x):
    for c in range(ROW_SUBLANES):
        ref[pl.ds(base + c, x.shape[0], stride=ROW_SUBLANES), :] = x[:, c * LANES:(c + 1) * LANES]


def _load_row_tiles(ref, base, n):
    return jnp.concatenate([ref[pl.ds(base + c, n, stride=ROW_SUBLANES), :] for c in range(ROW_SUBLANES)], axis=1)


def _tile_rows(ref, row):
    return ref.at[pl.ds(pl.multiple_of(row * ROW_SUBLANES, ROW_SUBLANES), ROW_SUBLANES), :]


def _dispatch_kernel(padlo_ref, pend_ref, nused_ref, dest_hbm, x_ref, xs_hbm, idx0, idx1, pk, zbuf, sems):
    i = pl.program_id(0)
    n = pl.num_programs(0)
    tm = x_ref.shape[0]
    bm = zbuf.shape[0] // ROW_SUBLANES
    dump = xs_hbm.shape[0] // ROW_SUBLANES - bm
    idx = (idx0, idx1)
    slot_rows = tm * ROW_SUBLANES

    def idx_copy(step, s):
        return pltpu.make_async_copy(dest_hbm.at[step], idx[s], sems.at[s])

    def rows_wait(s):
        for _ in range(TOP_K):
            pltpu.make_async_copy(pk.at[pl.ds(s * slot_rows, slot_rows), :], xs_hbm.at[pl.ds(0, slot_rows), :],
                                  sems.at[2 + s]).wait()

    @pl.when(i == 0)
    def _():
        idx_copy(0, 0).start()
        zbuf[...] = jnp.zeros_like(zbuf)

        def zero_block(b, c):
            cp = pltpu.make_async_copy(zbuf, xs_hbm.at[pl.ds(pl.multiple_of(b * zbuf.shape[0], ROW_SUBLANES),
                                                             zbuf.shape[0]), :], sems.at[4])
            cp.start()
            cp.wait()
            return c
        lax.fori_loop(nused_ref[0], dump // bm + 1, zero_block, 0)
        for e in range(N_EXPERTS):
            def zero_row(j, c, e=e):
                r = padlo_ref[e] + j
                r = jnp.where(r < pend_ref[e], r, dump + j)
                pltpu.make_async_copy(_tile_rows(zbuf, j), _tile_rows(xs_hbm, r), sems.at[4]).start()
                return c
            lax.fori_loop(0, bm, zero_row, 0, unroll=DMA_UNROLL)
            pltpu.make_async_copy(zbuf, xs_hbm.at[pl.ds(0, zbuf.shape[0]), :], sems.at[4]).wait()

    def step(s):
        idx_copy(i, s).wait()

        @pl.when(i + 1 < n)
        def _():
            idx_copy(i + 1, 1 - s).start()

        @pl.when(i >= 2)
        def _():
            rows_wait(s)

        _store_row_tiles(pk, s * slot_rows, x_ref[...])

        def issue(j, c):
            for k in range(TOP_K):
                pltpu.make_async_copy(_tile_rows(pk, s * tm + j), _tile_rows(xs_hbm, idx[s][j * TOP_K + k]),
                                      sems.at[2 + s]).start()
            return c
        lax.fori_loop(0, tm, issue, 0, unroll=DMA_UNROLL)

        @pl.when(i == n - 1)
        def _():
            rows_wait(s)

        @pl.when(jnp.logical_and(i == n - 1, n >= 2))
        def _():
            rows_wait(1 - s)

    for s in range(2):
        pl.when(i % 2 == s)(functools.partial(step, s))


def _dispatch(x, dest, pad_lo, pend, n_used, bm):
    t, d = x.shape
    nblk, per = dest.shape
    tm = per // TOP_K
    nb = TOP_K * t // bm + N_EXPERTS
    grid_spec = pltpu.PrefetchScalarGridSpec(
        num_scalar_prefetch=3,
        grid=(nblk,),
        in_specs=[pl.BlockSpec(memory_space=pl.ANY), pl.BlockSpec((tm, d), lambda i, a, b, c: (i, 0))],
        out_specs=pl.BlockSpec(memory_space=pl.ANY),
        scratch_shapes=[pltpu.SMEM((per,), jnp.int32), pltpu.SMEM((per,), jnp.int32),
                        pltpu.VMEM((2 * tm * ROW_SUBLANES, LANES), F32),
                        pltpu.VMEM((bm * ROW_SUBLANES, LANES), F32),
                        pltpu.SemaphoreType.DMA((5,))])
    return pl.pallas_call(
        _dispatch_kernel,
        grid_spec=grid_spec,
        out_shape=jax.ShapeDtypeStruct(((nb * bm + bm) * ROW_SUBLANES, LANES), F32),
        compiler_params=_cparams(("arbitrary",), 32 << 20),
        name="moe_dispatch",
    )(pad_lo, pend, n_used, dest, x)


def _expert_kernel(be_ref, nused_ref, xs_ref, w1g_ref, w1l_ref, b1g_ref, b1l_ref, w2_ref, b2_ref, y_ref):
    i = pl.program_id(0)

    @pl.when(i < nused_ref[0])
    def _():
        bm = xs_ref.shape[0] // ROW_SUBLANES
        x = _load_row_tiles(xs_ref, 0, bm).astype(BF16)
        hg = jnp.dot(x, w1g_ref[...], preferred_element_type=F32) + b1g_ref[...]
        hl = jnp.dot(x, w1l_ref[...], preferred_element_type=F32) + b1l_ref[...]
        hg = jnp.minimum(hg, SWIGLU_LIMIT)
        hl = jnp.clip(hl, -SWIGLU_LIMIT, SWIGLU_LIMIT)
        act = hg * (1.0 / (1.0 + jnp.exp(-SWIGLU_ALPHA * hg))) * (hl + 1.0)
        y = jnp.dot(act.astype(BF16), w2_ref[...], preferred_element_type=F32) + b2_ref[...]
        _store_row_tiles(y_ref, 0, y)

    @pl.when(i >= nused_ref[0])
    def _():
        y_ref[...] = jnp.zeros_like(y_ref)


def _experts(xs, blk_exp, n_used, layer, w1g, w1l, b1g, b1l, w2, b2, bm):
    d = ROW_SUBLANES * LANES
    nb = blk_exp.shape[0]
    de = w2.shape[2]
    blk = (bm * ROW_SUBLANES, LANES)
    wmap = lambda i, be, nu: (layer, be[i], 0, 0)
    grid_spec = pltpu.PrefetchScalarGridSpec(
        num_scalar_prefetch=2,
        grid=(nb,),
        in_specs=[pl.BlockSpec(blk, lambda i, be, nu: (jnp.minimum(i, nu[0] - 1), 0)),
                  pl.BlockSpec((None, None, d, de), wmap), pl.BlockSpec((None, None, d, de), wmap),
                  pl.BlockSpec((None, None, 1, de), wmap), pl.BlockSpec((None, None, 1, de), wmap),
                  pl.BlockSpec((None, None, de, d), wmap), pl.BlockSpec((None, None, 1, d), wmap)],
        out_specs=pl.BlockSpec(blk, lambda i, be, nu: (i, 0)))
    return pl.pallas_call(
        _expert_kernel,
        grid_spec=grid_spec,
        out_shape=jax.ShapeDtypeStruct((nb * blk[0], LANES), F32),
        compiler_params=_cparams(("arbitrary",), 48 << 20),
        name="moe_experts",
    )(blk_exp, n_used, xs, w1g, w1l, b1g, b1l, w2, b2)


def _combine_kernel(dest_hbm, gate_ref, x_ref, g_ref, b_ref, y_hbm, o_ref, idx0, idx1, ybuf, sems, *, blk_off):
    i = pl.program_id(0)
    n = pl.num_programs(0)
    tm = x_ref.shape[0]
    idx = (idx0, idx1)
    slab_rows = tm * ROW_SUBLANES

    def slab(s, k):
        return (s * TOP_K + k) * tm

    def idx_copy(step, s):
        return pltpu.make_async_copy(dest_hbm.at[blk_off + step], idx[s], sems.at[s])

    def issue_rows(s):
        def body(j, c):
            for k in range(TOP_K):
                pltpu.make_async_copy(_tile_rows(y_hbm, idx[s][j * TOP_K + k]), _tile_rows(ybuf, slab(s, k) + j),
                                      sems.at[2 + s]).start()
            return c
        lax.fori_loop(0, tm, body, 0, unroll=DMA_UNROLL)

    @pl.when(i == 0)
    def _():
        cp = idx_copy(0, 0)
        cp.start()
        cp.wait()
        issue_rows(0)

    @pl.when(jnp.logical_and(i == 0, n >= 2))
    def _():
        idx_copy(1, 1).start()

    def step(s):
        @pl.when(i + 1 < n)
        def _():
            idx_copy(i + 1, 1 - s).wait()
            issue_rows(1 - s)

        @pl.when(i + 2 < n)
        def _():
            idx_copy(i + 2, s).start()

        for k in range(TOP_K):
            pltpu.make_async_copy(y_hbm.at[pl.ds(0, slab_rows), :],
                                  ybuf.at[pl.ds(slab(s, k) * ROW_SUBLANES, slab_rows), :], sems.at[2 + s]).wait()
        gate = gate_ref[...]
        h = None
        for k in range(TOP_K):
            term = gate[:, k:k + 1] * _load_row_tiles(ybuf, slab(s, k) * ROW_SUBLANES, tm)
            h = term if h is None else h + term
        o_ref[...] = _layer_norm(DEEPNORM_ALPHA * x_ref[...] + h, g_ref[...], b_ref[...])

    for s in range(2):
        pl.when(i % 2 == s)(functools.partial(step, s))


def _combine(y, dest, gate, x, ln_g, ln_b, row_lo, rows):
    _, d = x.shape
    per = dest.shape[1]
    tm = per // TOP_K
    off = row_lo // tm
    vec = pl.BlockSpec((1, d), lambda i: (0, 0))
    return pl.pallas_call(
        functools.partial(_combine_kernel, blk_off=off),
        grid=(rows // tm,),
        in_specs=[pl.BlockSpec(memory_space=pl.ANY),
                  pl.BlockSpec((tm, TOP_K), lambda i: (off + i, 0)),
                  pl.BlockSpec((tm, d), lambda i: (off + i, 0)), vec, vec,
                  pl.BlockSpec(memory_space=pl.ANY)],
        out_specs=pl.BlockSpec((tm, d), lambda i: (i, 0)),
        out_shape=jax.ShapeDtypeStruct((rows, d), F32),
        scratch_shapes=[pltpu.SMEM((per,), jnp.int32), pltpu.SMEM((per,), jnp.int32),
                        pltpu.VMEM((2 * TOP_K * tm * ROW_SUBLANES, LANES), F32),
                        pltpu.SemaphoreType.DMA((4,))],
        compiler_params=_cparams(("arbitrary",), 40 << 20),
        name="moe_combine_ln",
    )(dest, gate, x, ln_g, ln_b, y)


def _moe_block(x, layer, router_w, router_b, w1g, w1l, b1g, b1l, w2, b2):
    e, gate, rank, counts = _router(x, router_w, router_b)
    dest, blk_exp, n_used, pad_lo, pend = _route_plan(e, rank, counts, EXPERT_ROWS, ROW_TILE)
    xs = _dispatch(x, dest, pad_lo, pend, n_used, EXPERT_ROWS)
    y = _experts(xs, blk_exp, n_used, layer, w1g, w1l, b1g, b1l, w2, b2, EXPERT_ROWS)
    return y, dest, gate


def _gelu(z):
    return 0.5 * z * (1.0 + jnp.tanh(0.7978845608028654 * (z + 0.044715 * (z * z * z))))


def _sgu_kernel(x_ref, win_ref, bin_ref, ng_ref, nb_ref, ws_ref, bs_ref, wout_ref, bout_ref,
                g_ref, b_ref, y_ref):
    tm = x_ref.shape[0]
    x = x_ref[...]
    xb = x.astype(BF16)
    dot = functools.partial(jnp.dot, preferred_element_type=F32)
    v = _gelu(dot(xb, win_ref[:, SGU_HALF:]) + bin_ref[:, SGU_HALF:])
    v = _layer_norm(v, ng_ref[...], nb_ref[...]).astype(BF16)
    acc = jnp.zeros((tm, D_MODEL), F32)
    for g in range(SGU_GROUPS):
        cols = slice(g * SGU_GROUP_DIM, (g + 1) * SGU_GROUP_DIM)
        u = _gelu(dot(xb, win_ref[:, cols]) + bin_ref[:, cols])
        mixed = [dot(ws_ref[g], v[c * SGU_CHUNK:(c + 1) * SGU_CHUNK, cols]) + bs_ref[:, g:g + 1]
                 for c in range(tm // SGU_CHUNK)]
        mixed = jnp.concatenate(mixed, axis=0)
        acc = acc + dot((u * mixed).astype(BF16), wout_ref[cols, :])
    h = acc + bout_ref[...]
    y_ref[...] = _layer_norm(DEEPNORM_ALPHA * x + h, g_ref[...], b_ref[...])


def _sgu_block(x, w_in, b_in, norm_g, norm_b, w_s, b_s_t, w_out, b_out, ln_g, ln_b):
    t, d = x.shape
    tm = ROW_TILE
    row = lambda i: (i, 0)
    c2 = lambda i: (0, 0)
    c3 = lambda i: (0, 0, 0)
    once = dict(pipeline_mode=pl.Buffered(1))
    vec = pl.BlockSpec((1, d), c2)
    return pl.pallas_call(
        _sgu_kernel,
        grid=(t // tm,),
        in_specs=[pl.BlockSpec((tm, d), row),
                  pl.BlockSpec((d, 2 * SGU_HALF), c2, **once), pl.BlockSpec((1, 2 * SGU_HALF), c2),
                  pl.BlockSpec((1, SGU_HALF), c2), pl.BlockSpec((1, SGU_HALF), c2),
                  pl.BlockSpec((SGU_GROUPS, SGU_CHUNK, SGU_CHUNK), c3),
                  pl.BlockSpec((SGU_CHUNK, SGU_GROUPS), c2),
                  pl.BlockSpec((SGU_HALF, d), c2, **once), vec, vec, vec],
        out_specs=pl.BlockSpec((tm, d), row),
        out_shape=jax.ShapeDtypeStruct((t, d), F32),
        compiler_params=_cparams(("parallel",), 52 << 20),
        name="sgu_block",
    )(x, w_in, b_in, norm_g, norm_b, w_s, b_s_t, w_out, b_out, ln_g, ln_b)


def _attention_block(x, w_qkv, w_o, b_o, ln_g, ln_b, seq_rows):
    n_groups = N_GROUPS * HEADS
    slopes = (2.0 ** (-8.0 * jnp.arange(1, n_groups + 1, dtype=F32) / n_groups)).reshape(N_GROUPS, HEADS)
    scale = jnp.concatenate([jnp.full((GROUP_DIM,), HEAD_DIM ** -0.5, F32), jnp.ones((2 * GROUP_DIM,), F32)])
    os_, lses = [], []
    for g, (_, dil) in enumerate(DIL_CONFIGS):
        cols = [w_qkv[:, (j * N_GROUPS + g) * GROUP_DIM:(j * N_GROUPS + g + 1) * GROUP_DIM] for j in range(3)]
        w_g = (jnp.concatenate(cols, axis=1) * scale).astype(BF16)
        qkv = _qkv_project(x, w_g, dil)
        segs = tuple((n // dil, length // dil) for n, length in seq_rows)
        o, lse = _attention(qkv, _band_bias(slopes[g], dil), dil, segs)
        t = x.shape[0]
        os_.append(o.reshape(t, GROUP_DIM))
        lses.append(lse.reshape(t, LANES))
    return _merge_project(os_, lses, x, w_o.astype(BF16), b_o[None], ln_g[None], ln_b[None])


def kernel(x_prompt, x_sample, attn_w_qkv, attn_w_o, attn_b_o, sgu_w_in, sgu_b_in, sgu_norm_g, sgu_norm_b,
           sgu_w_s, sgu_b_s, sgu_w_out, sgu_b_out, router_w, router_b, exp_w1, exp_b1, exp_w2, exp_b2,
           ln_mix_g, ln_mix_b, ln_ffn_g, ln_ffn_b):
    d = x_prompt.shape[-1]
    tp = x_prompt.shape[0] * x_prompt.shape[1]
    ts = x_sample.shape[0] * x_sample.shape[1]
    seq_rows = ((tp, x_prompt.shape[1]), (ts, x_sample.shape[1]))
    x = jnp.concatenate([x_prompt.reshape(tp, d), x_sample.reshape(ts, d)], axis=0)

    w1g = exp_w1[..., 0::2].astype(BF16)
    w1l = exp_w1[..., 1::2].astype(BF16)
    b1g = exp_b1[:, :, None, 0::2]
    b1l = exp_b1[:, :, None, 1::2]
    w2 = exp_w2.astype(BF16)
    b2 = exp_b2[:, :, None, :]

    outs = None
    for i in range(DEPTH):
        j = i // 2
        if i % 2 == 0:
            x = _attention_block(x, attn_w_qkv[j], attn_w_o[j], attn_b_o[j], ln_mix_g[i], ln_mix_b[i], seq_rows)
        else:
            x = _sgu_block(x, sgu_w_in[j].astype(BF16), sgu_b_in[j][None], sgu_norm_g[j][None], sgu_norm_b[j][None],
                           sgu_w_s[j].astype(BF16), jnp.transpose(sgu_b_s[j]), sgu_w_out[j].astype(BF16),
                           sgu_b_out[j][None], ln_mix_g[i][None], ln_mix_b[i][None])
        y, dest, gate = _moe_block(x, i, router_w[i], router_b[i][None], w1g, w1l, b1g, b1l, w2, b2)
        g_ln, b_ln = ln_ffn_g[i][None], ln_ffn_b[i][None]
        if i + 1 < DEPTH:
            x = _combine(y, dest, gate, x, g_ln, b_ln, 0, tp + ts)
        else:
            outs = (_combine(y, dest, gate, x, g_ln, b_ln, 0, tp).reshape(x_prompt.shape),
                    _combine(y, dest, gate, x, g_ln, b_ln, tp, ts).reshape(x_sample.shape))
    return outs
```

```python
import functools

import jax
import jax.numpy as jnp
from jax import lax
from jax.experimental import pallas as pl
from jax.experimental.pallas import tpu as pltpu

F32 = jnp.float32
BF16 = jnp.bfloat16
U32 = jnp.uint32

LANES = 128
SUBLANES = 8
VMEM_LIMIT_CAP = 56 * 1024 * 1024

D_MODEL = 1024
DIL_CONFIGS = ((128, 1), (512, 4), (2048, 16))
N_GROUPS = len(DIL_CONFIGS)
HEADS = 8
HEAD_DIM = 64
GROUP_DIM = HEADS * HEAD_DIM
N_SIDE = 64
Q_BLOCK = 128
KV_BLOCK = 64
KV_PIECES = (Q_BLOCK + 2 * N_SIDE) // KV_BLOCK
KV_WIN = KV_PIECES * KV_BLOCK
SGU_CHUNK = 128
SGU_HALF = 3 * D_MODEL
SGU_GROUPS = 8
SGU_GROUP_DIM = SGU_HALF // SGU_GROUPS
N_EXPERTS = 32
TOP_K = 4
D_EXPERT = D_MODEL
SWIGLU_ALPHA = 1.702
SWIGLU_LIMIT = 7.0
LN_EPS = 1e-5
DEPTH = 2
DEEPNORM_ALPHA = (2 * DEPTH) ** 0.25
NEG = -1e30

ROW_TILE = 256
EXPERT_ROWS = 256
DMA_UNROLL = 8


def _cparams(semantics, vmem_bytes):
    return pltpu.CompilerParams(dimension_semantics=semantics,
                                vmem_limit_bytes=min(int(vmem_bytes), VMEM_LIMIT_CAP))


def _layer_norm(y, g, b):
    mu = jnp.mean(y, axis=-1, keepdims=True)
    yc = y - mu
    var = jnp.mean(yc * yc, axis=-1, keepdims=True)
    return yc * lax.rsqrt(var + LN_EPS) * g + b


def _qkv_kernel(x_ref, w_ref, o_ref):
    o_ref[...] = jnp.dot(x_ref[...].astype(BF16), w_ref[...],
                         preferred_element_type=F32).astype(BF16)


def _qkv_project(x, w, dil):
    t, d = x.shape
    n = t // dil
    tm = min(512, n)
    width = w.shape[1]
    xv = x.reshape(n, dil * d)
    return pl.pallas_call(
        _qkv_kernel,
        grid=(dil, n // tm),
        in_specs=[pl.BlockSpec((tm, d), lambda r, i: (i, r)),
                  pl.BlockSpec((d, width), lambda r, i: (0, 0))],
        out_specs=pl.BlockSpec((None, tm, width), lambda r, i: (r, i, 0)),
        out_shape=jax.ShapeDtypeStruct((dil, n, width), BF16),
        compiler_params=_cparams(("parallel", "parallel"),
                                 2 * (tm * d * 4 + d * width * 2 + tm * width * 2) + tm * width * 4
                                 + (8 << 20)),
        name=f"qkv_project_d{dil}",
    )(xv, w)


def _attn_kernel(q_ref, *refs, segs):
    k_refs = refs[:KV_PIECES]
    v_refs = refs[KV_PIECES:2 * KV_PIECES]
    bias_ref, o_ref, lse_ref = refs[2 * KV_PIECES:]
    (n_a, len_a), (_, len_b) = segs
    row0 = pl.program_id(1) * Q_BLOCK
    in_a = row0 < n_a
    lo = jnp.where(in_a, (row0 // len_a) * len_a, n_a + ((row0 - n_a) // len_b) * len_b)
    hi = lo + jnp.where(in_a, len_a, len_b)
    krow = lax.broadcasted_iota(jnp.int32, (1, KV_WIN), 1) + (row0 - N_SIDE)
    col_bias = jnp.where((krow >= lo) & (krow < hi), 0.0, NEG).astype(F32)
    lane = lax.broadcasted_iota(jnp.int32, (1, LANES), 1)
    first_head = lane < HEAD_DIM
    lse_tile = jnp.zeros((Q_BLOCK, LANES), F32)
    nt = (((1,), (1,)), ((), ()))
    for p in range(HEADS // 2):
        sl = slice(p * LANES, (p + 1) * LANES)
        q = q_ref[:, sl]
        k = jnp.concatenate([r[:, sl] for r in k_refs], axis=0)
        v = jnp.concatenate([r[:, sl] for r in v_refs], axis=0)
        zero = jnp.zeros_like(q)
        outs, inv_ls = [], []
        for j, qh in enumerate((jnp.where(first_head, q, zero), jnp.where(first_head, zero, q))):
            h = 2 * p + j
            s = lax.dot_general(qh, k, nt, preferred_element_type=F32)
            s = s + bias_ref[h] + col_bias
            m = jnp.max(s, axis=-1, keepdims=True)
            e = jnp.exp(s - m)
            l = jnp.sum(e, axis=-1, keepdims=True)
            outs.append(jnp.dot(e.astype(BF16), v, preferred_element_type=F32))
            inv_ls.append(1.0 / l)
            lse_tile = jnp.where(lane == h, m + jnp.log(l), lse_tile)
        o = jnp.where(first_head, outs[0] * inv_ls[0], outs[1] * inv_ls[1])
        o_ref[:, sl] = o.astype(BF16)
    lse_ref[...] = lse_tile


def _band_bias(slopes, dil):
    qi = jnp.arange(Q_BLOCK)[:, None]
    kc = jnp.arange(KV_WIN)[None, :]
    dist = jnp.abs(kc - N_SIDE - qi)
    pen = -slopes[:, None, None] * (dist * dil).astype(F32)[None]
    return jnp.where((dist <= N_SIDE)[None], pen, NEG).astype(F32)


def _attention(qkv, bias, dil, segs):
    _, n, _ = qkv.shape
    nkb = n // KV_BLOCK

    def kv_spec(c, col):
        def imap(r, i):
            return (r, jnp.clip(2 * i - 1 + c, 0, nkb - 1), col)
        return pl.BlockSpec((None, KV_BLOCK, GROUP_DIM), imap)

    in_specs = ([pl.BlockSpec((None, Q_BLOCK, GROUP_DIM), lambda r, i: (r, i, 0))]
                + [kv_spec(c, 1) for c in range(KV_PIECES)]
                + [kv_spec(c, 2) for c in range(KV_PIECES)]
                + [pl.BlockSpec((HEADS, Q_BLOCK, KV_WIN), lambda r, i: (0, 0, 0))])
    return pl.pallas_call(
        functools.partial(_attn_kernel, segs=segs),
        grid=(dil, n // Q_BLOCK),
        in_specs=in_specs,
        out_specs=[pl.BlockSpec((Q_BLOCK, GROUP_DIM), lambda r, i: (i, r)),
                   pl.BlockSpec((Q_BLOCK, LANES), lambda r, i: (i, r))],
        out_shape=[jax.ShapeDtypeStruct((n, dil * GROUP_DIM), BF16),
                   jax.ShapeDtypeStruct((n, dil * LANES), F32)],
        compiler_params=_cparams(("parallel", "parallel"), 32 << 20),
        name=f"band_attention_d{dil}",
    )(qkv, *([qkv] * (2 * KV_PIECES)), bias)


def _merge_kernel(o0, o1, o2, l0, l1, l2, x_ref, wo_ref, bo_ref, g_ref, b_ref, y_ref):
    lses = [l0[...], l1[...], l2[...]]
    m = jnp.maximum(jnp.maximum(lses[0], lses[1]), lses[2])
    es = [jnp.exp(l - m) for l in lses]
    inv = 1.0 / (es[0] + es[1] + es[2])
    ws = [e * inv for e in es]
    lane = lax.broadcasted_iota(jnp.int32, (1, LANES), 1)
    first_head = lane < HEAD_DIM
    parts = []
    for p in range(HEADS // 2):
        sl = slice(p * LANES, (p + 1) * LANES)
        acc = None
        for w, o in zip(ws, (o0, o1, o2)):
            wp = jnp.where(first_head, w[:, 2 * p:2 * p + 1], w[:, 2 * p + 1:2 * p + 2])
            term = wp * o[:, sl].astype(F32)
            acc = term if acc is None else acc + term
        parts.append(acc)
    merged = jnp.concatenate(parts, axis=-1).astype(BF16)
    h = jnp.dot(merged, wo_ref[...], preferred_element_type=F32) + bo_ref[...]
    y_ref[...] = _layer_norm(DEEPNORM_ALPHA * x_ref[...] + h, g_ref[...], b_ref[...])


def _merge_project(os_, lses, x, w_o, b_o, ln_g, ln_b):
    t, d = x.shape
    tm = ROW_TILE
    row = lambda i: (i, 0)
    const = lambda i: (0, 0)
    vec = pl.BlockSpec((1, d), const)
    return pl.pallas_call(
        _merge_kernel,
        grid=(t // tm,),
        in_specs=([pl.BlockSpec((tm, GROUP_DIM), row)] * 3 + [pl.BlockSpec((tm, LANES), row)] * 3
                  + [pl.BlockSpec((tm, d), row), pl.BlockSpec((GROUP_DIM, d), const), vec, vec, vec]),
        out_specs=pl.BlockSpec((tm, d), row),
        out_shape=jax.ShapeDtypeStruct((t, d), F32),
        compiler_params=_cparams(("parallel",), 32 << 20),
        name="merge_outproj_ln",
    )(*os_, *lses, x, w_o, b_o, ln_g, ln_b)


def _split_bf16(a):
    hi = a.astype(BF16)
    return hi, (a - hi.astype(F32)).astype(BF16)


def _router_kernel(x_ref, w_ref, b_ref, e_ref, g_ref, r_ref, cnt_ref, carry_ref):
    tm = x_ref.shape[0]

    @pl.when(pl.program_id(0) == 0)
    def _():
        carry_ref[...] = jnp.zeros_like(carry_ref)

    xh, xl = _split_bf16(x_ref[...])
    wh, wl = _split_bf16(w_ref[...])
    dot = functools.partial(jnp.dot, preferred_element_type=F32)
    logits = dot(xh, wh) + (dot(xl, wh) + dot(xh, wl)) + b_ref[...]
    lane = lax.broadcasted_iota(jnp.int32, logits.shape, 1)
    col4 = lax.broadcasted_iota(jnp.int32, (tm, TOP_K), 1)
    work = logits
    picks, vals, idxs = [], [], []
    for _k in range(TOP_K):
        mx = jnp.max(work, axis=-1, keepdims=True)
        idx = jnp.min(jnp.where(work == mx, lane, N_EXPERTS), axis=-1, keepdims=True)
        pick = lane == idx
        work = jnp.where(pick, -jnp.inf, work)
        picks.append(pick)
        vals.append(mx)
        idxs.append(idx)
    ex = [jnp.exp(v - vals[0]) for v in vals]
    inv = 1.0 / (ex[0] + ex[1] + ex[2] + ex[3])
    chosen = jnp.zeros(logits.shape, F32)
    for pick in picks:
        chosen = chosen + pick.astype(F32)
    r_i = lax.broadcasted_iota(jnp.int32, (tm, tm), 0)
    c_i = lax.broadcasted_iota(jnp.int32, (tm, tm), 1)
    tri = (c_i < r_i).astype(BF16)
    before = dot(tri, chosen.astype(BF16)) + carry_ref[...]
    e_out = jnp.zeros((tm, TOP_K), jnp.int32)
    g_out = jnp.zeros((tm, TOP_K), F32)
    r_out = jnp.zeros((tm, TOP_K), jnp.int32)
    for k in range(TOP_K):
        rank = jnp.sum(jnp.where(picks[k], before, 0.0), axis=-1, keepdims=True)
        e_out = jnp.where(col4 == k, idxs[k], e_out)
        g_out = jnp.where(col4 == k, ex[k] * inv, g_out)
        r_out = jnp.where(col4 == k, rank.astype(jnp.int32), r_out)
    e_ref[...] = e_out
    g_ref[...] = g_out
    r_ref[...] = r_out
    carry_ref[...] += jnp.sum(chosen, axis=0, keepdims=True)
    cnt_ref[...] = carry_ref[...]


def _router(x, w, b):
    t, d = x.shape
    tm = ROW_TILE
    row = lambda i: (i, 0)
    const = lambda i: (0, 0)
    k_spec = pl.BlockSpec((tm, TOP_K), row)
    return pl.pallas_call(
        _router_kernel,
        grid=(t // tm,),
        in_specs=[pl.BlockSpec((tm, d), row), pl.BlockSpec((d, N_EXPERTS), const),
                  pl.BlockSpec((1, N_EXPERTS), const)],
        out_specs=[k_spec, k_spec, k_spec, pl.BlockSpec((1, N_EXPERTS), const)],
        out_shape=[jax.ShapeDtypeStruct((t, TOP_K), jnp.int32), jax.ShapeDtypeStruct((t, TOP_K), F32),
                   jax.ShapeDtypeStruct((t, TOP_K), jnp.int32), jax.ShapeDtypeStruct((1, N_EXPERTS), F32)],
        scratch_shapes=[pltpu.VMEM((1, N_EXPERTS), F32)],
        compiler_params=_cparams(("arbitrary",), 32 << 20),
        name="router_topk",
    )(x, w, b)


def _route_plan(e, rank, counts, bm, tm):
    t = e.shape[0]
    nb = TOP_K * t // bm + N_EXPERTS
    counts = counts.reshape(N_EXPERTS).astype(jnp.int32)
    padded = (counts + bm - 1) // bm * bm
    pend = jnp.cumsum(padded)
    pstart = pend - padded
    onehot = e[..., None] == jnp.arange(N_EXPERTS, dtype=jnp.int32)
    dest = jnp.sum(jnp.where(onehot, pstart, 0), axis=-1) + rank
    blk_row0 = jnp.arange(nb, dtype=jnp.int32) * bm
    blk_exp = jnp.minimum(jnp.sum((pend[None, :] <= blk_row0[:, None]).astype(jnp.int32), axis=1),
                          N_EXPERTS - 1)
    n_used = pend[-1:] // bm
    return dest.reshape(t // tm, tm * TOP_K), blk_exp, n_used, pstart + counts, pend


ROW_SUBLANES = D_MODEL // LANES


def _store_row_tiles(ref, base, x):
    for c in range(ROW_SUBLANES):
        ref[pl.ds(base + c, x.shape[0], stride=ROW_SUBLANES), :] = x[:, c * LANES:(c + 1) * LANES]


def _load_row_tiles(ref, base, n):
    return jnp.concatenate([ref[pl.ds(base + c, n, stride=ROW_SUBLANES), :] for c in range(ROW_SUBLANES)], axis=1)


def _tile_rows(ref, row):
    return ref.at[pl.ds(pl.multiple_of(row * ROW_SUBLANES, ROW_SUBLANES), ROW_SUBLANES), :]


def _dispatch_kernel(padlo_ref, pend_ref, nused_ref, dest_hbm, x_ref, xs_hbm, idx0, idx1, pk, zbuf, sems):
    i = pl.program_id(0)
    n = pl.num_programs(0)
    tm = x_ref.shape[0]
    bm = zbuf.shape[0] // ROW_SUBLANES
    dump = xs_hbm.shape[0] // ROW_SUBLANES - bm
    idx = (idx0, idx1)
    slot_rows = tm * ROW_SUBLANES

    def idx_copy(step, s):
        return pltpu.make_async_copy(dest_hbm.at[step], idx[s], sems.at[s])

    def rows_wait(s):
        for _ in range(TOP_K):
            pltpu.make_async_copy(pk.at[pl.ds(s * slot_rows, slot_rows), :], xs_hbm.at[pl.ds(0, slot_rows), :],
                                  sems.at[2 + s]).wait()

    @pl.when(i == 0)
    def _():
        idx_copy(0, 0).start()
        zbuf[...] = jnp.zeros_like(zbuf)

        def zero_block(b, c):
            cp = pltpu.make_async_copy(zbuf, xs_hbm.at[pl.ds(pl.multiple_of(b * zbuf.shape[0], ROW_SUBLANES),
                                                             zbuf.shape[0]), :], sems.at[4])
            cp.start()
            cp.wait()
            return c
        lax.fori_loop(nused_ref[0], dump // bm + 1, zero_block, 0)
        for e in range(N_EXPERTS):
            def zero_row(j, c, e=e):
                r = padlo_ref[e] + j
                r = jnp.where(r < pend_ref[e], r, dump + j)
                pltpu.make_async_copy(_tile_rows(zbuf, j), _tile_rows(xs_hbm, r), sems.at[4]).start()
                return c
            lax.fori_loop(0, bm, zero_row, 0, unroll=DMA_UNROLL)
            pltpu.make_async_copy(zbuf, xs_hbm.at[pl.ds(0, zbuf.shape[0]), :], sems.at[4]).wait()

    def step(s):
        idx_copy(i, s).wait()

        @pl.when(i + 1 < n)
        def _():
            idx_copy(i + 1, 1 - s).start()

        @pl.when(i >= 2)
        def _():
            rows_wait(s)

        _store_row_tiles(pk, s * slot_rows, x_ref[...])

        def issue(j, c):
            for k in range(TOP_K):
                pltpu.make_async_copy(_tile_rows(pk, s * tm + j), _tile_rows(xs_hbm, idx[s][j * TOP_K + k]),
                                      sems.at[2 + s]).start()
            return c
        lax.fori_loop(0, tm, issue, 0, unroll=DMA_UNROLL)

        @pl.when(i == n - 1)
        def _():
            rows_wait(s)

        @pl.when(jnp.logical_and(i == n - 1, n >= 2))
        def _():
            rows_wait(1 - s)

    for s in range(2):
        pl.when(i % 2 == s)(functools.partial(step, s))


def _dispatch(x, dest, pad_lo, pend, n_used, bm):
    t, d = x.shape
    nblk, per = dest.shape
    tm = per // TOP_K
    nb = TOP_K * t // bm + N_EXPERTS
    grid_spec = pltpu.PrefetchScalarGridSpec(
        num_scalar_prefetch=3,
        grid=(nblk,),
        in_specs=[pl.BlockSpec(memory_space=pl.ANY), pl.BlockSpec((tm, d), lambda i, a, b, c: (i, 0))],
        out_specs=pl.BlockSpec(memory_space=pl.ANY),
        scratch_shapes=[pltpu.SMEM((per,), jnp.int32), pltpu.SMEM((per,), jnp.int32),
                        pltpu.VMEM((2 * tm * ROW_SUBLANES, LANES), F32),
                        pltpu.VMEM((bm * ROW_SUBLANES, LANES), F32),
                        pltpu.SemaphoreType.DMA((5,))])
    return pl.pallas_call(
        _dispatch_kernel,
        grid_spec=grid_spec,
        out_shape=jax.ShapeDtypeStruct(((nb * bm + bm) * ROW_SUBLANES, LANES), F32),
        compiler_params=_cparams(("arbitrary",), 32 << 20),
        name="moe_dispatch",
    )(pad_lo, pend, n_used, dest, x)


def _expert_kernel(be_ref, nused_ref, xs_ref, w1_ref, b1g_ref, b1l_ref, w2_ref, b2_ref, y_ref,
                   w1gt, w1lt, w2b, tbuf):
    i = pl.program_id(0)
    d, de = w2b.shape[1], w2b.shape[0]
    nt = (((1,), (1,)), ((), ()))

    @pl.when(i < nused_ref[0])
    def _():
        changed = jnp.logical_or(i == 0, be_ref[i] != be_ref[jnp.maximum(i - 1, 0)])

        @pl.when(changed)
        def _():
            for a in range(d // LANES):
                cols = slice(a * LANES, (a + 1) * LANES)
                tbuf[...] = w1_ref[cols, :].T
                w1gt[:, cols] = tbuf[pl.ds(0, de, stride=2), :].astype(BF16)
                w1lt[:, cols] = tbuf[pl.ds(1, de, stride=2), :].astype(BF16)
            w2b[...] = w2_ref[...].astype(BF16)

        bm = xs_ref.shape[0] // ROW_SUBLANES
        x = _load_row_tiles(xs_ref, 0, bm).astype(BF16)
        hg = lax.dot_general(x, w1gt[...], nt, preferred_element_type=F32) + b1g_ref[...]
        hl = lax.dot_general(x, w1lt[...], nt, preferred_element_type=F32) + b1l_ref[...]
        hg = jnp.minimum(hg, SWIGLU_LIMIT)
        hl = jnp.clip(hl, -SWIGLU_LIMIT, SWIGLU_LIMIT)
        act = hg * (1.0 / (1.0 + jnp.exp(-SWIGLU_ALPHA * hg))) * (hl + 1.0)
        y = jnp.dot(act.astype(BF16), w2b[...], preferred_element_type=F32) + b2_ref[...]
        _store_row_tiles(y_ref, 0, y)

    @pl.when(i >= nused_ref[0])
    def _():
        y_ref[...] = jnp.zeros_like(y_ref)


def _experts(xs, blk_exp, n_used, layer, w1, b1g, b1l, w2, b2, bm):
    d = ROW_SUBLANES * LANES
    nb = blk_exp.shape[0]
    de = w2.shape[2]
    blk = (bm * ROW_SUBLANES, LANES)
    wmap = lambda i, be, nu: (layer, be[i], 0, 0)
    grid_spec = pltpu.PrefetchScalarGridSpec(
        num_scalar_prefetch=2,
        grid=(nb,),
        in_specs=[pl.BlockSpec(blk, lambda i, be, nu: (jnp.minimum(i, nu[0] - 1), 0)),
                  pl.BlockSpec((None, None, d, 2 * de), wmap),
                  pl.BlockSpec((None, None, 1, de), wmap), pl.BlockSpec((None, None, 1, de), wmap),
                  pl.BlockSpec((None, None, de, d), wmap), pl.BlockSpec((None, None, 1, d), wmap)],
        out_specs=pl.BlockSpec(blk, lambda i, be, nu: (i, 0)),
        scratch_shapes=[pltpu.VMEM((de, d), BF16), pltpu.VMEM((de, d), BF16), pltpu.VMEM((de, d), BF16),
                        pltpu.VMEM((2 * de, LANES), F32)])
    return pl.pallas_call(
        _expert_kernel,
        grid_spec=grid_spec,
        out_shape=jax.ShapeDtypeStruct((nb * blk[0], LANES), F32),
        compiler_params=_cparams(("arbitrary",), 54 << 20),
        name="moe_experts",
    )(blk_exp, n_used, xs, w1, b1g, b1l, w2, b2)


def _combine_kernel(dest_hbm, gate_ref, x_ref, g_ref, b_ref, y_hbm, o_ref, idx0, idx1, ybuf, sems, *, blk_off):
    i = pl.program_id(0)
    n = pl.num_programs(0)
    tm = x_ref.shape[0]
    idx = (idx0, idx1)
    slab_rows = tm * ROW_SUBLANES

    def slab(s, k):
        return (s * TOP_K + k) * tm

    def idx_copy(step, s):
        return pltpu.make_async_copy(dest_hbm.at[blk_off + step], idx[s], sems.at[s])

    def issue_rows(s):
        def body(j, c):
            for k in range(TOP_K):
                pltpu.make_async_copy(_tile_rows(y_hbm, idx[s][j * TOP_K + k]), _tile_rows(ybuf, slab(s, k) + j),
                                      sems.at[2 + s]).start()
            return c
        lax.fori_loop(0, tm, body, 0, unroll=DMA_UNROLL)

    @pl.when(i == 0)
    def _():
        cp = idx_copy(0, 0)
        cp.start()
        cp.wait()
        issue_rows(0)

    @pl.when(jnp.logical_and(i == 0, n >= 2))
    def _():
        idx_copy(1, 1).start()

    def step(s):
        @pl.when(i + 1 < n)
        def _():
            idx_copy(i + 1, 1 - s).wait()
            issue_rows(1 - s)

        @pl.when(i + 2 < n)
        def _():
            idx_copy(i + 2, s).start()

        for k in range(TOP_K):
            pltpu.make_async_copy(y_hbm.at[pl.ds(0, slab_rows), :],
                                  ybuf.at[pl.ds(slab(s, k) * ROW_SUBLANES, slab_rows), :], sems.at[2 + s]).wait()
        gate = gate_ref[...]
        h = None
        for k in range(TOP_K):
            term = gate[:, k:k + 1] * _load_row_tiles(ybuf, slab(s, k) * ROW_SUBLANES, tm)
            h = term if h is None else h + term
        o_ref[...] = _layer_norm(DEEPNORM_ALPHA * x_ref[...] + h, g_ref[...], b_ref[...])

    for s in range(2):
        pl.when(i % 2 == s)(functools.partial(step, s))


def _combine(y, dest, gate, x, ln_g, ln_b, row_lo, rows):
    _, d = x.shape
    per = dest.shape[1]
    tm = per // TOP_K
    off = row_lo // tm
    vec = pl.BlockSpec((1, d), lambda i: (0, 0))
    return pl.pallas_call(
        functools.partial(_combine_kernel, blk_off=off),
        grid=(rows // tm,),
        in_specs=[pl.BlockSpec(memory_space=pl.ANY),
                  pl.BlockSpec((tm, TOP_K), lambda i: (off + i, 0)),
                  pl.BlockSpec((tm, d), lambda i: (off + i, 0)), vec, vec,
                  pl.BlockSpec(memory_space=pl.ANY)],
        out_specs=pl.BlockSpec((tm, d), lambda i: (i, 0)),
        out_shape=jax.ShapeDtypeStruct((rows, d), F32),
        scratch_shapes=[pltpu.SMEM((per,), jnp.int32), pltpu.SMEM((per,), jnp.int32),
                        pltpu.VMEM((2 * TOP_K * tm * ROW_SUBLANES, LANES), F32),
                        pltpu.SemaphoreType.DMA((4,))],
        compiler_params=_cparams(("arbitrary",), 40 << 20),
        name="moe_combine_ln",
    )(dest, gate, x, ln_g, ln_b, y)


def _moe_block(x, layer, router_w, router_b, w1, b1g, b1l, w2, b2):
    e, gate, rank, counts = _router(x, router_w, router_b)
    dest, blk_exp, n_used, pad_lo, pend = _route_plan(e, rank, counts, EXPERT_ROWS, ROW_TILE)
    xs = _dispatch(x, dest, pad_lo, pend, n_used, EXPERT_ROWS)
    y = _experts(xs, blk_exp, n_used, layer, w1, b1g, b1l, w2, b2, EXPERT_ROWS)
    return y, dest, gate


def _gelu(z):
    return 0.5 * z * (1.0 + jnp.tanh(0.7978845608028654 * (z + 0.044715 * (z * z * z))))


def _sgu_kernel(x_ref, win_ref, bin_ref, ng_ref, nb_ref, ws_ref, bs_ref, wout_ref, bout_ref,
                g_ref, b_ref, y_ref):
    tm = x_ref.shape[0]
    x = x_ref[...]
    xb = x.astype(BF16)
    dot = functools.partial(jnp.dot, preferred_element_type=F32)
    v = _gelu(dot(xb, win_ref[:, SGU_HALF:]) + bin_ref[:, SGU_HALF:])
    v = _layer_norm(v, ng_ref[...], nb_ref[...]).astype(BF16)
    acc = jnp.zeros((tm, D_MODEL), F32)
    for g in range(SGU_GROUPS):
        cols = slice(g * SGU_GROUP_DIM, (g + 1) * SGU_GROUP_DIM)
        u = _gelu(dot(xb, win_ref[:, cols]) + bin_ref[:, cols])
        mixed = [dot(ws_ref[g], v[c * SGU_CHUNK:(c + 1) * SGU_CHUNK, cols]) + bs_ref[:, g:g + 1]
                 for c in range(tm // SGU_CHUNK)]
        mixed = jnp.concatenate(mixed, axis=0)
        acc = acc + dot((u * mixed).astype(BF16), wout_ref[cols, :])
    h = acc + bout_ref[...]
    y_ref[...] = _layer_norm(DEEPNORM_ALPHA * x + h, g_ref[...], b_ref[...])


def _sgu_block(x, w_in, b_in, norm_g, norm_b, w_s, b_s_t, w_out, b_out, ln_g, ln_b):
    t, d = x.shape
    tm = ROW_TILE
    row = lambda i: (i, 0)
    c2 = lambda i: (0, 0)
    c3 = lambda i: (0, 0, 0)
    once = dict(pipeline_mode=pl.Buffered(1))
    vec = pl.BlockSpec((1, d), c2)
    return pl.pallas_call(
        _sgu_kernel,
        grid=(t // tm,),
        in_specs=[pl.BlockSpec((tm, d), row),
                  pl.BlockSpec((d, 2 * SGU_HALF), c2, **once), pl.BlockSpec((1, 2 * SGU_HALF), c2),
                  pl.BlockSpec((1, SGU_HALF), c2), pl.BlockSpec((1, SGU_HALF), c2),
                  pl.BlockSpec((SGU_GROUPS, SGU_CHUNK, SGU_CHUNK), c3),
                  pl.BlockSpec((SGU_CHUNK, SGU_GROUPS), c2),
                  pl.BlockSpec((SGU_HALF, d), c2, **once), vec, vec, vec],
        out_specs=pl.BlockSpec((tm, d), row),
        out_shape=jax.ShapeDtypeStruct((t, d), F32),
        compiler_params=_cparams(("parallel",), 52 << 20),
        name="sgu_block",
    )(x, w_in, b_in, norm_g, norm_b, w_s, b_s_t, w_out, b_out, ln_g, ln_b)


def _attention_block(x, w_qkv, w_o, b_o, ln_g, ln_b, seq_rows):
    n_groups = N_GROUPS * HEADS
    slopes = (2.0 ** (-8.0 * jnp.arange(1, n_groups + 1, dtype=F32) / n_groups)).reshape(N_GROUPS, HEADS)
    scale = jnp.concatenate([jnp.full((GROUP_DIM,), HEAD_DIM ** -0.5, F32), jnp.ones((2 * GROUP_DIM,), F32)])
    os_, lses = [], []
    for g, (_, dil) in enumerate(DIL_CONFIGS):
        cols = [w_qkv[:, (j * N_GROUPS + g) * GROUP_DIM:(j * N_GROUPS + g + 1) * GROUP_DIM] for j in range(3)]
        w_g = (jnp.concatenate(cols, axis=1) * scale).astype(BF16)
        qkv = _qkv_project(x, w_g, dil)
        segs = tuple((n // dil, length // dil) for n, length in seq_rows)
        o, lse = _attention(qkv, _band_bias(slopes[g], dil), dil, segs)
        t = x.shape[0]
        os_.append(o.reshape(t, GROUP_DIM))
        lses.append(lse.reshape(t, LANES))
    return _merge_project(os_, lses, x, w_o.astype(BF16), b_o[None], ln_g[None], ln_b[None])


def kernel(x_prompt, x_sample, attn_w_qkv, attn_w_o, attn_b_o, sgu_w_in, sgu_b_in, sgu_norm_g, sgu_norm_b,
           sgu_w_s, sgu_b_s, sgu_w_out, sgu_b_out, router_w, router_b, exp_w1, exp_b1, exp_w2, exp_b2,
           ln_mix_g, ln_mix_b, ln_ffn_g, ln_ffn_b):
    d = x_prompt.shape[-1]
    tp = x_prompt.shape[0] * x_prompt.shape[1]
    ts = x_sample.shape[0] * x_sample.shape[1]
    seq_rows = ((tp, x_prompt.shape[1]), (ts, x_sample.shape[1]))
    x = jnp.concatenate([x_prompt.reshape(tp, d), x_sample.reshape(ts, d)], axis=0)

    b1g = exp_b1[:, :, None, 0::2]
    b1l = exp_b1[:, :, None, 1::2]
    b2 = exp_b2[:, :, None, :]

    outs = None
    for i in range(DEPTH):
        j = i // 2
        if i % 2 == 0:
            x = _attention_block(x, attn_w_qkv[j], attn_w_o[j], attn_b_o[j], ln_mix_g[i], ln_mix_b[i], seq_rows)
        else:
            x = _sgu_block(x, sgu_w_in[j].astype(BF16), sgu_b_in[j][None], sgu_norm_g[j][None], sgu_norm_b[j][None],
                           sgu_w_s[j].astype(BF16), jnp.transpose(sgu_b_s[j]), sgu_w_out[j].astype(BF16),
                           sgu_b_out[j][None], ln_mix_g[i][None], ln_mix_b[i][None])
        y, dest, gate = _moe_block(x, i, router_w[i], router_b[i][None], exp_w1, b1g, b1l, exp_w2, b2)
        g_ln, b_ln = ln_ffn_g[i][None], ln_ffn_b[i][None]
        if i + 1 < DEPTH:
            x = _combine(y, dest, gate, x, g_ln, b_ln, 0, tp + ts)
        else:
            outs = (_combine(y, dest, gate, x, g_ln, b_ln, 0, tp).reshape(x_prompt.shape),
                    _combine(y, dest, gate, x, g_ln, b_ln, tp, ts).reshape(x_sample.shape))
    return outs
```

```python
import functools

import jax
import jax.numpy as jnp
from jax import lax
from jax.experimental import pallas as pl
from jax.experimental.pallas import tpu as pltpu

F32 = jnp.float32
BF16 = jnp.bfloat16
U32 = jnp.uint32

LANES = 128
SUBLANES = 8
VMEM_LIMIT_CAP = 56 * 1024 * 1024

D_MODEL = 1024
DIL_CONFIGS = ((128, 1), (512, 4), (2048, 16))
N_GROUPS = len(DIL_CONFIGS)
HEADS = 8
HEAD_DIM = 64
GROUP_DIM = HEADS * HEAD_DIM
N_SIDE = 64
Q_BLOCK = 128
KV_BLOCK = 64
KV_PIECES = (Q_BLOCK + 2 * N_SIDE) // KV_BLOCK
KV_WIN = KV_PIECES * KV_BLOCK
SGU_CHUNK = 128
SGU_HALF = 3 * D_MODEL
SGU_GROUPS = 8
SGU_GROUP_DIM = SGU_HALF // SGU_GROUPS
N_EXPERTS = 32
TOP_K = 4
D_EXPERT = D_MODEL
SWIGLU_ALPHA = 1.702
SWIGLU_LIMIT = 7.0
LN_EPS = 1e-5
DEPTH = 2
DEEPNORM_ALPHA = (2 * DEPTH) ** 0.25
NEG = -1e30

ROW_TILE = 256
SGU_ROWS = 512
EXPERT_ROWS = 512
DMA_UNROLL = 8
DMA_QUEUES = 2


def _cparams(semantics, vmem_bytes):
    return pltpu.CompilerParams(dimension_semantics=semantics,
                                vmem_limit_bytes=min(int(vmem_bytes), VMEM_LIMIT_CAP))


def _layer_norm(y, g, b):
    mu = jnp.mean(y, axis=-1, keepdims=True)
    yc = y - mu
    var = jnp.mean(yc * yc, axis=-1, keepdims=True)
    return yc * lax.rsqrt(var + LN_EPS) * g + b


def _qkv_kernel(x_ref, w_ref, o_ref):
    o_ref[...] = jnp.dot(x_ref[...].astype(BF16), w_ref[...],
                         preferred_element_type=F32).astype(BF16)


def _qkv_project(x, w, dil):
    t, d = x.shape
    n = t // dil
    tm = min(512, n)
    width = w.shape[1]
    xv = x.reshape(n, dil * d)
    return pl.pallas_call(
        _qkv_kernel,
        grid=(dil, n // tm),
        in_specs=[pl.BlockSpec((tm, d), lambda r, i: (i, r)),
                  pl.BlockSpec((d, width), lambda r, i: (0, 0))],
        out_specs=pl.BlockSpec((None, tm, width), lambda r, i: (r, i, 0)),
        out_shape=jax.ShapeDtypeStruct((dil, n, width), BF16),
        compiler_params=_cparams(("parallel", "parallel"),
                                 2 * (tm * d * 4 + d * width * 2 + tm * width * 2) + tm * width * 4
                                 + (8 << 20)),
        name=f"qkv_project_d{dil}",
    )(xv, w)


def _attn_kernel(q_ref, *refs, segs):
    k_refs = refs[:KV_PIECES]
    v_refs = refs[KV_PIECES:2 * KV_PIECES]
    bias_ref, o_ref, lse_ref = refs[2 * KV_PIECES:]
    (n_a, len_a), (_, len_b) = segs
    row0 = pl.program_id(1) * Q_BLOCK
    in_a = row0 < n_a
    lo = jnp.where(in_a, (row0 // len_a) * len_a, n_a + ((row0 - n_a) // len_b) * len_b)
    hi = lo + jnp.where(in_a, len_a, len_b)
    krow = lax.broadcasted_iota(jnp.int32, (1, KV_WIN), 1) + (row0 - N_SIDE)
    col_bias = jnp.where((krow >= lo) & (krow < hi), 0.0, NEG).astype(F32)
    lane = lax.broadcasted_iota(jnp.int32, (1, LANES), 1)
    first_head = lane < HEAD_DIM
    lse_tile = jnp.zeros((Q_BLOCK, LANES), F32)
    nt = (((1,), (1,)), ((), ()))
    for p in range(HEADS // 2):
        sl = slice(p * LANES, (p + 1) * LANES)
        q = q_ref[:, sl]
        k = jnp.concatenate([r[:, sl] for r in k_refs], axis=0)
        v = jnp.concatenate([r[:, sl] for r in v_refs], axis=0)
        zero = jnp.zeros_like(q)
        outs, inv_ls = [], []
        for j, qh in enumerate((jnp.where(first_head, q, zero), jnp.where(first_head, zero, q))):
            h = 2 * p + j
            s = lax.dot_general(qh, k, nt, preferred_element_type=F32)
            s = s + bias_ref[h] + col_bias
            m = jnp.max(s, axis=-1, keepdims=True)
            e = jnp.exp(s - m)
            l = jnp.sum(e, axis=-1, keepdims=True)
            outs.append(jnp.dot(e.astype(BF16), v, preferred_element_type=F32))
            inv_ls.append(1.0 / l)
            lse_tile = jnp.where(lane == h, m + jnp.log(l), lse_tile)
        o = jnp.where(first_head, outs[0] * inv_ls[0], outs[1] * inv_ls[1])
        o_ref[:, sl] = o.astype(BF16)
    lse_ref[...] = lse_tile


def _band_bias(slopes, dil):
    qi = jnp.arange(Q_BLOCK)[:, None]
    kc = jnp.arange(KV_WIN)[None, :]
    dist = jnp.abs(kc - N_SIDE - qi)
    pen = -slopes[:, None, None] * (dist * dil).astype(F32)[None]
    return jnp.where((dist <= N_SIDE)[None], pen, NEG).astype(F32)


def _attention(qkv, bias, dil, segs):
    _, n, _ = qkv.shape
    nkb = n // KV_BLOCK

    def kv_spec(c, col):
        def imap(r, i):
            return (r, jnp.clip(2 * i - 1 + c, 0, nkb - 1), col)
        return pl.BlockSpec((None, KV_BLOCK, GROUP_DIM), imap)

    in_specs = ([pl.BlockSpec((None, Q_BLOCK, GROUP_DIM), lambda r, i: (r, i, 0))]
                + [kv_spec(c, 1) for c in range(KV_PIECES)]
                + [kv_spec(c, 2) for c in range(KV_PIECES)]
                + [pl.BlockSpec((HEADS, Q_BLOCK, KV_WIN), lambda r, i: (0, 0, 0))])
    return pl.pallas_call(
        functools.partial(_attn_kernel, segs=segs),
        grid=(dil, n // Q_BLOCK),
        in_specs=in_specs,
        out_specs=[pl.BlockSpec((Q_BLOCK, GROUP_DIM), lambda r, i: (i, r)),
                   pl.BlockSpec((Q_BLOCK, LANES), lambda r, i: (i, r))],
        out_shape=[jax.ShapeDtypeStruct((n, dil * GROUP_DIM), BF16),
                   jax.ShapeDtypeStruct((n, dil * LANES), F32)],
        compiler_params=_cparams(("parallel", "parallel"), 32 << 20),
        name=f"band_attention_d{dil}",
    )(qkv, *([qkv] * (2 * KV_PIECES)), bias)


def _merge_kernel(o0, o1, o2, l0, l1, l2, x_ref, wo_ref, bo_ref, g_ref, b_ref, y_ref):
    lses = [l0[...], l1[...], l2[...]]
    m = jnp.maximum(jnp.maximum(lses[0], lses[1]), lses[2])
    es = [jnp.exp(l - m) for l in lses]
    inv = 1.0 / (es[0] + es[1] + es[2])
    ws = [e * inv for e in es]
    lane = lax.broadcasted_iota(jnp.int32, (1, LANES), 1)
    first_head = lane < HEAD_DIM
    parts = []
    for p in range(HEADS // 2):
        sl = slice(p * LANES, (p + 1) * LANES)
        acc = None
        for w, o in zip(ws, (o0, o1, o2)):
            wp = jnp.where(first_head, w[:, 2 * p:2 * p + 1], w[:, 2 * p + 1:2 * p + 2])
            term = wp * o[:, sl].astype(F32)
            acc = term if acc is None else acc + term
        parts.append(acc)
    merged = jnp.concatenate(parts, axis=-1).astype(BF16)
    h = jnp.dot(merged, wo_ref[...], preferred_element_type=F32) + bo_ref[...]
    y_ref[...] = _layer_norm(DEEPNORM_ALPHA * x_ref[...] + h, g_ref[...], b_ref[...])


def _merge_project(os_, lses, x, w_o, b_o, ln_g, ln_b):
    t, d = x.shape
    tm = ROW_TILE
    row = lambda i: (i, 0)
    const = lambda i: (0, 0)
    vec = pl.BlockSpec((1, d), const)
    return pl.pallas_call(
        _merge_kernel,
        grid=(t // tm,),
        in_specs=([pl.BlockSpec((tm, GROUP_DIM), row)] * 3 + [pl.BlockSpec((tm, LANES), row)] * 3
                  + [pl.BlockSpec((tm, d), row), pl.BlockSpec((GROUP_DIM, d), const), vec, vec, vec]),
        out_specs=pl.BlockSpec((tm, d), row),
        out_shape=jax.ShapeDtypeStruct((t, d), F32),
        compiler_params=_cparams(("parallel",), 32 << 20),
        name="merge_outproj_ln",
    )(*os_, *lses, x, w_o, b_o, ln_g, ln_b)


def _split_bf16(a):
    hi = a.astype(BF16)
    return hi, (a - hi.astype(F32)).astype(BF16)


def _router_kernel(x_ref, w_ref, b_ref, e_ref, g_ref, r_ref, cnt_ref, carry_ref):
    tm = x_ref.shape[0]

    @pl.when(pl.program_id(0) == 0)
    def _():
        carry_ref[...] = jnp.zeros_like(carry_ref)

    xh, xl = _split_bf16(x_ref[...])
    wh, wl = _split_bf16(w_ref[...])
    dot = functools.partial(jnp.dot, preferred_element_type=F32)
    logits = dot(xh, wh) + (dot(xl, wh) + dot(xh, wl)) + b_ref[...]
    lane = lax.broadcasted_iota(jnp.int32, logits.shape, 1).astype(F32)
    col4 = lax.broadcasted_iota(jnp.int32, (tm, TOP_K), 1)
    work = logits
    picks, vals, idxs = [], [], []
    for _k in range(TOP_K):
        mx = jnp.max(work, axis=-1, keepdims=True)
        idx = jnp.min(jnp.where(work == mx, lane, float(N_EXPERTS)), axis=-1, keepdims=True)
        pick = lane == idx
        work = jnp.where(pick, -jnp.inf, work)
        picks.append(pick)
        vals.append(mx)
        idxs.append(idx)
    ex = [jnp.exp(v - vals[0]) for v in vals]
    inv = 1.0 / (ex[0] + ex[1] + ex[2] + ex[3])
    chosen = jnp.zeros(logits.shape, F32)
    for pick in picks:
        chosen = chosen + pick.astype(F32)
    r_i = lax.broadcasted_iota(jnp.int32, (tm, tm), 0)
    c_i = lax.broadcasted_iota(jnp.int32, (tm, tm), 1)
    tri = (c_i < r_i).astype(BF16)
    before = dot(tri, chosen.astype(BF16)) + carry_ref[...]
    e_out = jnp.zeros((tm, TOP_K), jnp.int32)
    g_out = jnp.zeros((tm, TOP_K), F32)
    r_out = jnp.zeros((tm, TOP_K), jnp.int32)
    for k in range(TOP_K):
        rank = jnp.sum(jnp.where(picks[k], before, 0.0), axis=-1, keepdims=True)
        e_out = jnp.where(col4 == k, idxs[k].astype(jnp.int32), e_out)
        g_out = jnp.where(col4 == k, ex[k] * inv, g_out)
        r_out = jnp.where(col4 == k, rank.astype(jnp.int32), r_out)
    e_ref[...] = e_out
    g_ref[...] = g_out
    r_ref[...] = r_out
    carry_ref[...] += jnp.sum(chosen, axis=0, keepdims=True)
    cnt_ref[...] = carry_ref[...]


def _router(x, w, b):
    t, d = x.shape
    tm = ROW_TILE
    row = lambda i: (i, 0)
    const = lambda i: (0, 0)
    k_spec = pl.BlockSpec((tm, TOP_K), row)
    return pl.pallas_call(
        _router_kernel,
        grid=(t // tm,),
        in_specs=[pl.BlockSpec((tm, d), row), pl.BlockSpec((d, N_EXPERTS), const),
                  pl.BlockSpec((1, N_EXPERTS), const)],
        out_specs=[k_spec, k_spec, k_spec, pl.BlockSpec((1, N_EXPERTS), const)],
        out_shape=[jax.ShapeDtypeStruct((t, TOP_K), jnp.int32), jax.ShapeDtypeStruct((t, TOP_K), F32),
                   jax.ShapeDtypeStruct((t, TOP_K), jnp.int32), jax.ShapeDtypeStruct((1, N_EXPERTS), F32)],
        scratch_shapes=[pltpu.VMEM((1, N_EXPERTS), F32)],
        compiler_params=_cparams(("arbitrary",), 32 << 20),
        name="router_topk",
    )(x, w, b)


def _route_plan(e, rank, counts, bm, tm):
    t = e.shape[0]
    nb = TOP_K * t // bm + N_EXPERTS
    counts = counts.reshape(N_EXPERTS).astype(jnp.int32)
    padded = (counts + bm - 1) // bm * bm
    pend = jnp.cumsum(padded)
    pstart = pend - padded
    onehot = e[..., None] == jnp.arange(N_EXPERTS, dtype=jnp.int32)
    dest = jnp.sum(jnp.where(onehot, pstart, 0), axis=-1) + rank
    blk_row0 = jnp.arange(nb, dtype=jnp.int32) * bm
    blk_exp = jnp.minimum(jnp.sum((pend[None, :] <= blk_row0[:, None]).astype(jnp.int32), axis=1),
                          N_EXPERTS - 1)
    n_used = pend[-1:] // bm
    return dest.reshape(t // tm, tm * TOP_K), blk_exp, n_used, pstart + counts, pend


ROW_SUBLANES = D_MODEL // LANES


def _store_row_tiles(ref, base, x):
    for c in range(ROW_SUBLANES):
        ref[pl.ds(base + c, x.shape[0], stride=ROW_SUBLANES), :] = x[:, c * LANES:(c + 1) * LANES]


def _load_row_tiles(ref, base, n):
    return jnp.concatenate([ref[pl.ds(base + c, n, stride=ROW_SUBLANES), :] for c in range(ROW_SUBLANES)], axis=1)


def _tile_rows(ref, row):
    return ref.at[pl.ds(pl.multiple_of(row * ROW_SUBLANES, ROW_SUBLANES), ROW_SUBLANES), :]


def _dispatch_kernel(padlo_ref, pend_ref, nused_ref, dest_hbm, x_ref, xs_hbm, idx0, idx1, pk, zbuf, sems):
    i = pl.program_id(0)
    n = pl.num_programs(0)
    tm = x_ref.shape[0]
    bm = zbuf.shape[0] // ROW_SUBLANES
    dump = xs_hbm.shape[0] // ROW_SUBLANES - bm
    idx = (idx0, idx1)
    slot_rows = tm * ROW_SUBLANES

    def idx_copy(step, s):
        return pltpu.make_async_copy(dest_hbm.at[step], idx[s], sems.at[s])

    def rows_wait(s):
        for _ in range(TOP_K):
            pltpu.make_async_copy(pk.at[pl.ds(s * slot_rows, slot_rows), :], xs_hbm.at[pl.ds(0, slot_rows), :],
                                  sems.at[2 + s]).wait()

    @pl.when(i == 0)
    def _():
        idx_copy(0, 0).start()
        zbuf[...] = jnp.zeros_like(zbuf)

        def zero_block(b, c):
            cp = pltpu.make_async_copy(zbuf, xs_hbm.at[pl.ds(pl.multiple_of(b * zbuf.shape[0], ROW_SUBLANES),
                                                             zbuf.shape[0]), :], sems.at[4])
            cp.start()
            cp.wait()
            return c
        lax.fori_loop(nused_ref[0], dump // bm + 1, zero_block, 0)
        for e in range(N_EXPERTS):
            def zero_row(jj, c, e=e):
                for u in range(DMA_QUEUES):
                    j = jj * DMA_QUEUES + u
                    r = padlo_ref[e] + j
                    r = jnp.where(r < pend_ref[e], r, dump + j)
                    pltpu.make_async_copy(_tile_rows(zbuf, j), _tile_rows(xs_hbm, r), sems.at[4]).start(priority=u)
                return c
            lax.fori_loop(0, bm // DMA_QUEUES, zero_row, 0, unroll=DMA_UNROLL // DMA_QUEUES)
            pltpu.make_async_copy(zbuf, xs_hbm.at[pl.ds(0, zbuf.shape[0]), :], sems.at[4]).wait()

    def step(s):
        idx_copy(i, s).wait()

        @pl.when(i + 1 < n)
        def _():
            idx_copy(i + 1, 1 - s).start()

        @pl.when(i >= 2)
        def _():
            rows_wait(s)

        _store_row_tiles(pk, s * slot_rows, x_ref[...])

        def issue(j, c):
            for k in range(TOP_K):
                pltpu.make_async_copy(_tile_rows(pk, s * tm + j), _tile_rows(xs_hbm, idx[s][j * TOP_K + k]),
                                      sems.at[2 + s]).start(priority=k % DMA_QUEUES)
            return c
        lax.fori_loop(0, tm, issue, 0, unroll=DMA_UNROLL)

        @pl.when(i == n - 1)
        def _():
            rows_wait(s)

        @pl.when(jnp.logical_and(i == n - 1, n >= 2))
        def _():
            rows_wait(1 - s)

    for s in range(2):
        pl.when(i % 2 == s)(functools.partial(step, s))


def _dispatch(x, dest, pad_lo, pend, n_used, bm):
    t, d = x.shape
    nblk, per = dest.shape
    tm = per // TOP_K
    nb = TOP_K * t // bm + N_EXPERTS
    grid_spec = pltpu.PrefetchScalarGridSpec(
        num_scalar_prefetch=3,
        grid=(nblk,),
        in_specs=[pl.BlockSpec(memory_space=pl.ANY), pl.BlockSpec((tm, d), lambda i, a, b, c: (i, 0))],
        out_specs=pl.BlockSpec(memory_space=pl.ANY),
        scratch_shapes=[pltpu.SMEM((per,), jnp.int32), pltpu.SMEM((per,), jnp.int32),
                        pltpu.VMEM((2 * tm * ROW_SUBLANES, LANES), F32),
                        pltpu.VMEM((bm * ROW_SUBLANES, LANES), F32),
                        pltpu.SemaphoreType.DMA((5,))])
    return pl.pallas_call(
        _dispatch_kernel,
        grid_spec=grid_spec,
        out_shape=jax.ShapeDtypeStruct(((nb * bm + bm) * ROW_SUBLANES, LANES), F32),
        compiler_params=_cparams(("arbitrary",), 32 << 20),
        name="moe_dispatch",
    )(pad_lo, pend, n_used, dest, x)


def _expert_kernel(be_ref, nused_ref, xs_ref, w1_ref, b1g_ref, b1l_ref, w2_ref, b2_ref, y_ref,
                   w1gt, w1lt, w2b, tbuf):
    i = pl.program_id(0)
    d, de = w2b.shape[1], w2b.shape[0]
    nt = (((1,), (1,)), ((), ()))

    @pl.when(i < nused_ref[0])
    def _():
        changed = jnp.logical_or(i == 0, be_ref[i] != be_ref[jnp.maximum(i - 1, 0)])

        @pl.when(changed)
        def _():
            for a in range(d // LANES):
                cols = slice(a * LANES, (a + 1) * LANES)
                tbuf[...] = w1_ref[cols, :].T
                w1gt[:, cols] = tbuf[pl.ds(0, de, stride=2), :].astype(BF16)
                w1lt[:, cols] = tbuf[pl.ds(1, de, stride=2), :].astype(BF16)
            w2b[...] = w2_ref[...].astype(BF16)

        bm = xs_ref.shape[0] // ROW_SUBLANES
        x = _load_row_tiles(xs_ref, 0, bm).astype(BF16)
        hg = lax.dot_general(x, w1gt[...], nt, preferred_element_type=F32) + b1g_ref[...]
        hl = lax.dot_general(x, w1lt[...], nt, preferred_element_type=F32) + b1l_ref[...]
        hg = jnp.minimum(hg, SWIGLU_LIMIT)
        hl = jnp.clip(hl, -SWIGLU_LIMIT, SWIGLU_LIMIT)
        act = hg * (1.0 / (1.0 + jnp.exp(-SWIGLU_ALPHA * hg))) * (hl + 1.0)
        y = jnp.dot(act.astype(BF16), w2b[...], preferred_element_type=F32) + b2_ref[...]
        _store_row_tiles(y_ref, 0, y)

    @pl.when(i >= nused_ref[0])
    def _():
        y_ref[...] = jnp.zeros_like(y_ref)


def _experts(xs, blk_exp, n_used, layer, w1, b1g, b1l, w2, b2, bm):
    d = ROW_SUBLANES * LANES
    nb = blk_exp.shape[0]
    de = w2.shape[2]
    blk = (bm * ROW_SUBLANES, LANES)
    wmap = lambda i, be, nu: (layer, be[i], 0, 0)
    grid_spec = pltpu.PrefetchScalarGridSpec(
        num_scalar_prefetch=2,
        grid=(nb,),
        in_specs=[pl.BlockSpec(blk, lambda i, be, nu: (jnp.minimum(i, nu[0] - 1), 0)),
                  pl.BlockSpec((None, None, d, 2 * de), wmap),
                  pl.BlockSpec((None, None, 1, de), wmap), pl.BlockSpec((None, None, 1, de), wmap),
                  pl.BlockSpec((None, None, de, d), wmap), pl.BlockSpec((None, None, 1, d), wmap)],
        out_specs=pl.BlockSpec(blk, lambda i, be, nu: (i, 0)),
        scratch_shapes=[pltpu.VMEM((de, d), BF16), pltpu.VMEM((de, d), BF16), pltpu.VMEM((de, d), BF16),
                        pltpu.VMEM((2 * de, LANES), F32)])
    return pl.pallas_call(
        _expert_kernel,
        grid_spec=grid_spec,
        out_shape=jax.ShapeDtypeStruct((nb * blk[0], LANES), F32),
        compiler_params=_cparams(("arbitrary",), 54 << 20),
        name="moe_experts",
    )(blk_exp, n_used, xs, w1, b1g, b1l, w2, b2)


def _combine_kernel(dest_hbm, gate_ref, x_ref, g_ref, b_ref, y_hbm, o_ref, idx0, idx1, ybuf, sems, *, blk_off):
    i = pl.program_id(0)
    n = pl.num_programs(0)
    tm = x_ref.shape[0]
    idx = (idx0, idx1)
    slab_rows = tm * ROW_SUBLANES

    def slab(s, k):
        return (s * TOP_K + k) * tm

    def idx_copy(step, s):
        return pltpu.make_async_copy(dest_hbm.at[blk_off + step], idx[s], sems.at[s])

    def issue_rows(s):
        def body(j, c):
            for k in range(TOP_K):
                pltpu.make_async_copy(_tile_rows(y_hbm, idx[s][j * TOP_K + k]), _tile_rows(ybuf, slab(s, k) + j),
                                      sems.at[2 + s]).start(priority=k % DMA_QUEUES)
            return c
        lax.fori_loop(0, tm, body, 0, unroll=DMA_UNROLL)

    @pl.when(i == 0)
    def _():
        cp = idx_copy(0, 0)
        cp.start()
        cp.wait()
        issue_rows(0)

    @pl.when(jnp.logical_and(i == 0, n >= 2))
    def _():
        idx_copy(1, 1).start()

    def step(s):
        @pl.when(i + 1 < n)
        def _():
            idx_copy(i + 1, 1 - s).wait()
            issue_rows(1 - s)

        @pl.when(i + 2 < n)
        def _():
            idx_copy(i + 2, s).start()

        for k in range(TOP_K):
            pltpu.make_async_copy(y_hbm.at[pl.ds(0, slab_rows), :],
                                  ybuf.at[pl.ds(slab(s, k) * ROW_SUBLANES, slab_rows), :], sems.at[2 + s]).wait()
        gate = gate_ref[...]
        h = None
        for k in range(TOP_K):
            term = gate[:, k:k + 1] * _load_row_tiles(ybuf, slab(s, k) * ROW_SUBLANES, tm)
            h = term if h is None else h + term
        o_ref[...] = _layer_norm(DEEPNORM_ALPHA * x_ref[...] + h, g_ref[...], b_ref[...])

    for s in range(2):
        pl.when(i % 2 == s)(functools.partial(step, s))


def _combine(y, dest, gate, x, ln_g, ln_b, row_lo, rows):
    _, d = x.shape
    per = dest.shape[1]
    tm = per // TOP_K
    off = row_lo // tm
    vec = pl.BlockSpec((1, d), lambda i: (0, 0))
    return pl.pallas_call(
        functools.partial(_combine_kernel, blk_off=off),
        grid=(rows // tm,),
        in_specs=[pl.BlockSpec(memory_space=pl.ANY),
                  pl.BlockSpec((tm, TOP_K), lambda i: (off + i, 0)),
                  pl.BlockSpec((tm, d), lambda i: (off + i, 0)), vec, vec,
                  pl.BlockSpec(memory_space=pl.ANY)],
        out_specs=pl.BlockSpec((tm, d), lambda i: (i, 0)),
        out_shape=jax.ShapeDtypeStruct((rows, d), F32),
        scratch_shapes=[pltpu.SMEM((per,), jnp.int32), pltpu.SMEM((per,), jnp.int32),
                        pltpu.VMEM((2 * TOP_K * tm * ROW_SUBLANES, LANES), F32),
                        pltpu.SemaphoreType.DMA((4,))],
        compiler_params=_cparams(("arbitrary",), 40 << 20),
        name="moe_combine_ln",
    )(dest, gate, x, ln_g, ln_b, y)


def _moe_block(x, layer, router_w, router_b, w1, b1g, b1l, w2, b2):
    e, gate, rank, counts = _router(x, router_w, router_b)
    dest, blk_exp, n_used, pad_lo, pend = _route_plan(e, rank, counts, EXPERT_ROWS, ROW_TILE)
    xs = _dispatch(x, dest, pad_lo, pend, n_used, EXPERT_ROWS)
    y = _experts(xs, blk_exp, n_used, layer, w1, b1g, b1l, w2, b2, EXPERT_ROWS)
    return y, dest, gate


def _gelu(z):
    return 0.5 * z * (1.0 + jnp.tanh(0.7978845608028654 * (z + 0.044715 * (z * z * z))))


def _sgu_kernel(x_ref, win_ref, bin_ref, ng_ref, nb_ref, ws_ref, bs_ref, wout_ref, bout_ref,
                g_ref, b_ref, y_ref):
    tm = x_ref.shape[0]
    x = x_ref[...]
    xb = x.astype(BF16)
    dot = functools.partial(jnp.dot, preferred_element_type=F32)
    v = _gelu(dot(xb, win_ref[:, SGU_HALF:]) + bin_ref[:, SGU_HALF:])
    v = _layer_norm(v, ng_ref[...], nb_ref[...]).astype(BF16)
    acc = jnp.zeros((tm, D_MODEL), F32)
    for g in range(SGU_GROUPS):
        cols = slice(g * SGU_GROUP_DIM, (g + 1) * SGU_GROUP_DIM)
        u = _gelu(dot(xb, win_ref[:, cols]) + bin_ref[:, cols])
        mixed = [dot(ws_ref[g], v[c * SGU_CHUNK:(c + 1) * SGU_CHUNK, cols]) + bs_ref[:, g:g + 1]
                 for c in range(tm // SGU_CHUNK)]
        mixed = jnp.concatenate(mixed, axis=0)
        acc = acc + dot((u * mixed).astype(BF16), wout_ref[cols, :])
    h = acc + bout_ref[...]
    y_ref[...] = _layer_norm(DEEPNORM_ALPHA * x + h, g_ref[...], b_ref[...])


def _sgu_block(x, w_in, b_in, norm_g, norm_b, w_s, b_s_t, w_out, b_out, ln_g, ln_b):
    t, d = x.shape
    tm = SGU_ROWS
    row = lambda i: (i, 0)
    c2 = lambda i: (0, 0)
    c3 = lambda i: (0, 0, 0)
    once = dict(pipeline_mode=pl.Buffered(1))
    vec = pl.BlockSpec((1, d), c2)
    return pl.pallas_call(
        _sgu_kernel,
        grid=(t // tm,),
        in_specs=[pl.BlockSpec((tm, d), row),
                  pl.BlockSpec((d, 2 * SGU_HALF), c2, **once), pl.BlockSpec((1, 2 * SGU_HALF), c2),
                  pl.BlockSpec((1, SGU_HALF), c2), pl.BlockSpec((1, SGU_HALF), c2),
                  pl.BlockSpec((SGU_GROUPS, SGU_CHUNK, SGU_CHUNK), c3),
                  pl.BlockSpec((SGU_CHUNK, SGU_GROUPS), c2),
                  pl.BlockSpec((SGU_HALF, d), c2, **once), vec, vec, vec],
        out_specs=pl.BlockSpec((tm, d), row),
        out_shape=jax.ShapeDtypeStruct((t, d), F32),
        compiler_params=_cparams(("parallel",), 52 << 20),
        name="sgu_block",
    )(x, w_in, b_in, norm_g, norm_b, w_s, b_s_t, w_out, b_out, ln_g, ln_b)


def _attention_block(x, w_qkv, w_o, b_o, ln_g, ln_b, seq_rows):
    n_groups = N_GROUPS * HEADS
    slopes = (2.0 ** (-8.0 * jnp.arange(1, n_groups + 1, dtype=F32) / n_groups)).reshape(N_GROUPS, HEADS)
    scale = jnp.concatenate([jnp.full((GROUP_DIM,), HEAD_DIM ** -0.5, F32), jnp.ones((2 * GROUP_DIM,), F32)])
    os_, lses = [], []
    for g, (_, dil) in enumerate(DIL_CONFIGS):
        cols = [w_qkv[:, (j * N_GROUPS + g) * GROUP_DIM:(j * N_GROUPS + g + 1) * GROUP_DIM] for j in range(3)]
        w_g = (jnp.concatenate(cols, axis=1) * scale).astype(BF16)
        qkv = _qkv_project(x, w_g, dil)
        segs = tuple((n // dil, length // dil) for n, length in seq_rows)
        o, lse = _attention(qkv, _band_bias(slopes[g], dil), dil, segs)
        t = x.shape[0]
        os_.append(o.reshape(t, GROUP_DIM))
        lses.append(lse.reshape(t, LANES))
    return _merge_project(os_, lses, x, w_o.astype(BF16), b_o[None], ln_g[None], ln_b[None])


def kernel(x_prompt, x_sample, attn_w_qkv, attn_w_o, attn_b_o, sgu_w_in, sgu_b_in, sgu_norm_g, sgu_norm_b,
           sgu_w_s, sgu_b_s, sgu_w_out, sgu_b_out, router_w, router_b, exp_w1, exp_b1, exp_w2, exp_b2,
           ln_mix_g, ln_mix_b, ln_ffn_g, ln_ffn_b):
    d = x_prompt.shape[-1]
    tp = x_prompt.shape[0] * x_prompt.shape[1]
    ts = x_sample.shape[0] * x_sample.shape[1]
    seq_rows = ((tp, x_prompt.shape[1]), (ts, x_sample.shape[1]))
    x = jnp.concatenate([x_prompt.reshape(tp, d), x_sample.reshape(ts, d)], axis=0)

    b1g = exp_b1[:, :, None, 0::2]
    b1l = exp_b1[:, :, None, 1::2]
    b2 = exp_b2[:, :, None, :]

    outs = None
    for i in range(DEPTH):
        j = i // 2
        if i % 2 == 0:
            x = _attention_block(x, attn_w_qkv[j], attn_w_o[j], attn_b_o[j], ln_mix_g[i], ln_mix_b[i], seq_rows)
        else:
            x = _sgu_block(x, sgu_w_in[j].astype(BF16), sgu_b_in[j][None], sgu_norm_g[j][None], sgu_norm_b[j][None],
                           sgu_w_s[j].astype(BF16), jnp.transpose(sgu_b_s[j]), sgu_w_out[j].astype(BF16),
                           sgu_b_out[j][None], ln_mix_g[i][None], ln_mix_b[i][None])
        y, dest, gate = _moe_block(x, i, router_w[i], router_b[i][None], exp_w1, b1g, b1l, exp_w2, b2)
        g_ln, b_ln = ln_ffn_g[i][None], ln_ffn_b[i][None]
        if i + 1 < DEPTH:
            x = _combine(y, dest, gate, x, g_ln, b_ln, 0, tp + ts)
        else:
            outs = (_combine(y, dest, gate, x, g_ln, b_ln, 0, tp).reshape(x_prompt.shape),
                    _combine(y, dest, gate, x, g_ln, b_ln, tp, ts).reshape(x_sample.shape))
    return outs
```

```python
import functools

import jax
import jax.numpy as jnp
from jax import lax
from jax.experimental import pallas as pl
from jax.experimental.pallas import tpu as pltpu

F32 = jnp.float32
BF16 = jnp.bfloat16
U32 = jnp.uint32

LANES = 128
SUBLANES = 8
VMEM_LIMIT_CAP = 56 * 1024 * 1024

D_MODEL = 1024
DIL_CONFIGS = ((128, 1), (512, 4), (2048, 16))
N_GROUPS = len(DIL_CONFIGS)
HEADS = 8
HEAD_DIM = 64
GROUP_DIM = HEADS * HEAD_DIM
N_SIDE = 64
Q_BLOCK = 128
KV_BLOCK = N_SIDE
KV_WIN = Q_BLOCK + 2 * N_SIDE
ATTN_Q_STEPS = (4, 2, 1)
SGU_CHUNK = 128
SGU_HALF = 3 * D_MODEL
SGU_GROUPS = 8
SGU_GROUP_DIM = SGU_HALF // SGU_GROUPS
N_EXPERTS = 32
TOP_K = 4
D_EXPERT = D_MODEL
SWIGLU_ALPHA = 1.702
SWIGLU_LIMIT = 7.0
LN_EPS = 1e-5
DEPTH = 2
DEEPNORM_ALPHA = (2 * DEPTH) ** 0.25
NEG = -1e30

ROW_TILE = 512
RANK_CHUNK = 256
SGU_ROWS = 512
EXPERT_ROWS = 512
DMA_UNROLL = 8
DMA_QUEUES = 2


def _cparams(semantics, vmem_bytes):
    return pltpu.CompilerParams(dimension_semantics=semantics,
                                vmem_limit_bytes=min(int(vmem_bytes), VMEM_LIMIT_CAP))


def _layer_norm(y, g, b):
    mu = jnp.mean(y, axis=-1, keepdims=True)
    yc = y - mu
    var = jnp.mean(yc * yc, axis=-1, keepdims=True)
    return yc * lax.rsqrt(var + LN_EPS) * g + b


def _qkv_kernel(x_ref, w_ref, o_ref):
    o_ref[...] = jnp.dot(x_ref[...].astype(BF16), w_ref[...],
                         preferred_element_type=F32).astype(BF16)


def _qkv_project(x, w, dil):
    t, d = x.shape
    n = t // dil
    tm = min(512, n)
    width = w.shape[1]
    xv = x.reshape(n, dil * d)
    return pl.pallas_call(
        _qkv_kernel,
        grid=(dil, n // tm),
        in_specs=[pl.BlockSpec((tm, d), lambda r, i: (i, r)),
                  pl.BlockSpec((d, width), lambda r, i: (0, 0))],
        out_specs=pl.BlockSpec((None, tm, width), lambda r, i: (r, i, 0)),
        out_shape=jax.ShapeDtypeStruct((dil, n, width), BF16),
        compiler_params=_cparams(("parallel", "parallel"),
                                 2 * (tm * d * 4 + d * width * 2 + tm * width * 2) + tm * width * 4
                                 + (8 << 20)),
        name=f"qkv_project_d{dil}",
    )(xv, w)


def _attn_kernel(q_ref, kp_ref, kb_ref, kn_ref, vp_ref, vb_ref, vn_ref, bias_ref, o_ref, lse_ref, *, segs):
    (n_a, len_a), (_, len_b) = segs
    step_rows = q_ref.shape[0]
    lane = lax.broadcasted_iota(jnp.int32, (1, LANES), 1)
    first_head = lane < HEAD_DIM
    nt = (((1,), (1,)), ((), ()))

    def window(prev, body, nxt, first, sl):
        parts = []
        if first < 0:
            parts.append(prev[:, sl])
        parts.append(body[max(first, 0):min(first + KV_WIN, step_rows), sl])
        if first + KV_WIN > step_rows:
            parts.append(nxt[:, sl])
        return jnp.concatenate(parts, axis=0)

    for sub in range(step_rows // Q_BLOCK):
        rows = slice(sub * Q_BLOCK, (sub + 1) * Q_BLOCK)
        row0 = pl.program_id(1) * step_rows + sub * Q_BLOCK
        in_a = row0 < n_a
        lo = jnp.where(in_a, (row0 // len_a) * len_a, n_a + ((row0 - n_a) // len_b) * len_b)
        hi = lo + jnp.where(in_a, len_a, len_b)
        krow = lax.broadcasted_iota(jnp.int32, (1, KV_WIN), 1) + (row0 - N_SIDE)
        col_bias = jnp.where((krow >= lo) & (krow < hi), 0.0, NEG).astype(F32)
        lse_tile = jnp.zeros((Q_BLOCK, LANES), F32)
        for p in range(HEADS // 2):
            sl = slice(p * LANES, (p + 1) * LANES)
            q = q_ref[rows, sl]
            k = window(kp_ref, kb_ref, kn_ref, sub * Q_BLOCK - N_SIDE, sl)
            v = window(vp_ref, vb_ref, vn_ref, sub * Q_BLOCK - N_SIDE, sl)
            zero = jnp.zeros_like(q)
            outs, inv_ls = [], []
            for j, qh in enumerate((jnp.where(first_head, q, zero), jnp.where(first_head, zero, q))):
                h = 2 * p + j
                s = lax.dot_general(qh, k, nt, preferred_element_type=F32)
                s = s + bias_ref[h] + col_bias
                m = jnp.max(s, axis=-1, keepdims=True)
                e = jnp.exp(s - m)
                l = jnp.sum(e, axis=-1, keepdims=True)
                outs.append(jnp.dot(e.astype(BF16), v, preferred_element_type=F32))
                inv_ls.append(1.0 / l)
                lse_tile = jnp.where(lane == h, m + jnp.log(l), lse_tile)
            o = jnp.where(first_head, outs[0] * inv_ls[0], outs[1] * inv_ls[1])
            o_ref[rows, sl] = o.astype(BF16)
        lse_ref[rows, :] = lse_tile


def _band_bias(slopes, dil):
    qi = jnp.arange(Q_BLOCK)[:, None]
    kc = jnp.arange(KV_WIN)[None, :]
    dist = jnp.abs(kc - N_SIDE - qi)
    pen = -slopes[:, None, None] * (dist * dil).astype(F32)[None]
    return jnp.where((dist <= N_SIDE)[None], pen, NEG).astype(F32)


def _attention(qkv, bias, dil, segs):
    _, n, _ = qkv.shape
    nkb = n // KV_BLOCK
    q_step = max(s for s in ATTN_Q_STEPS if (n // Q_BLOCK) % s == 0)
    step_rows = q_step * Q_BLOCK
    per_step = step_rows // KV_BLOCK

    def body_spec(col):
        return pl.BlockSpec((None, step_rows, GROUP_DIM), lambda r, i: (r, i, col))

    def edge_spec(col, after):
        def imap(r, i):
            return (r, jnp.clip((i + after) * per_step - 1 + after, 0, nkb - 1), col)
        return pl.BlockSpec((None, KV_BLOCK, GROUP_DIM), imap)

    in_specs = ([body_spec(0)]
                + [edge_spec(1, 0), body_spec(1), edge_spec(1, 1)]
                + [edge_spec(2, 0), body_spec(2), edge_spec(2, 1)]
                + [pl.BlockSpec((HEADS, Q_BLOCK, KV_WIN), lambda r, i: (0, 0, 0))])
    return pl.pallas_call(
        functools.partial(_attn_kernel, segs=segs),
        grid=(dil, n // step_rows),
        in_specs=in_specs,
        out_specs=[pl.BlockSpec((step_rows, GROUP_DIM), lambda r, i: (i, r)),
                   pl.BlockSpec((step_rows, LANES), lambda r, i: (i, r))],
        out_shape=[jax.ShapeDtypeStruct((n, dil * GROUP_DIM), BF16),
                   jax.ShapeDtypeStruct((n, dil * LANES), F32)],
        compiler_params=_cparams(("parallel", "parallel"), 32 << 20),
        name=f"band_attention_d{dil}",
    )(*([qkv] * 7), bias)


def _merge_kernel(o0, o1, o2, l0, l1, l2, x_ref, wo_ref, bo_ref, g_ref, b_ref, y_ref):
    lses = [l0[...], l1[...], l2[...]]
    m = jnp.maximum(jnp.maximum(lses[0], lses[1]), lses[2])
    es = [jnp.exp(l - m) for l in lses]
    inv = 1.0 / (es[0] + es[1] + es[2])
    ws = [e * inv for e in es]
    lane = lax.broadcasted_iota(jnp.int32, (1, LANES), 1)
    first_head = lane < HEAD_DIM
    parts = []
    for p in range(HEADS // 2):
        sl = slice(p * LANES, (p + 1) * LANES)
        acc = None
        for w, o in zip(ws, (o0, o1, o2)):
            wp = jnp.where(first_head, w[:, 2 * p:2 * p + 1], w[:, 2 * p + 1:2 * p + 2])
            term = wp * o[:, sl].astype(F32)
            acc = term if acc is None else acc + term
        parts.append(acc)
    merged = jnp.concatenate(parts, axis=-1).astype(BF16)
    h = jnp.dot(merged, wo_ref[...], preferred_element_type=F32) + bo_ref[...]
    y_ref[...] = _layer_norm(DEEPNORM_ALPHA * x_ref[...] + h, g_ref[...], b_ref[...])


def _merge_project(os_, lses, x, w_o, b_o, ln_g, ln_b):
    t, d = x.shape
    tm = ROW_TILE
    row = lambda i: (i, 0)
    const = lambda i: (0, 0)
    vec = pl.BlockSpec((1, d), const)
    return pl.pallas_call(
        _merge_kernel,
        grid=(t // tm,),
        in_specs=([pl.BlockSpec((tm, GROUP_DIM), row)] * 3 + [pl.BlockSpec((tm, LANES), row)] * 3
                  + [pl.BlockSpec((tm, d), row), pl.BlockSpec((GROUP_DIM, d), const), vec, vec, vec]),
        out_specs=pl.BlockSpec((tm, d), row),
        out_shape=jax.ShapeDtypeStruct((t, d), F32),
        compiler_params=_cparams(("parallel",), 32 << 20),
        name="merge_outproj_ln",
    )(*os_, *lses, x, w_o, b_o, ln_g, ln_b)


def _split_bf16(a):
    hi = a.astype(BF16)
    return hi, (a - hi.astype(F32)).astype(BF16)


def _router_kernel(x_ref, w_ref, b_ref, e_ref, g_ref, r_ref, cnt_ref, carry_ref):
    tm = x_ref.shape[0]

    @pl.when(pl.program_id(0) == 0)
    def _():
        carry_ref[...] = jnp.zeros_like(carry_ref)

    xh, xl = _split_bf16(x_ref[...])
    wh, wl = _split_bf16(w_ref[...])
    dot = functools.partial(jnp.dot, preferred_element_type=F32)
    logits = dot(xh, wh) + (dot(xl, wh) + dot(xh, wl)) + b_ref[...]
    lane = lax.broadcasted_iota(jnp.int32, logits.shape, 1).astype(F32)
    col4 = lax.broadcasted_iota(jnp.int32, (tm, TOP_K), 1)
    work = logits
    picks, vals, idxs = [], [], []
    for _k in range(TOP_K):
        mx = jnp.max(work, axis=-1, keepdims=True)
        idx = jnp.min(jnp.where(work == mx, lane, float(N_EXPERTS)), axis=-1, keepdims=True)
        pick = lane == idx
        work = jnp.where(pick, -jnp.inf, work)
        picks.append(pick)
        vals.append(mx)
        idxs.append(idx)
    ex = [jnp.exp(v - vals[0]) for v in vals]
    inv = 1.0 / (ex[0] + ex[1] + ex[2] + ex[3])
    chosen = jnp.zeros(logits.shape, F32)
    for pick in picks:
        chosen = chosen + pick.astype(F32)
    ch = min(RANK_CHUNK, tm)
    r_i = lax.broadcasted_iota(jnp.int32, (ch, ch), 0)
    c_i = lax.broadcasted_iota(jnp.int32, (ch, ch), 1)
    tri = (c_i < r_i).astype(BF16)
    carry = carry_ref[...]
    before = []
    for c in range(tm // ch):
        part = chosen[c * ch:(c + 1) * ch]
        before.append(dot(tri, part.astype(BF16)) + carry)
        carry = carry + jnp.sum(part, axis=0, keepdims=True)
    before = jnp.concatenate(before, axis=0)
    e_out = jnp.zeros((tm, TOP_K), jnp.int32)
    g_out = jnp.zeros((tm, TOP_K), F32)
    r_out = jnp.zeros((tm, TOP_K), jnp.int32)
    for k in range(TOP_K):
        rank = jnp.sum(jnp.where(picks[k], before, 0.0), axis=-1, keepdims=True)
        e_out = jnp.where(col4 == k, idxs[k].astype(jnp.int32), e_out)
        g_out = jnp.where(col4 == k, ex[k] * inv, g_out)
        r_out = jnp.where(col4 == k, rank.astype(jnp.int32), r_out)
    e_ref[...] = e_out
    g_ref[...] = g_out
    r_ref[...] = r_out
    carry_ref[...] = carry
    cnt_ref[...] = carry


def _router(x, w, b):
    t, d = x.shape
    tm = ROW_TILE
    row = lambda i: (i, 0)
    const = lambda i: (0, 0)
    k_spec = pl.BlockSpec((tm, TOP_K), row)
    return pl.pallas_call(
        _router_kernel,
        grid=(t // tm,),
        in_specs=[pl.BlockSpec((tm, d), row), pl.BlockSpec((d, N_EXPERTS), const),
                  pl.BlockSpec((1, N_EXPERTS), const)],
        out_specs=[k_spec, k_spec, k_spec, pl.BlockSpec((1, N_EXPERTS), const)],
        out_shape=[jax.ShapeDtypeStruct((t, TOP_K), jnp.int32), jax.ShapeDtypeStruct((t, TOP_K), F32),
                   jax.ShapeDtypeStruct((t, TOP_K), jnp.int32), jax.ShapeDtypeStruct((1, N_EXPERTS), F32)],
        scratch_shapes=[pltpu.VMEM((1, N_EXPERTS), F32)],
        compiler_params=_cparams(("arbitrary",), 32 << 20),
        name="router_topk",
    )(x, w, b)


def _route_plan(e, rank, counts, bm, tm):
    t = e.shape[0]
    nb = TOP_K * t // bm + N_EXPERTS
    counts = counts.reshape(N_EXPERTS).astype(jnp.int32)
    padded = (counts + bm - 1) // bm * bm
    pend = jnp.cumsum(padded)
    pstart = pend - padded
    onehot = e[..., None] == jnp.arange(N_EXPERTS, dtype=jnp.int32)
    dest = jnp.sum(jnp.where(onehot, pstart, 0), axis=-1) + rank
    blk_row0 = jnp.arange(nb, dtype=jnp.int32) * bm
    blk_exp = jnp.minimum(jnp.sum((pend[None, :] <= blk_row0[:, None]).astype(jnp.int32), axis=1),
                          N_EXPERTS - 1)
    n_used = pend[-1:] // bm
    return dest.reshape(t // tm, tm * TOP_K), blk_exp, n_used, pstart + counts, pend


ROW_SUBLANES = D_MODEL // LANES


def _store_row_tiles(ref, base, x):
    for c in range(ROW_SUBLANES):
        ref[pl.ds(base + c, x.shape[0], stride=ROW_SUBLANES), :] = x[:, c * LANES:(c + 1) * LANES]


def _load_row_tiles(ref, base, n):
    return jnp.concatenate([ref[pl.ds(base + c, n, stride=ROW_SUBLANES), :] for c in range(ROW_SUBLANES)], axis=1)


def _tile_rows(ref, row):
    return ref.at[pl.ds(pl.multiple_of(row * ROW_SUBLANES, ROW_SUBLANES), ROW_SUBLANES), :]


def _dispatch_kernel(padlo_ref, pend_ref, nused_ref, dest_hbm, x_ref, xs_hbm, idx0, idx1, pk, zbuf, sems):
    i = pl.program_id(0)
    n = pl.num_programs(0)
    tm = x_ref.shape[0]
    bm = zbuf.shape[0] // ROW_SUBLANES
    dump = xs_hbm.shape[0] // ROW_SUBLANES - bm
    idx = (idx0, idx1)
    slot_rows = tm * ROW_SUBLANES

    def idx_copy(step, s):
        return pltpu.make_async_copy(dest_hbm.at[step], idx[s], sems.at[s])

    def rows_wait(s):
        for _ in range(TOP_K):
            pltpu.make_async_copy(pk.at[pl.ds(s * slot_rows, slot_rows), :], xs_hbm.at[pl.ds(0, slot_rows), :],
                                  sems.at[2 + s]).wait()

    @pl.when(i == 0)
    def _():
        idx_copy(0, 0).start()
        zbuf[...] = jnp.zeros_like(zbuf)

        def zero_block(b, c):
            cp = pltpu.make_async_copy(zbuf, xs_hbm.at[pl.ds(pl.multiple_of(b * zbuf.shape[0], ROW_SUBLANES),
                                                             zbuf.shape[0]), :], sems.at[4])
            cp.start()
            cp.wait()
            return c
        lax.fori_loop(nused_ref[0], dump // bm + 1, zero_block, 0)
        for e in range(N_EXPERTS):
            def zero_row(jj, c, e=e):
                for u in range(DMA_QUEUES):
                    j = jj * DMA_QUEUES + u
                    r = padlo_ref[e] + j
                    r = jnp.where(r < pend_ref[e], r, dump + j)
                    pltpu.make_async_copy(_tile_rows(zbuf, j), _tile_rows(xs_hbm, r), sems.at[4]).start(priority=u)
                return c
            lax.fori_loop(0, bm // DMA_QUEUES, zero_row, 0, unroll=DMA_UNROLL // DMA_QUEUES)
            pltpu.make_async_copy(zbuf, xs_hbm.at[pl.ds(0, zbuf.shape[0]), :], sems.at[4]).wait()

    def step(s):
        idx_copy(i, s).wait()

        @pl.when(i + 1 < n)
        def _():
            idx_copy(i + 1, 1 - s).start()

        @pl.when(i >= 2)
        def _():
            rows_wait(s)

        _store_row_tiles(pk, s * slot_rows, x_ref[...])

        def issue(j, c):
            for k in range(TOP_K):
                pltpu.make_async_copy(_tile_rows(pk, s * tm + j), _tile_rows(xs_hbm, idx[s][j * TOP_K + k]),
                                      sems.at[2 + s]).start(priority=k % DMA_QUEUES)
            return c
        lax.fori_loop(0, tm, issue, 0, unroll=DMA_UNROLL)

        @pl.when(i == n - 1)
        def _():
            rows_wait(s)

        @pl.when(jnp.logical_and(i == n - 1, n >= 2))
        def _():
            rows_wait(1 - s)

    for s in range(2):
        pl.when(i % 2 == s)(functools.partial(step, s))


def _dispatch(x, dest, pad_lo, pend, n_used, bm):
    t, d = x.shape
    nblk, per = dest.shape
    tm = per // TOP_K
    nb = TOP_K * t // bm + N_EXPERTS
    grid_spec = pltpu.PrefetchScalarGridSpec(
        num_scalar_prefetch=3,
        grid=(nblk,),
        in_specs=[pl.BlockSpec(memory_space=pl.ANY), pl.BlockSpec((tm, d), lambda i, a, b, c: (i, 0))],
        out_specs=pl.BlockSpec(memory_space=pl.ANY),
        scratch_shapes=[pltpu.SMEM((per,), jnp.int32), pltpu.SMEM((per,), jnp.int32),
                        pltpu.VMEM((2 * tm * ROW_SUBLANES, LANES), F32),
                        pltpu.VMEM((bm * ROW_SUBLANES, LANES), F32),
                        pltpu.SemaphoreType.DMA((5,))])
    return pl.pallas_call(
        _dispatch_kernel,
        grid_spec=grid_spec,
        out_shape=jax.ShapeDtypeStruct(((nb * bm + bm) * ROW_SUBLANES, LANES), F32),
        compiler_params=_cparams(("arbitrary",), 32 << 20),
        name="moe_dispatch",
    )(pad_lo, pend, n_used, dest, x)


def _expert_kernel(be_ref, nused_ref, xs_ref, w1_ref, b1g_ref, b1l_ref, w2_ref, b2_ref, y_ref,
                   w1gt, w1lt, w2b, tbuf):
    i = pl.program_id(0)
    d, de = w2b.shape[1], w2b.shape[0]
    nt = (((1,), (1,)), ((), ()))

    @pl.when(i < nused_ref[0])
    def _():
        changed = jnp.logical_or(i == 0, be_ref[i] != be_ref[jnp.maximum(i - 1, 0)])

        @pl.when(changed)
        def _():
            for a in range(d // LANES):
                cols = slice(a * LANES, (a + 1) * LANES)
                tbuf[...] = w1_ref[cols, :].T
                w1gt[:, cols] = tbuf[pl.ds(0, de, stride=2), :].astype(BF16)
                w1lt[:, cols] = tbuf[pl.ds(1, de, stride=2), :].astype(BF16)
            w2b[...] = w2_ref[...].astype(BF16)

        bm = xs_ref.shape[0] // ROW_SUBLANES
        x = _load_row_tiles(xs_ref, 0, bm).astype(BF16)
        hg = lax.dot_general(x, w1gt[...], nt, preferred_element_type=F32) + b1g_ref[...]
        hl = lax.dot_general(x, w1lt[...], nt, preferred_element_type=F32) + b1l_ref[...]
        hg = jnp.minimum(hg, SWIGLU_LIMIT)
        hl = jnp.clip(hl, -SWIGLU_LIMIT, SWIGLU_LIMIT)
        act = hg * (1.0 / (1.0 + jnp.exp(-SWIGLU_ALPHA * hg))) * (hl + 1.0)
        y = jnp.dot(act.astype(BF16), w2b[...], preferred_element_type=F32) + b2_ref[...]
        _store_row_tiles(y_ref, 0, y)

    @pl.when(i >= nused_ref[0])
    def _():
        y_ref[...] = jnp.zeros_like(y_ref)


def _experts(xs, blk_exp, n_used, layer, w1, b1g, b1l, w2, b2, bm):
    d = ROW_SUBLANES * LANES
    nb = blk_exp.shape[0]
    de = w2.shape[2]
    blk = (bm * ROW_SUBLANES, LANES)
    wmap = lambda i, be, nu: (layer, be[i], 0, 0)
    grid_spec = pltpu.PrefetchScalarGridSpec(
        num_scalar_prefetch=2,
        grid=(nb,),
        in_specs=[pl.BlockSpec(blk, lambda i, be, nu: (jnp.minimum(i, nu[0] - 1), 0)),
                  pl.BlockSpec((None, None, d, 2 * de), wmap),
                  pl.BlockSpec((None, None, 1, de), wmap), pl.BlockSpec((None, None, 1, de), wmap),
                  pl.BlockSpec((None, None, de, d), wmap), pl.BlockSpec((None, None, 1, d), wmap)],
        out_specs=pl.BlockSpec(blk, lambda i, be, nu: (i, 0)),
        scratch_shapes=[pltpu.VMEM((de, d), BF16), pltpu.VMEM((de, d), BF16), pltpu.VMEM((de, d), BF16),
                        pltpu.VMEM((2 * de, LANES), F32)])
    return pl.pallas_call(
        _expert_kernel,
        grid_spec=grid_spec,
        out_shape=jax.ShapeDtypeStruct((nb * blk[0], LANES), F32),
        compiler_params=_cparams(("arbitrary",), 54 << 20),
        name="moe_experts",
    )(blk_exp, n_used, xs, w1, b1g, b1l, w2, b2)


def _combine_kernel(dest_hbm, gate_ref, x_ref, g_ref, b_ref, y_hbm, o_ref, idx0, idx1, ybuf, sems, *, blk_off):
    i = pl.program_id(0)
    n = pl.num_programs(0)
    tm = x_ref.shape[0]
    idx = (idx0, idx1)
    slab_rows = tm * ROW_SUBLANES

    def slab(s, k):
        return (s * TOP_K + k) * tm

    def idx_copy(step, s):
        return pltpu.make_async_copy(dest_hbm.at[blk_off + step], idx[s], sems.at[s])

    def issue_rows(s):
        def body(j, c):
            for k in range(TOP_K):
                pltpu.make_async_copy(_tile_rows(y_hbm, idx[s][j * TOP_K + k]), _tile_rows(ybuf, slab(s, k) + j),
                                      sems.at[2 + s]).start(priority=k % DMA_QUEUES)
            return c
        lax.fori_loop(0, tm, body, 0, unroll=DMA_UNROLL)

    @pl.when(i == 0)
    def _():
        cp = idx_copy(0, 0)
        cp.start()
        cp.wait()
        issue_rows(0)

    @pl.when(jnp.logical_and(i == 0, n >= 2))
    def _():
        idx_copy(1, 1).start()

    def step(s):
        @pl.when(i + 1 < n)
        def _():
            idx_copy(i + 1, 1 - s).wait()
            issue_rows(1 - s)

        @pl.when(i + 2 < n)
        def _():
            idx_copy(i + 2, s).start()

        for k in range(TOP_K):
            pltpu.make_async_copy(y_hbm.at[pl.ds(0, slab_rows), :],
                                  ybuf.at[pl.ds(slab(s, k) * ROW_SUBLANES, slab_rows), :], sems.at[2 + s]).wait()
        gate = gate_ref[...]
        h = None
        for k in range(TOP_K):
            term = gate[:, k:k + 1] * _load_row_tiles(ybuf, slab(s, k) * ROW_SUBLANES, tm)
            h = term if h is None else h + term
        o_ref[...] = _layer_norm(DEEPNORM_ALPHA * x_ref[...] + h, g_ref[...], b_ref[...])

    for s in range(2):
        pl.when(i % 2 == s)(functools.partial(step, s))


def _combine(y, dest, gate, x, ln_g, ln_b, row_lo, rows):
    _, d = x.shape
    per = dest.shape[1]
    tm = per // TOP_K
    off = row_lo // tm
    vec = pl.BlockSpec((1, d), lambda i: (0, 0))
    return pl.pallas_call(
        functools.partial(_combine_kernel, blk_off=off),
        grid=(rows // tm,),
        in_specs=[pl.BlockSpec(memory_space=pl.ANY),
                  pl.BlockSpec((tm, TOP_K), lambda i: (off + i, 0)),
                  pl.BlockSpec((tm, d), lambda i: (off + i, 0)), vec, vec,
                  pl.BlockSpec(memory_space=pl.ANY)],
        out_specs=pl.BlockSpec((tm, d), lambda i: (i, 0)),
        out_shape=jax.ShapeDtypeStruct((rows, d), F32),
        scratch_shapes=[pltpu.SMEM((per,), jnp.int32), pltpu.SMEM((per,), jnp.int32),
                        pltpu.VMEM((2 * TOP_K * tm * ROW_SUBLANES, LANES), F32),
                        pltpu.SemaphoreType.DMA((4,))],
        compiler_params=_cparams(("arbitrary",), 40 << 20),
        name="moe_combine_ln",
    )(dest, gate, x, ln_g, ln_b, y)


def _moe_block(x, layer, router_w, router_b, w1, b1g, b1l, w2, b2):
    e, gate, rank, counts = _router(x, router_w, router_b)
    dest, blk_exp, n_used, pad_lo, pend = _route_plan(e, rank, counts, EXPERT_ROWS, ROW_TILE)
    xs = _dispatch(x, dest, pad_lo, pend, n_used, EXPERT_ROWS)
    y = _experts(xs, blk_exp, n_used, layer, w1, b1g, b1l, w2, b2, EXPERT_ROWS)
    return y, dest, gate


def _gelu(z):
    return 0.5 * z * (1.0 + jnp.tanh(0.7978845608028654 * (z + 0.044715 * (z * z * z))))


def _sgu_kernel(x_ref, win_ref, bin_ref, ng_ref, nb_ref, ws_ref, bs_ref, wout_ref, bout_ref,
                g_ref, b_ref, y_ref):
    tm = x_ref.shape[0]
    x = x_ref[...]
    xb = x.astype(BF16)
    dot = functools.partial(jnp.dot, preferred_element_type=F32)
    v = _gelu(dot(xb, win_ref[:, SGU_HALF:]) + bin_ref[:, SGU_HALF:])
    v = _layer_norm(v, ng_ref[...], nb_ref[...]).astype(BF16)
    acc = jnp.zeros((tm, D_MODEL), F32)
    for g in range(SGU_GROUPS):
        cols = slice(g * SGU_GROUP_DIM, (g + 1) * SGU_GROUP_DIM)
        u = _gelu(dot(xb, win_ref[:, cols]) + bin_ref[:, cols])
        mixed = [dot(ws_ref[g], v[c * SGU_CHUNK:(c + 1) * SGU_CHUNK, cols]) + bs_ref[:, g:g + 1]
                 for c in range(tm // SGU_CHUNK)]
        mixed = jnp.concatenate(mixed, axis=0)
        acc = acc + dot((u * mixed).astype(BF16), wout_ref[cols, :])
    h = acc + bout_ref[...]
    y_ref[...] = _layer_norm(DEEPNORM_ALPHA * x + h, g_ref[...], b_ref[...])


def _sgu_block(x, w_in, b_in, norm_g, norm_b, w_s, b_s_t, w_out, b_out, ln_g, ln_b):
    t, d = x.shape
    tm = SGU_ROWS
    row = lambda i: (i, 0)
    c2 = lambda i: (0, 0)
    c3 = lambda i: (0, 0, 0)
    once = dict(pipeline_mode=pl.Buffered(1))
    vec = pl.BlockSpec((1, d), c2)
    return pl.pallas_call(
        _sgu_kernel,
        grid=(t // tm,),
        in_specs=[pl.BlockSpec((tm, d), row),
                  pl.BlockSpec((d, 2 * SGU_HALF), c2, **once), pl.BlockSpec((1, 2 * SGU_HALF), c2),
                  pl.BlockSpec((1, SGU_HALF), c2), pl.BlockSpec((1, SGU_HALF), c2),
                  pl.BlockSpec((SGU_GROUPS, SGU_CHUNK, SGU_CHUNK), c3),
                  pl.BlockSpec((SGU_CHUNK, SGU_GROUPS), c2),
                  pl.BlockSpec((SGU_HALF, d), c2, **once), vec, vec, vec],
        out_specs=pl.BlockSpec((tm, d), row),
        out_shape=jax.ShapeDtypeStruct((t, d), F32),
        compiler_params=_cparams(("parallel",), 52 << 20),
        name="sgu_block",
    )(x, w_in, b_in, norm_g, norm_b, w_s, b_s_t, w_out, b_out, ln_g, ln_b)


def _attention_block(x, w_qkv, w_o, b_o, ln_g, ln_b, seq_rows):
    n_groups = N_GROUPS * HEADS
    slopes = (2.0 ** (-8.0 * jnp.arange(1, n_groups + 1, dtype=F32) / n_groups)).reshape(N_GROUPS, HEADS)
    scale = jnp.concatenate([jnp.full((GROUP_DIM,), HEAD_DIM ** -0.5, F32), jnp.ones((2 * GROUP_DIM,), F32)])
    os_, lses = [], []
    for g, (_, dil) in enumerate(DIL_CONFIGS):
        cols = [w_qkv[:, (j * N_GROUPS + g) * GROUP_DIM:(j * N_GROUPS + g + 1) * GROUP_DIM] for j in range(3)]
        w_g = (jnp.concatenate(cols, axis=1) * scale).astype(BF16)
        qkv = _qkv_project(x, w_g, dil)
        segs = tuple((n // dil, length // dil) for n, length in seq_rows)
        o, lse = _attention(qkv, _band_bias(slopes[g], dil), dil, segs)
        t = x.shape[0]
        os_.append(o.reshape(t, GROUP_DIM))
        lses.append(lse.reshape(t, LANES))
    return _merge_project(os_, lses, x, w_o.astype(BF16), b_o[None], ln_g[None], ln_b[None])


def kernel(x_prompt, x_sample, attn_w_qkv, attn_w_o, attn_b_o, sgu_w_in, sgu_b_in, sgu_norm_g, sgu_norm_b,
           sgu_w_s, sgu_b_s, sgu_w_out, sgu_b_out, router_w, router_b, exp_w1, exp_b1, exp_w2, exp_b2,
           ln_mix_g, ln_mix_b, ln_ffn_g, ln_ffn_b):
    d = x_prompt.shape[-1]
    tp = x_prompt.shape[0] * x_prompt.shape[1]
    ts = x_sample.shape[0] * x_sample.shape[1]
    seq_rows = ((tp, x_prompt.shape[1]), (ts, x_sample.shape[1]))
    x = jnp.concatenate([x_prompt.reshape(tp, d), x_sample.reshape(ts, d)], axis=0)

    b1g = exp_b1[:, :, None, 0::2]
    b1l = exp_b1[:, :, None, 1::2]
    b2 = exp_b2[:, :, None, :]

    outs = None
    for i in range(DEPTH):
        j = i // 2
        if i % 2 == 0:
            x = _attention_block(x, attn_w_qkv[j], attn_w_o[j], attn_b_o[j], ln_mix_g[i], ln_mix_b[i], seq_rows)
        else:
            x = _sgu_block(x, sgu_w_in[j].astype(BF16), sgu_b_in[j][None], sgu_norm_g[j][None], sgu_norm_b[j][None],
                           sgu_w_s[j].astype(BF16), jnp.transpose(sgu_b_s[j]), sgu_w_out[j].astype(BF16),
                           sgu_b_out[j][None], ln_mix_g[i][None], ln_mix_b[i][None])
        y, dest, gate = _moe_block(x, i, router_w[i], router_b[i][None], exp_w1, b1g, b1l, exp_w2, b2)
        g_ln, b_ln = ln_ffn_g[i][None], ln_ffn_b[i][None]
        if i + 1 < DEPTH:
            x = _combine(y, dest, gate, x, g_ln, b_ln, 0, tp + ts)
        else:
            outs = (_combine(y, dest, gate, x, g_ln, b_ln, 0, tp).reshape(x_prompt.shape),
                    _combine(y, dest, gate, x, g_ln, b_ln, tp, ts).reshape(x_sample.shape))
    return outs
```

```python
import functools

import jax
import jax.numpy as jnp
from jax import lax
from jax.experimental import pallas as pl
from jax.experimental.pallas import tpu as pltpu

F32 = jnp.float32
BF16 = jnp.bfloat16
U32 = jnp.uint32

LANES = 128
SUBLANES = 8
VMEM_LIMIT_CAP = 56 * 1024 * 1024

D_MODEL = 1024
DIL_CONFIGS = ((128, 1), (512, 4), (2048, 16))
N_GROUPS = len(DIL_CONFIGS)
HEADS = 8
HEAD_DIM = 64
GROUP_DIM = HEADS * HEAD_DIM
N_SIDE = 64
Q_BLOCK = 128
KV_BLOCK = N_SIDE
KV_WIN = Q_BLOCK + 2 * N_SIDE
ATTN_Q_STEPS = (4, 2, 1)
SGU_CHUNK = 128
SGU_HALF = 3 * D_MODEL
SGU_GROUPS = 8
SGU_GROUP_DIM = SGU_HALF // SGU_GROUPS
N_EXPERTS = 32
TOP_K = 4
D_EXPERT = D_MODEL
SWIGLU_ALPHA = 1.702
SWIGLU_LIMIT = 7.0
LN_EPS = 1e-5
DEPTH = 2
DEEPNORM_ALPHA = (2 * DEPTH) ** 0.25
NEG = -1e30

ROW_TILE = 512
RANK_CHUNK = 256
SGU_ROWS = 512
EXPERT_ROWS = 512
DMA_UNROLL = 8
DMA_QUEUES = 2


def _cparams(semantics, vmem_bytes):
    return pltpu.CompilerParams(dimension_semantics=semantics,
                                vmem_limit_bytes=min(int(vmem_bytes), VMEM_LIMIT_CAP))


def _layer_norm(y, g, b):
    mu = jnp.mean(y, axis=-1, keepdims=True)
    yc = y - mu
    var = jnp.mean(yc * yc, axis=-1, keepdims=True)
    return yc * lax.rsqrt(var + LN_EPS) * g + b


def _qkv_kernel(x_ref, w_ref, o_ref):
    o_ref[...] = jnp.dot(x_ref[...].astype(BF16), w_ref[...],
                         preferred_element_type=F32).astype(BF16)


def _qkv_project(x, w, dil):
    t, d = x.shape
    n = t // dil
    tm = min(512, n)
    width = w.shape[1]
    xv = x.reshape(n, dil * d)
    return pl.pallas_call(
        _qkv_kernel,
        grid=(dil, n // tm),
        in_specs=[pl.BlockSpec((tm, d), lambda r, i: (i, r)),
                  pl.BlockSpec((d, width), lambda r, i: (0, 0))],
        out_specs=pl.BlockSpec((None, tm, width), lambda r, i: (r, i, 0)),
        out_shape=jax.ShapeDtypeStruct((dil, n, width), BF16),
        compiler_params=_cparams(("parallel", "parallel"),
                                 2 * (tm * d * 4 + d * width * 2 + tm * width * 2) + tm * width * 4
                                 + (8 << 20)),
        name=f"qkv_project_d{dil}",
    )(xv, w)


def _attn_kernel(q_ref, kp_ref, kb_ref, kn_ref, vp_ref, vb_ref, vn_ref, bias_ref, o_ref, lse_ref, *, segs):
    (n_a, len_a), (_, len_b) = segs
    step_rows = q_ref.shape[0]
    lane = lax.broadcasted_iota(jnp.int32, (1, LANES), 1)
    first_head = lane < HEAD_DIM
    nt = (((1,), (1,)), ((), ()))

    def window(prev, body, nxt, first, sl):
        parts = []
        if first < 0:
            parts.append(prev[:, sl])
        parts.append(body[max(first, 0):min(first + KV_WIN, step_rows), sl])
        if first + KV_WIN > step_rows:
            parts.append(nxt[:, sl])
        return jnp.concatenate(parts, axis=0)

    for sub in range(step_rows // Q_BLOCK):
        rows = slice(sub * Q_BLOCK, (sub + 1) * Q_BLOCK)
        row0 = pl.program_id(1) * step_rows + sub * Q_BLOCK
        in_a = row0 < n_a
        lo = jnp.where(in_a, (row0 // len_a) * len_a, n_a + ((row0 - n_a) // len_b) * len_b)
        hi = lo + jnp.where(in_a, len_a, len_b)
        krow = lax.broadcasted_iota(jnp.int32, (1, KV_WIN), 1) + (row0 - N_SIDE)
        col_bias = jnp.where((krow >= lo) & (krow < hi), 0.0, NEG).astype(F32)
        lse_tile = jnp.zeros((Q_BLOCK, LANES), F32)
        for p in range(HEADS // 2):
            sl = slice(p * LANES, (p + 1) * LANES)
            q = q_ref[rows, sl]
            k = window(kp_ref, kb_ref, kn_ref, sub * Q_BLOCK - N_SIDE, sl)
            v = window(vp_ref, vb_ref, vn_ref, sub * Q_BLOCK - N_SIDE, sl)
            zero = jnp.zeros_like(q)
            outs, inv_ls = [], []
            for j, qh in enumerate((jnp.where(first_head, q, zero), jnp.where(first_head, zero, q))):
                h = 2 * p + j
                s = lax.dot_general(qh, k, nt, preferred_element_type=F32)
                s = s + bias_ref[h] + col_bias
                m = jnp.max(s, axis=-1, keepdims=True)
                e = jnp.exp(s - m)
                l = jnp.sum(e, axis=-1, keepdims=True)
                outs.append(jnp.dot(e.astype(BF16), v, preferred_element_type=F32))
                inv_ls.append(1.0 / l)
                lse_tile = jnp.where(lane == h, m + jnp.log(l), lse_tile)
            o = jnp.where(first_head, outs[0] * inv_ls[0], outs[1] * inv_ls[1])
            o_ref[rows, sl] = o.astype(BF16)
        lse_ref[rows, :] = lse_tile


def _band_bias(slopes, dil):
    qi = jnp.arange(Q_BLOCK)[:, None]
    kc = jnp.arange(KV_WIN)[None, :]
    dist = jnp.abs(kc - N_SIDE - qi)
    pen = -slopes[:, None, None] * (dist * dil).astype(F32)[None]
    return jnp.where((dist <= N_SIDE)[None], pen, NEG).astype(F32)


def _attention(qkv, bias, dil, segs):
    _, n, _ = qkv.shape
    nkb = n // KV_BLOCK
    q_step = max(s for s in ATTN_Q_STEPS if (n // Q_BLOCK) % s == 0)
    step_rows = q_step * Q_BLOCK
    per_step = step_rows // KV_BLOCK

    def body_spec(col):
        return pl.BlockSpec((None, step_rows, GROUP_DIM), lambda r, i: (r, i, col))

    def edge_spec(col, after):
        def imap(r, i):
            return (r, jnp.clip((i + after) * per_step - 1 + after, 0, nkb - 1), col)
        return pl.BlockSpec((None, KV_BLOCK, GROUP_DIM), imap)

    in_specs = ([body_spec(0)]
                + [edge_spec(1, 0), body_spec(1), edge_spec(1, 1)]
                + [edge_spec(2, 0), body_spec(2), edge_spec(2, 1)]
                + [pl.BlockSpec((HEADS, Q_BLOCK, KV_WIN), lambda r, i: (0, 0, 0))])
    return pl.pallas_call(
        functools.partial(_attn_kernel, segs=segs),
        grid=(dil, n // step_rows),
        in_specs=in_specs,
        out_specs=[pl.BlockSpec((step_rows, GROUP_DIM), lambda r, i: (i, r)),
                   pl.BlockSpec((step_rows, LANES), lambda r, i: (i, r))],
        out_shape=[jax.ShapeDtypeStruct((n, dil * GROUP_DIM), BF16),
                   jax.ShapeDtypeStruct((n, dil * LANES), F32)],
        compiler_params=_cparams(("parallel", "parallel"), 32 << 20),
        name=f"band_attention_d{dil}",
    )(*([qkv] * 7), bias)


def _merge_kernel(o0, o1, o2, l0, l1, l2, x_ref, wo_ref, bo_ref, g_ref, b_ref, y_ref):
    lses = [l0[...], l1[...], l2[...]]
    m = jnp.maximum(jnp.maximum(lses[0], lses[1]), lses[2])
    es = [jnp.exp(l - m) for l in lses]
    inv = 1.0 / (es[0] + es[1] + es[2])
    ws = [e * inv for e in es]
    lane = lax.broadcasted_iota(jnp.int32, (1, LANES), 1)
    first_head = lane < HEAD_DIM
    parts = []
    for p in range(HEADS // 2):
        sl = slice(p * LANES, (p + 1) * LANES)
        acc = None
        for w, o in zip(ws, (o0, o1, o2)):
            wp = jnp.where(first_head, w[:, 2 * p:2 * p + 1], w[:, 2 * p + 1:2 * p + 2])
            term = wp * o[:, sl].astype(F32)
            acc = term if acc is None else acc + term
        parts.append(acc)
    merged = jnp.concatenate(parts, axis=-1).astype(BF16)
    h = jnp.dot(merged, wo_ref[...], preferred_element_type=F32) + bo_ref[...]
    y_ref[...] = _layer_norm(DEEPNORM_ALPHA * x_ref[...] + h, g_ref[...], b_ref[...])


def _merge_project(os_, lses, x, w_o, b_o, ln_g, ln_b):
    t, d = x.shape
    tm = ROW_TILE
    row = lambda i: (i, 0)
    const = lambda i: (0, 0)
    vec = pl.BlockSpec((1, d), const)
    return pl.pallas_call(
        _merge_kernel,
        grid=(t // tm,),
        in_specs=([pl.BlockSpec((tm, GROUP_DIM), row)] * 3 + [pl.BlockSpec((tm, LANES), row)] * 3
                  + [pl.BlockSpec((tm, d), row), pl.BlockSpec((GROUP_DIM, d), const), vec, vec, vec]),
        out_specs=pl.BlockSpec((tm, d), row),
        out_shape=jax.ShapeDtypeStruct((t, d), F32),
        compiler_params=_cparams(("parallel",), 32 << 20),
        name="merge_outproj_ln",
    )(*os_, *lses, x, w_o, b_o, ln_g, ln_b)


def _split_bf16(a):
    hi = a.astype(BF16)
    return hi, (a - hi.astype(F32)).astype(BF16)


def _router_kernel(x_ref, w_ref, b_ref, e_ref, g_ref, r_ref, cnt_ref, carry_ref):
    tm = x_ref.shape[0]

    @pl.when(pl.program_id(0) == 0)
    def _():
        carry_ref[...] = jnp.zeros_like(carry_ref)

    xh, xl = _split_bf16(x_ref[...])
    wh, wl = _split_bf16(w_ref[...])
    dot = functools.partial(jnp.dot, preferred_element_type=F32)
    logits = dot(xh, wh) + (dot(xl, wh) + dot(xh, wl)) + b_ref[...]
    lane = lax.broadcasted_iota(jnp.int32, logits.shape, 1).astype(F32)
    col4 = lax.broadcasted_iota(jnp.int32, (tm, TOP_K), 1)
    work = logits
    picks, vals, idxs = [], [], []
    for _k in range(TOP_K):
        mx = jnp.max(work, axis=-1, keepdims=True)
        idx = jnp.min(jnp.where(work == mx, lane, float(N_EXPERTS)), axis=-1, keepdims=True)
        pick = lane == idx
        work = jnp.where(pick, -jnp.inf, work)
        picks.append(pick)
        vals.append(mx)
        idxs.append(idx)
    ex = [jnp.exp(v - vals[0]) for v in vals]
    inv = 1.0 / (ex[0] + ex[1] + ex[2] + ex[3])
    chosen = jnp.zeros(logits.shape, F32)
    for pick in picks:
        chosen = chosen + pick.astype(F32)
    ch = min(RANK_CHUNK, tm)
    r_i = lax.broadcasted_iota(jnp.int32, (ch, ch), 0)
    c_i = lax.broadcasted_iota(jnp.int32, (ch, ch), 1)
    tri = (c_i < r_i).astype(BF16)
    carry = carry_ref[...]
    before = []
    for c in range(tm // ch):
        part = chosen[c * ch:(c + 1) * ch]
        before.append(dot(tri, part.astype(BF16)) + carry)
        carry = carry + jnp.sum(part, axis=0, keepdims=True)
    before = jnp.concatenate(before, axis=0)
    e_out = jnp.zeros((tm, TOP_K), jnp.int32)
    g_out = jnp.zeros((tm, TOP_K), F32)
    r_out = jnp.zeros((tm, TOP_K), jnp.int32)
    for k in range(TOP_K):
        rank = jnp.sum(jnp.where(picks[k], before, 0.0), axis=-1, keepdims=True)
        e_out = jnp.where(col4 == k, idxs[k].astype(jnp.int32), e_out)
        g_out = jnp.where(col4 == k, ex[k] * inv, g_out)
        r_out = jnp.where(col4 == k, rank.astype(jnp.int32), r_out)
    e_ref[...] = e_out
    g_ref[...] = g_out
    r_ref[...] = r_out
    carry_ref[...] = carry
    cnt_ref[...] = carry


def _router(x, w, b):
    t, d = x.shape
    tm = ROW_TILE
    row = lambda i: (i, 0)
    const = lambda i: (0, 0)
    k_spec = pl.BlockSpec((tm, TOP_K), row)
    return pl.pallas_call(
        _router_kernel,
        grid=(t // tm,),
        in_specs=[pl.BlockSpec((tm, d), row), pl.BlockSpec((d, N_EXPERTS), const),
                  pl.BlockSpec((1, N_EXPERTS), const)],
        out_specs=[k_spec, k_spec, k_spec, pl.BlockSpec((1, N_EXPERTS), const)],
        out_shape=[jax.ShapeDtypeStruct((t, TOP_K), jnp.int32), jax.ShapeDtypeStruct((t, TOP_K), F32),
                   jax.ShapeDtypeStruct((t, TOP_K), jnp.int32), jax.ShapeDtypeStruct((1, N_EXPERTS), F32)],
        scratch_shapes=[pltpu.VMEM((1, N_EXPERTS), F32)],
        compiler_params=_cparams(("arbitrary",), 32 << 20),
        name="router_topk",
    )(x, w, b)


def _route_plan(e, rank, counts, bm, tm):
    t = e.shape[0]
    nb = TOP_K * t // bm + N_EXPERTS
    counts = counts.reshape(N_EXPERTS).astype(jnp.int32)
    padded = (counts + bm - 1) // bm * bm
    pend = jnp.cumsum(padded)
    pstart = pend - padded
    onehot = e[..., None] == jnp.arange(N_EXPERTS, dtype=jnp.int32)
    dest = jnp.sum(jnp.where(onehot, pstart, 0), axis=-1) + rank
    blk_row0 = jnp.arange(nb, dtype=jnp.int32) * bm
    blk_exp = jnp.minimum(jnp.sum((pend[None, :] <= blk_row0[:, None]).astype(jnp.int32), axis=1),
                          N_EXPERTS - 1)
    n_used = pend[-1:] // bm
    return dest.reshape(t // tm, tm * TOP_K), blk_exp, n_used, pstart + counts, pend, padded // bm


ROW_SUBLANES = D_MODEL // LANES


def _store_row_tiles(ref, base, x):
    for c in range(ROW_SUBLANES):
        ref[pl.ds(base + c, x.shape[0], stride=ROW_SUBLANES), :] = x[:, c * LANES:(c + 1) * LANES]


def _load_row_tiles(ref, base, n):
    return jnp.concatenate([ref[pl.ds(base + c, n, stride=ROW_SUBLANES), :] for c in range(ROW_SUBLANES)], axis=1)


def _tile_rows(ref, row):
    return ref.at[pl.ds(pl.multiple_of(row * ROW_SUBLANES, ROW_SUBLANES), ROW_SUBLANES), :]


def _dispatch_kernel(padlo_ref, pend_ref, nused_ref, dest_hbm, x_ref, xs_hbm, idx0, idx1, pk, zbuf, sems):
    i = pl.program_id(0)
    n = pl.num_programs(0)
    tm = x_ref.shape[0]
    bm = zbuf.shape[0] // ROW_SUBLANES
    dump = xs_hbm.shape[0] // ROW_SUBLANES - bm
    idx = (idx0, idx1)
    slot_rows = tm * ROW_SUBLANES

    def idx_copy(step, s):
        return pltpu.make_async_copy(dest_hbm.at[step], idx[s], sems.at[s])

    def rows_wait(s):
        for _ in range(TOP_K):
            pltpu.make_async_copy(pk.at[pl.ds(s * slot_rows, slot_rows), :], xs_hbm.at[pl.ds(0, slot_rows), :],
                                  sems.at[2 + s]).wait()

    @pl.when(i == 0)
    def _():
        idx_copy(0, 0).start()
        zbuf[...] = jnp.zeros_like(zbuf)

        def zero_block(b, c):
            cp = pltpu.make_async_copy(zbuf, xs_hbm.at[pl.ds(pl.multiple_of(b * zbuf.shape[0], ROW_SUBLANES),
                                                             zbuf.shape[0]), :], sems.at[4])
            cp.start()
            cp.wait()
            return c
        lax.fori_loop(nused_ref[0], dump // bm + 1, zero_block, 0)
        for e in range(N_EXPERTS):
            def zero_row(jj, c, e=e):
                for u in range(DMA_QUEUES):
                    j = jj * DMA_QUEUES + u
                    r = padlo_ref[e] + j
                    r = jnp.where(r < pend_ref[e], r, dump + j)
                    pltpu.make_async_copy(_tile_rows(zbuf, j), _tile_rows(xs_hbm, r), sems.at[4]).start(priority=u)
                return c
            lax.fori_loop(0, bm // DMA_QUEUES, zero_row, 0, unroll=DMA_UNROLL // DMA_QUEUES)
            pltpu.make_async_copy(zbuf, xs_hbm.at[pl.ds(0, zbuf.shape[0]), :], sems.at[4]).wait()

    def step(s):
        idx_copy(i, s).wait()

        @pl.when(i + 1 < n)
        def _():
            idx_copy(i + 1, 1 - s).start()

        @pl.when(i >= 2)
        def _():
            rows_wait(s)

        _store_row_tiles(pk, s * slot_rows, x_ref[...])

        def issue(j, c):
            for k in range(TOP_K):
                pltpu.make_async_copy(_tile_rows(pk, s * tm + j), _tile_rows(xs_hbm, idx[s][j * TOP_K + k]),
                                      sems.at[2 + s]).start(priority=k % DMA_QUEUES)
            return c
        lax.fori_loop(0, tm, issue, 0, unroll=DMA_UNROLL)

        @pl.when(i == n - 1)
        def _():
            rows_wait(s)

        @pl.when(jnp.logical_and(i == n - 1, n >= 2))
        def _():
            rows_wait(1 - s)

    for s in range(2):
        pl.when(i % 2 == s)(functools.partial(step, s))


def _dispatch(x, dest, pad_lo, pend, n_used, bm):
    t, d = x.shape
    nblk, per = dest.shape
    tm = per // TOP_K
    nb = TOP_K * t // bm + N_EXPERTS
    grid_spec = pltpu.PrefetchScalarGridSpec(
        num_scalar_prefetch=3,
        grid=(nblk,),
        in_specs=[pl.BlockSpec(memory_space=pl.ANY), pl.BlockSpec((tm, d), lambda i, a, b, c: (i, 0))],
        out_specs=pl.BlockSpec(memory_space=pl.ANY),
        scratch_shapes=[pltpu.SMEM((per,), jnp.int32), pltpu.SMEM((per,), jnp.int32),
                        pltpu.VMEM((2 * tm * ROW_SUBLANES, LANES), F32),
                        pltpu.VMEM((bm * ROW_SUBLANES, LANES), F32),
                        pltpu.SemaphoreType.DMA((5,))])
    return pl.pallas_call(
        _dispatch_kernel,
        grid_spec=grid_spec,
        out_shape=jax.ShapeDtypeStruct(((nb * bm + bm) * ROW_SUBLANES, LANES), F32),
        compiler_params=_cparams(("arbitrary",), 32 << 20),
        name="moe_dispatch",
    )(pad_lo, pend, n_used, dest, x)


def _expert_kernel(be_ref, nused_ref, gblk_ref, xs_ref, w1_hbm, b1g_ref, b1l_ref, w2_hbm, b2_ref, y_ref,
                   w1gt, w1lt, w2b, tbuf, w1s, w2s, slot_ref, sems, *, layer):
    i = pl.program_id(0)
    d, de = w2b.shape[1], w2b.shape[0]
    nt = (((1,), (1,)), ((), ()))

    def weight_copies(e, s):
        return (pltpu.make_async_copy(w1_hbm.at[layer, e], w1s.at[s], sems.at[0, s]),
                pltpu.make_async_copy(w2_hbm.at[layer, e], w2s.at[s], sems.at[1, s]))

    @pl.when(i < nused_ref[0])
    def _():
        e = be_ref[i]
        changed = jnp.logical_or(i == 0, e != be_ref[jnp.maximum(i - 1, 0)])

        @pl.when(changed)
        def _():
            @pl.when(i == 0)
            def _():
                slot_ref[0] = 0
                for cp in weight_copies(e, 0):
                    cp.start()

            s = slot_ref[0]
            for cp in weight_copies(e, s):
                cp.wait()
            j = i + gblk_ref[e]

            @pl.when(j < nused_ref[0])
            def _():
                for cp in weight_copies(be_ref[jnp.minimum(j, be_ref.shape[0] - 1)], 1 - s):
                    cp.start()

            for a in range(d // LANES):
                cols = slice(a * LANES, (a + 1) * LANES)
                tbuf[...] = w1s[s, cols, :].T
                w1gt[:, cols] = tbuf[pl.ds(0, de, stride=2), :].astype(BF16)
                w1lt[:, cols] = tbuf[pl.ds(1, de, stride=2), :].astype(BF16)
            w2b[...] = w2s[s].astype(BF16)
            slot_ref[0] = 1 - s

        bm = xs_ref.shape[0] // ROW_SUBLANES
        x = _load_row_tiles(xs_ref, 0, bm).astype(BF16)
        hg = lax.dot_general(x, w1gt[...], nt, preferred_element_type=F32) + b1g_ref[...]
        hl = lax.dot_general(x, w1lt[...], nt, preferred_element_type=F32) + b1l_ref[...]
        hg = jnp.minimum(hg, SWIGLU_LIMIT)
        hl = jnp.clip(hl, -SWIGLU_LIMIT, SWIGLU_LIMIT)
        act = hg * (1.0 / (1.0 + jnp.exp(-SWIGLU_ALPHA * hg))) * (hl + 1.0)
        y = jnp.dot(act.astype(BF16), w2b[...], preferred_element_type=F32) + b2_ref[...]
        _store_row_tiles(y_ref, 0, y)

    @pl.when(i >= nused_ref[0])
    def _():
        y_ref[...] = jnp.zeros_like(y_ref)


def _experts(xs, blk_exp, n_used, grp_blocks, layer, w1, b1g, b1l, w2, b2, bm):
    d = ROW_SUBLANES * LANES
    nb = blk_exp.shape[0]
    de = w2.shape[2]
    blk = (bm * ROW_SUBLANES, LANES)
    bmap = lambda i, be, nu, gb: (layer, be[i], 0, 0)
    any_spec = pl.BlockSpec(memory_space=pl.ANY)
    grid_spec = pltpu.PrefetchScalarGridSpec(
        num_scalar_prefetch=3,
        grid=(nb,),
        in_specs=[pl.BlockSpec(blk, lambda i, be, nu, gb: (jnp.minimum(i, nu[0] - 1), 0)),
                  any_spec,
                  pl.BlockSpec((None, None, 1, de), bmap), pl.BlockSpec((None, None, 1, de), bmap),
                  any_spec,
                  pl.BlockSpec((None, None, 1, d), bmap)],
        out_specs=pl.BlockSpec(blk, lambda i, be, nu, gb: (i, 0)),
        scratch_shapes=[pltpu.VMEM((de, d), BF16), pltpu.VMEM((de, d), BF16), pltpu.VMEM((de, d), BF16),
                        pltpu.VMEM((2 * de, LANES), F32),
                        pltpu.VMEM((2, d, 2 * de), F32), pltpu.VMEM((2, de, d), F32),
                        pltpu.SMEM((1,), jnp.int32), pltpu.SemaphoreType.DMA((2, 2))])
    return pl.pallas_call(
        functools.partial(_expert_kernel, layer=layer),
        grid_spec=grid_spec,
        out_shape=jax.ShapeDtypeStruct((nb * blk[0], LANES), F32),
        compiler_params=_cparams(("arbitrary",), 54 << 20),
        name="moe_experts",
    )(blk_exp, n_used, grp_blocks, xs, w1, b1g, b1l, w2, b2)


def _combine_kernel(dest_hbm, gate_ref, x_ref, g_ref, b_ref, y_hbm, o_ref, idx0, idx1, ybuf, sems, *, blk_off):
    i = pl.program_id(0)
    n = pl.num_programs(0)
    tm = x_ref.shape[0]
    idx = (idx0, idx1)
    slab_rows = tm * ROW_SUBLANES

    def slab(s, k):
        return (s * TOP_K + k) * tm

    def idx_copy(step, s):
        return pltpu.make_async_copy(dest_hbm.at[blk_off + step], idx[s], sems.at[s])

    def issue_rows(s):
        def body(j, c):
            for k in range(TOP_K):
                pltpu.make_async_copy(_tile_rows(y_hbm, idx[s][j * TOP_K + k]), _tile_rows(ybuf, slab(s, k) + j),
                                      sems.at[2 + s]).start(priority=k % DMA_QUEUES)
            return c
        lax.fori_loop(0, tm, body, 0, unroll=DMA_UNROLL)

    @pl.when(i == 0)
    def _():
        cp = idx_copy(0, 0)
        cp.start()
        cp.wait()
        issue_rows(0)

    @pl.when(jnp.logical_and(i == 0, n >= 2))
    def _():
        idx_copy(1, 1).start()

    def step(s):
        @pl.when(i + 1 < n)
        def _():
            idx_copy(i + 1, 1 - s).wait()
            issue_rows(1 - s)

        @pl.when(i + 2 < n)
        def _():
            idx_copy(i + 2, s).start()

        for k in range(TOP_K):
            pltpu.make_async_copy(y_hbm.at[pl.ds(0, slab_rows), :],
                                  ybuf.at[pl.ds(slab(s, k) * ROW_SUBLANES, slab_rows), :], sems.at[2 + s]).wait()
        gate = gate_ref[...]
        h = None
        for k in range(TOP_K):
            term = gate[:, k:k + 1] * _load_row_tiles(ybuf, slab(s, k) * ROW_SUBLANES, tm)
            h = term if h is None else h + term
        o_ref[...] = _layer_norm(DEEPNORM_ALPHA * x_ref[...] + h, g_ref[...], b_ref[...])

    for s in range(2):
        pl.when(i % 2 == s)(functools.partial(step, s))


def _combine(y, dest, gate, x, ln_g, ln_b, row_lo, rows):
    _, d = x.shape
    per = dest.shape[1]
    tm = per // TOP_K
    off = row_lo // tm
    vec = pl.BlockSpec((1, d), lambda i: (0, 0))
    return pl.pallas_call(
        functools.partial(_combine_kernel, blk_off=off),
        grid=(rows // tm,),
        in_specs=[pl.BlockSpec(memory_space=pl.ANY),
                  pl.BlockSpec((tm, TOP_K), lambda i: (off + i, 0)),
                  pl.BlockSpec((tm, d), lambda i: (off + i, 0)), vec, vec,
                  pl.BlockSpec(memory_space=pl.ANY)],
        out_specs=pl.BlockSpec((tm, d), lambda i: (i, 0)),
        out_shape=jax.ShapeDtypeStruct((rows, d), F32),
        scratch_shapes=[pltpu.SMEM((per,), jnp.int32), pltpu.SMEM((per,), jnp.int32),
                        pltpu.VMEM((2 * TOP_K * tm * ROW_SUBLANES, LANES), F32),
                        pltpu.SemaphoreType.DMA((4,))],
        compiler_params=_cparams(("arbitrary",), 40 << 20),
        name="moe_combine_ln",
    )(dest, gate, x, ln_g, ln_b, y)


def _moe_block(x, layer, router_w, router_b, w1, b1g, b1l, w2, b2):
    e, gate, rank, counts = _router(x, router_w, router_b)
    dest, blk_exp, n_used, pad_lo, pend, grp_blocks = _route_plan(e, rank, counts, EXPERT_ROWS, ROW_TILE)
    xs = _dispatch(x, dest, pad_lo, pend, n_used, EXPERT_ROWS)
    y = _experts(xs, blk_exp, n_used, grp_blocks, layer, w1, b1g, b1l, w2, b2, EXPERT_ROWS)
    return y, dest, gate


def _gelu(z):
    return 0.5 * z * (1.0 + jnp.tanh(0.7978845608028654 * (z + 0.044715 * (z * z * z))))


def _sgu_kernel(x_ref, win_ref, bin_ref, ng_ref, nb_ref, ws_ref, bs_ref, wout_ref, bout_ref,
                g_ref, b_ref, y_ref):
    tm = x_ref.shape[0]
    x = x_ref[...]
    xb = x.astype(BF16)
    dot = functools.partial(jnp.dot, preferred_element_type=F32)
    v = _gelu(dot(xb, win_ref[:, SGU_HALF:]) + bin_ref[:, SGU_HALF:])
    v = _layer_norm(v, ng_ref[...], nb_ref[...]).astype(BF16)
    acc = jnp.zeros((tm, D_MODEL), F32)
    for g in range(SGU_GROUPS):
        cols = slice(g * SGU_GROUP_DIM, (g + 1) * SGU_GROUP_DIM)
        u = _gelu(dot(xb, win_ref[:, cols]) + bin_ref[:, cols])
        mixed = [dot(ws_ref[g], v[c * SGU_CHUNK:(c + 1) * SGU_CHUNK, cols]) + bs_ref[:, g:g + 1]
                 for c in range(tm // SGU_CHUNK)]
        mixed = jnp.concatenate(mixed, axis=0)
        acc = acc + dot((u * mixed).astype(BF16), wout_ref[cols, :])
    h = acc + bout_ref[...]
    y_ref[...] = _layer_norm(DEEPNORM_ALPHA * x + h, g_ref[...], b_ref[...])


def _sgu_block(x, w_in, b_in, norm_g, norm_b, w_s, b_s_t, w_out, b_out, ln_g, ln_b):
    t, d = x.shape
    tm = SGU_ROWS
    row = lambda i: (i, 0)
    c2 = lambda i: (0, 0)
    c3 = lambda i: (0, 0, 0)
    once = dict(pipeline_mode=pl.Buffered(1))
    vec = pl.BlockSpec((1, d), c2)
    return pl.pallas_call(
        _sgu_kernel,
        grid=(t // tm,),
        in_specs=[pl.BlockSpec((tm, d), row),
                  pl.BlockSpec((d, 2 * SGU_HALF), c2, **once), pl.BlockSpec((1, 2 * SGU_HALF), c2),
                  pl.BlockSpec((1, SGU_HALF), c2), pl.BlockSpec((1, SGU_HALF), c2),
                  pl.BlockSpec((SGU_GROUPS, SGU_CHUNK, SGU_CHUNK), c3),
                  pl.BlockSpec((SGU_CHUNK, SGU_GROUPS), c2),
                  pl.BlockSpec((SGU_HALF, d), c2, **once), vec, vec, vec],
        out_specs=pl.BlockSpec((tm, d), row),
        out_shape=jax.ShapeDtypeStruct((t, d), F32),
        compiler_params=_cparams(("parallel",), 52 << 20),
        name="sgu_block",
    )(x, w_in, b_in, norm_g, norm_b, w_s, b_s_t, w_out, b_out, ln_g, ln_b)


def _attention_block(x, w_qkv, w_o, b_o, ln_g, ln_b, seq_rows):
    n_groups = N_GROUPS * HEADS
    slopes = (2.0 ** (-8.0 * jnp.arange(1, n_groups + 1, dtype=F32) / n_groups)).reshape(N_GROUPS, HEADS)
    scale = jnp.concatenate([jnp.full((GROUP_DIM,), HEAD_DIM ** -0.5, F32), jnp.ones((2 * GROUP_DIM,), F32)])
    os_, lses = [], []
    for g, (_, dil) in enumerate(DIL_CONFIGS):
        cols = [w_qkv[:, (j * N_GROUPS + g) * GROUP_DIM:(j * N_GROUPS + g + 1) * GROUP_DIM] for j in range(3)]
        w_g = (jnp.concatenate(cols, axis=1) * scale).astype(BF16)
        qkv = _qkv_project(x, w_g, dil)
        segs = tuple((n // dil, length // dil) for n, length in seq_rows)
        o, lse = _attention(qkv, _band_bias(slopes[g], dil), dil, segs)
        t = x.shape[0]
        os_.append(o.reshape(t, GROUP_DIM))
        lses.append(lse.reshape(t, LANES))
    return _merge_project(os_, lses, x, w_o.astype(BF16), b_o[None], ln_g[None], ln_b[None])


def kernel(x_prompt, x_sample, attn_w_qkv, attn_w_o, attn_b_o, sgu_w_in, sgu_b_in, sgu_norm_g, sgu_norm_b,
           sgu_w_s, sgu_b_s, sgu_w_out, sgu_b_out, router_w, router_b, exp_w1, exp_b1, exp_w2, exp_b2,
           ln_mix_g, ln_mix_b, ln_ffn_g, ln_ffn_b):
    d = x_prompt.shape[-1]
    tp = x_prompt.shape[0] * x_prompt.shape[1]
    ts = x_sample.shape[0] * x_sample.shape[1]
    seq_rows = ((tp, x_prompt.shape[1]), (ts, x_sample.shape[1]))
    x = jnp.concatenate([x_prompt.reshape(tp, d), x_sample.reshape(ts, d)], axis=0)

    b1g = exp_b1[:, :, None, 0::2]
    b1l = exp_b1[:, :, None, 1::2]
    b2 = exp_b2[:, :, None, :]

    outs = None
    for i in range(DEPTH):
        j = i // 2
        if i % 2 == 0:
            x = _attention_block(x, attn_w_qkv[j], attn_w_o[j], attn_b_o[j], ln_mix_g[i], ln_mix_b[i], seq_rows)
        else:
            x = _sgu_block(x, sgu_w_in[j].astype(BF16), sgu_b_in[j][None], sgu_norm_g[j][None], sgu_norm_b[j][None],
                           sgu_w_s[j].astype(BF16), jnp.transpose(sgu_b_s[j]), sgu_w_out[j].astype(BF16),
                           sgu_b_out[j][None], ln_mix_g[i][None], ln_mix_b[i][None])
        y, dest, gate = _moe_block(x, i, router_w[i], router_b[i][None], exp_w1, b1g, b1l, exp_w2, b2)
        g_ln, b_ln = ln_ffn_g[i][None], ln_ffn_b[i][None]
        if i + 1 < DEPTH:
            x = _combine(y, dest, gate, x, g_ln, b_ln, 0, tp + ts)
        else:
            outs = (_combine(y, dest, gate, x, g_ln, b_ln, 0, tp).reshape(x_prompt.shape),
                    _combine(y, dest, gate, x, g_ln, b_ln, tp, ts).reshape(x_sample.shape))
    return outs
```

```python
import functools

import jax
import jax.numpy as jnp
from jax import lax
from jax.experimental import pallas as pl
from jax.experimental.pallas import tpu as pltpu

F32 = jnp.float32
BF16 = jnp.bfloat16
U32 = jnp.uint32

LANES = 128
SUBLANES = 8
VMEM_LIMIT_CAP = 56 * 1024 * 1024

D_MODEL = 1024
DIL_CONFIGS = ((128, 1), (512, 4), (2048, 16))
N_GROUPS = len(DIL_CONFIGS)
HEADS = 8
HEAD_DIM = 64
GROUP_DIM = HEADS * HEAD_DIM
N_SIDE = 64
Q_BLOCK = 128
KV_BLOCK = N_SIDE
KV_WIN = Q_BLOCK + 2 * N_SIDE
ATTN_Q_STEPS = (4, 2, 1)
SGU_CHUNK = 128
SGU_HALF = 3 * D_MODEL
SGU_GROUPS = 8
SGU_GROUP_DIM = SGU_HALF // SGU_GROUPS
N_EXPERTS = 32
TOP_K = 4
D_EXPERT = D_MODEL
SWIGLU_ALPHA = 1.702
SWIGLU_LIMIT = 7.0
LN_EPS = 1e-5
DEPTH = 2
DEEPNORM_ALPHA = (2 * DEPTH) ** 0.25
NEG = -1e30

ROW_TILE = 512
RANK_CHUNK = 256
SGU_ROWS = 512
EXPERT_ROWS = 512
DMA_UNROLL = 8
DMA_QUEUES = 2
RUN_CHUNK = 64


def _cparams(semantics, vmem_bytes):
    return pltpu.CompilerParams(dimension_semantics=semantics,
                                vmem_limit_bytes=min(int(vmem_bytes), VMEM_LIMIT_CAP))


def _layer_norm(y, g, b):
    mu = jnp.mean(y, axis=-1, keepdims=True)
    yc = y - mu
    var = jnp.mean(yc * yc, axis=-1, keepdims=True)
    return yc * lax.rsqrt(var + LN_EPS) * g + b


def _qkv_kernel(x_ref, w_ref, o_ref):
    o_ref[...] = jnp.dot(x_ref[...].astype(BF16), w_ref[...],
                         preferred_element_type=F32).astype(BF16)


def _qkv_project(x, w, dil):
    t, d = x.shape
    n = t // dil
    tm = min(512, n)
    width = w.shape[1]
    xv = x.reshape(n, dil * d)
    return pl.pallas_call(
        _qkv_kernel,
        grid=(dil, n // tm),
        in_specs=[pl.BlockSpec((tm, d), lambda r, i: (i, r)),
                  pl.BlockSpec((d, width), lambda r, i: (0, 0))],
        out_specs=pl.BlockSpec((None, tm, width), lambda r, i: (r, i, 0)),
        out_shape=jax.ShapeDtypeStruct((dil, n, width), BF16),
        compiler_params=_cparams(("parallel", "parallel"),
                                 2 * (tm * d * 4 + d * width * 2 + tm * width * 2) + tm * width * 4
                                 + (8 << 20)),
        name=f"qkv_project_d{dil}",
    )(xv, w)


def _attn_kernel(q_ref, kp_ref, kb_ref, kn_ref, vp_ref, vb_ref, vn_ref, bias_ref, o_ref, lse_ref, *, segs):
    (n_a, len_a), (_, len_b) = segs
    step_rows = q_ref.shape[0]
    lane = lax.broadcasted_iota(jnp.int32, (1, LANES), 1)
    first_head = lane < HEAD_DIM
    nt = (((1,), (1,)), ((), ()))

    def window(prev, body, nxt, first, sl):
        parts = []
        if first < 0:
            parts.append(prev[:, sl])
        parts.append(body[max(first, 0):min(first + KV_WIN, step_rows), sl])
        if first + KV_WIN > step_rows:
            parts.append(nxt[:, sl])
        return jnp.concatenate(parts, axis=0)

    for sub in range(step_rows // Q_BLOCK):
        rows = slice(sub * Q_BLOCK, (sub + 1) * Q_BLOCK)
        row0 = pl.program_id(1) * step_rows + sub * Q_BLOCK
        in_a = row0 < n_a
        lo = jnp.where(in_a, (row0 // len_a) * len_a, n_a + ((row0 - n_a) // len_b) * len_b)
        hi = lo + jnp.where(in_a, len_a, len_b)
        krow = lax.broadcasted_iota(jnp.int32, (1, KV_WIN), 1) + (row0 - N_SIDE)
        col_bias = jnp.where((krow >= lo) & (krow < hi), 0.0, NEG).astype(F32)
        lse_tile = jnp.zeros((Q_BLOCK, LANES), F32)
        for p in range(HEADS // 2):
            sl = slice(p * LANES, (p + 1) * LANES)
            q = q_ref[rows, sl]
            k = window(kp_ref, kb_ref, kn_ref, sub * Q_BLOCK - N_SIDE, sl)
            v = window(vp_ref, vb_ref, vn_ref, sub * Q_BLOCK - N_SIDE, sl)
            zero = jnp.zeros_like(q)
            outs, inv_ls = [], []
            for j, qh in enumerate((jnp.where(first_head, q, zero), jnp.where(first_head, zero, q))):
                h = 2 * p + j
                s = lax.dot_general(qh, k, nt, preferred_element_type=F32)
                s = s + bias_ref[h] + col_bias
                m = jnp.max(s, axis=-1, keepdims=True)
                e = jnp.exp(s - m)
                l = jnp.sum(e, axis=-1, keepdims=True)
                outs.append(jnp.dot(e.astype(BF16), v, preferred_element_type=F32))
                inv_ls.append(1.0 / l)
                lse_tile = jnp.where(lane == h, m + jnp.log(l), lse_tile)
            o = jnp.where(first_head, outs[0] * inv_ls[0], outs[1] * inv_ls[1])
            o_ref[rows, sl] = o.astype(BF16)
        lse_ref[rows, :] = lse_tile


def _band_bias(slopes, dil):
    qi = jnp.arange(Q_BLOCK)[:, None]
    kc = jnp.arange(KV_WIN)[None, :]
    dist = jnp.abs(kc - N_SIDE - qi)
    pen = -slopes[:, None, None] * (dist * dil).astype(F32)[None]
    return jnp.where((dist <= N_SIDE)[None], pen, NEG).astype(F32)


def _attention(qkv, bias, dil, segs):
    _, n, _ = qkv.shape
    nkb = n // KV_BLOCK
    q_step = max(s for s in ATTN_Q_STEPS if (n // Q_BLOCK) % s == 0)
    step_rows = q_step * Q_BLOCK
    per_step = step_rows // KV_BLOCK

    def body_spec(col):
        return pl.BlockSpec((None, step_rows, GROUP_DIM), lambda r, i: (r, i, col))

    def edge_spec(col, after):
        def imap(r, i):
            return (r, jnp.clip((i + after) * per_step - 1 + after, 0, nkb - 1), col)
        return pl.BlockSpec((None, KV_BLOCK, GROUP_DIM), imap)

    in_specs = ([body_spec(0)]
                + [edge_spec(1, 0), body_spec(1), edge_spec(1, 1)]
                + [edge_spec(2, 0), body_spec(2), edge_spec(2, 1)]
                + [pl.BlockSpec((HEADS, Q_BLOCK, KV_WIN), lambda r, i: (0, 0, 0))])
    return pl.pallas_call(
        functools.partial(_attn_kernel, segs=segs),
        grid=(dil, n // step_rows),
        in_specs=in_specs,
        out_specs=[pl.BlockSpec((step_rows, GROUP_DIM), lambda r, i: (i, r)),
                   pl.BlockSpec((step_rows, LANES), lambda r, i: (i, r))],
        out_shape=[jax.ShapeDtypeStruct((n, dil * GROUP_DIM), BF16),
                   jax.ShapeDtypeStruct((n, dil * LANES), F32)],
        compiler_params=_cparams(("parallel", "parallel"), 32 << 20),
        name=f"band_attention_d{dil}",
    )(*([qkv] * 7), bias)


def _merge_kernel(o0, o1, o2, l0, l1, l2, x_ref, wo_ref, bo_ref, g_ref, b_ref, y_ref):
    lses = [l0[...], l1[...], l2[...]]
    m = jnp.maximum(jnp.maximum(lses[0], lses[1]), lses[2])
    es = [jnp.exp(l - m) for l in lses]
    inv = 1.0 / (es[0] + es[1] + es[2])
    ws = [e * inv for e in es]
    lane = lax.broadcasted_iota(jnp.int32, (1, LANES), 1)
    first_head = lane < HEAD_DIM
    parts = []
    for p in range(HEADS // 2):
        sl = slice(p * LANES, (p + 1) * LANES)
        acc = None
        for w, o in zip(ws, (o0, o1, o2)):
            wp = jnp.where(first_head, w[:, 2 * p:2 * p + 1], w[:, 2 * p + 1:2 * p + 2])
            term = wp * o[:, sl].astype(F32)
            acc = term if acc is None else acc + term
        parts.append(acc)
    merged = jnp.concatenate(parts, axis=-1).astype(BF16)
    h = jnp.dot(merged, wo_ref[...], preferred_element_type=F32) + bo_ref[...]
    y_ref[...] = _layer_norm(DEEPNORM_ALPHA * x_ref[...] + h, g_ref[...], b_ref[...])


def _merge_project(os_, lses, x, w_o, b_o, ln_g, ln_b):
    t, d = x.shape
    tm = ROW_TILE
    row = lambda i: (i, 0)
    const = lambda i: (0, 0)
    vec = pl.BlockSpec((1, d), const)
    return pl.pallas_call(
        _merge_kernel,
        grid=(t // tm,),
        in_specs=([pl.BlockSpec((tm, GROUP_DIM), row)] * 3 + [pl.BlockSpec((tm, LANES), row)] * 3
                  + [pl.BlockSpec((tm, d), row), pl.BlockSpec((GROUP_DIM, d), const), vec, vec, vec]),
        out_specs=pl.BlockSpec((tm, d), row),
        out_shape=jax.ShapeDtypeStruct((t, d), F32),
        compiler_params=_cparams(("parallel",), 32 << 20),
        name="merge_outproj_ln",
    )(*os_, *lses, x, w_o, b_o, ln_g, ln_b)


def _split_bf16(a):
    hi = a.astype(BF16)
    return hi, (a - hi.astype(F32)).astype(BF16)


def _router_kernel(x_ref, w_ref, b_ref, e_ref, g_ref, r_ref, p_ref, bcnt_ref, bstart_ref, cnt_ref, carry_ref):
    tm = x_ref.shape[0]

    @pl.when(pl.program_id(0) == 0)
    def _():
        carry_ref[...] = jnp.zeros_like(carry_ref)

    xh, xl = _split_bf16(x_ref[...])
    wh, wl = _split_bf16(w_ref[...])
    dot = functools.partial(jnp.dot, preferred_element_type=F32)
    logits = dot(xh, wh) + (dot(xl, wh) + dot(xh, wl)) + b_ref[...]
    lane = lax.broadcasted_iota(jnp.int32, logits.shape, 1).astype(F32)
    col4 = lax.broadcasted_iota(jnp.int32, (tm, TOP_K), 1)
    work = logits
    picks, vals, idxs = [], [], []
    for _k in range(TOP_K):
        mx = jnp.max(work, axis=-1, keepdims=True)
        idx = jnp.min(jnp.where(work == mx, lane, float(N_EXPERTS)), axis=-1, keepdims=True)
        pick = lane == idx
        work = jnp.where(pick, -jnp.inf, work)
        picks.append(pick)
        vals.append(mx)
        idxs.append(idx)
    ex = [jnp.exp(v - vals[0]) for v in vals]
    inv = 1.0 / (ex[0] + ex[1] + ex[2] + ex[3])
    chosen = jnp.zeros(logits.shape, F32)
    for pick in picks:
        chosen = chosen + pick.astype(F32)
    ch = min(RANK_CHUNK, tm)
    r_i = lax.broadcasted_iota(jnp.int32, (ch, ch), 0)
    c_i = lax.broadcasted_iota(jnp.int32, (ch, ch), 1)
    tri = (c_i < r_i).astype(BF16)
    carry0 = carry_ref[...]
    carry = carry0
    before = []
    for c in range(tm // ch):
        part = chosen[c * ch:(c + 1) * ch]
        before.append(dot(tri, part.astype(BF16)) + carry)
        carry = carry + jnp.sum(part, axis=0, keepdims=True)
    before = jnp.concatenate(before, axis=0)
    blk_cnt = carry - carry0
    e_out = jnp.zeros((tm, TOP_K), jnp.int32)
    g_out = jnp.zeros((tm, TOP_K), F32)
    r_out = jnp.zeros((tm, TOP_K), jnp.int32)
    p_out = jnp.zeros((tm, TOP_K), jnp.int32)
    for k in range(TOP_K):
        rank = jnp.sum(jnp.where(picks[k], before, 0.0), axis=-1, keepdims=True)
        lower = jnp.sum(jnp.where(lane < idxs[k], blk_cnt, 0.0), axis=-1, keepdims=True)
        inside = jnp.sum(jnp.where(picks[k], before - carry0, 0.0), axis=-1, keepdims=True)
        e_out = jnp.where(col4 == k, idxs[k].astype(jnp.int32), e_out)
        g_out = jnp.where(col4 == k, ex[k] * inv, g_out)
        r_out = jnp.where(col4 == k, rank.astype(jnp.int32), r_out)
        p_out = jnp.where(col4 == k, (lower + inside).astype(jnp.int32), p_out)
    e_ref[...] = e_out
    g_ref[...] = g_out
    r_ref[...] = r_out
    p_ref[...] = p_out
    bcnt_ref[...] = blk_cnt
    bstart_ref[...] = carry0
    carry_ref[...] = carry
    cnt_ref[...] = carry


def _router(x, w, b):
    t, d = x.shape
    tm = ROW_TILE
    row = lambda i: (i, 0)
    const = lambda i: (0, 0)
    k_spec = pl.BlockSpec((tm, TOP_K), row)
    blk_spec = pl.BlockSpec((None, 1, N_EXPERTS), lambda i: (i, 0, 0))
    k_i32 = jax.ShapeDtypeStruct((t, TOP_K), jnp.int32)
    per_blk = jax.ShapeDtypeStruct((t // tm, 1, N_EXPERTS), F32)
    return pl.pallas_call(
        _router_kernel,
        grid=(t // tm,),
        in_specs=[pl.BlockSpec((tm, d), row), pl.BlockSpec((d, N_EXPERTS), const),
                  pl.BlockSpec((1, N_EXPERTS), const)],
        out_specs=[k_spec, k_spec, k_spec, k_spec, blk_spec, blk_spec, pl.BlockSpec((1, N_EXPERTS), const)],
        out_shape=[k_i32, jax.ShapeDtypeStruct((t, TOP_K), F32), k_i32, k_i32, per_blk, per_blk,
                   jax.ShapeDtypeStruct((1, N_EXPERTS), F32)],
        scratch_shapes=[pltpu.VMEM((1, N_EXPERTS), F32)],
        compiler_params=_cparams(("arbitrary",), 32 << 20),
        name="router_topk",
    )(x, w, b)


def _route_plan(e, rank, blk_cnt, blk_start, counts, bm, tm):
    t = e.shape[0]
    nb = TOP_K * t // bm + N_EXPERTS
    counts = counts.reshape(N_EXPERTS).astype(jnp.int32)
    padded = (counts + bm - 1) // bm * bm
    pend = jnp.cumsum(padded)
    pstart = pend - padded
    onehot = e[..., None] == jnp.arange(N_EXPERTS, dtype=jnp.int32)
    dest = jnp.sum(jnp.where(onehot, pstart, 0), axis=-1) + rank
    blk_row0 = jnp.arange(nb, dtype=jnp.int32) * bm
    blk_exp = jnp.minimum(jnp.sum((pend[None, :] <= blk_row0[:, None]).astype(jnp.int32), axis=1),
                          N_EXPERTS - 1)
    n_used = pend[-1:] // bm
    run_len = blk_cnt.reshape(-1, N_EXPERTS).astype(jnp.int32)
    run_dst = pstart[None, :] + blk_start.reshape(-1, N_EXPERTS).astype(jnp.int32)
    return (dest.reshape(t // tm, tm * TOP_K), blk_exp, n_used, pstart + counts, pend, padded // bm,
            run_len.reshape(-1), run_dst.reshape(-1))


ROW_SUBLANES = D_MODEL // LANES


def _store_row_tiles(ref, base, x):
    for c in range(ROW_SUBLANES):
        ref[pl.ds(base + c, x.shape[0], stride=ROW_SUBLANES), :] = x[:, c * LANES:(c + 1) * LANES]


def _load_row_tiles(ref, base, n):
    return jnp.concatenate([ref[pl.ds(base + c, n, stride=ROW_SUBLANES), :] for c in range(ROW_SUBLANES)], axis=1)


def _tile_rows(ref, row):
    return ref.at[pl.ds(pl.multiple_of(row * ROW_SUBLANES, ROW_SUBLANES), ROW_SUBLANES), :]


def _dispatch_kernel(padlo_ref, pend_ref, nused_ref, rlen_ref, rdst_ref, lpos_hbm, x_ref, xs_hbm,
                     idx0, idx1, pk, cb, zbuf, sems):
    i = pl.program_id(0)
    n = pl.num_programs(0)
    tm = x_ref.shape[0]
    bm = zbuf.shape[0] // ROW_SUBLANES
    dump = xs_hbm.shape[0] // ROW_SUBLANES - bm
    idx = (idx0, idx1)
    slot_tiles = TOP_K * tm
    slot_rows = slot_tiles * ROW_SUBLANES

    def idx_copy(step, s):
        return pltpu.make_async_copy(lpos_hbm.at[step], idx[s], sems.at[s])

    def rows_wait(s):
        pltpu.make_async_copy(cb.at[pl.ds(s * slot_rows, slot_rows), :], xs_hbm.at[pl.ds(0, slot_rows), :],
                              sems.at[2 + s]).wait()

    def run_copy(s, src_tile, dst_tile, tiles, queue):
        rows = tiles * ROW_SUBLANES
        src = pl.multiple_of((s * slot_tiles + src_tile) * ROW_SUBLANES, ROW_SUBLANES)
        dst = pl.multiple_of(dst_tile * ROW_SUBLANES, ROW_SUBLANES)
        pltpu.make_async_copy(cb.at[pl.ds(src, rows), :], xs_hbm.at[pl.ds(dst, rows), :],
                              sems.at[2 + s]).start(priority=queue)

    @pl.when(i == 0)
    def _():
        idx_copy(0, 0).start()
        zbuf[...] = jnp.zeros_like(zbuf)

        def zero_block(b, c):
            cp = pltpu.make_async_copy(zbuf, xs_hbm.at[pl.ds(pl.multiple_of(b * zbuf.shape[0], ROW_SUBLANES),
                                                             zbuf.shape[0]), :], sems.at[4])
            cp.start()
            cp.wait()
            return c
        lax.fori_loop(nused_ref[0], dump // bm + 1, zero_block, 0)
        for e in range(N_EXPERTS):
            def zero_row(jj, c, e=e):
                for u in range(DMA_QUEUES):
                    j = jj * DMA_QUEUES + u
                    r = padlo_ref[e] + j
                    r = jnp.where(r < pend_ref[e], r, dump + j)
                    pltpu.make_async_copy(_tile_rows(zbuf, j), _tile_rows(xs_hbm, r), sems.at[4]).start(priority=u)
                return c
            lax.fori_loop(0, bm // DMA_QUEUES, zero_row, 0, unroll=DMA_UNROLL // DMA_QUEUES)
            pltpu.make_async_copy(zbuf, xs_hbm.at[pl.ds(0, zbuf.shape[0]), :], sems.at[4]).wait()

    def step(s):
        idx_copy(i, s).wait()

        @pl.when(i + 1 < n)
        def _():
            idx_copy(i + 1, 1 - s).start()

        @pl.when(i >= 2)
        def _():
            rows_wait(s)

        _store_row_tiles(pk, 0, x_ref[...])

        def place(j, c):
            tile = pk[pl.ds(pl.multiple_of(j * ROW_SUBLANES, ROW_SUBLANES), ROW_SUBLANES), :]
            for k in range(TOP_K):
                pos = s * slot_tiles + idx[s][j * TOP_K + k]
                cb[pl.ds(pl.multiple_of(pos * ROW_SUBLANES, ROW_SUBLANES), ROW_SUBLANES), :] = tile
            return c
        lax.fori_loop(0, tm, place, 0, unroll=DMA_UNROLL)

        off = jnp.int32(0)
        for e in range(N_EXPERTS):
            run = rlen_ref[i * N_EXPERTS + e]
            dst0 = rdst_ref[i * N_EXPERTS + e]
            whole = run // RUN_CHUNK

            def chunk(c, carry, off=off, dst0=dst0, e=e):
                run_copy(s, off + c * RUN_CHUNK, dst0 + c * RUN_CHUNK, RUN_CHUNK, e % DMA_QUEUES)
                return carry
            lax.fori_loop(0, whole, chunk, 0)
            bit = RUN_CHUNK // 2
            while bit >= 1:
                lo = whole * RUN_CHUNK + (run & (RUN_CHUNK - 1) & ~(2 * bit - 1))
                pl.when((run & bit) != 0)(functools.partial(run_copy, s, off + lo, dst0 + lo, bit, e % DMA_QUEUES))
                bit //= 2
            off = off + run

        @pl.when(i == n - 1)
        def _():
            rows_wait(s)

        @pl.when(jnp.logical_and(i == n - 1, n >= 2))
        def _():
            rows_wait(1 - s)

    for s in range(2):
        pl.when(i % 2 == s)(functools.partial(step, s))


def _dispatch(x, lpos, run_len, run_dst, pad_lo, pend, n_used, bm):
    t, d = x.shape
    nblk, per = lpos.shape
    tm = per // TOP_K
    nb = TOP_K * t // bm + N_EXPERTS
    grid_spec = pltpu.PrefetchScalarGridSpec(
        num_scalar_prefetch=5,
        grid=(nblk,),
        in_specs=[pl.BlockSpec(memory_space=pl.ANY), pl.BlockSpec((tm, d), lambda i, *_: (i, 0))],
        out_specs=pl.BlockSpec(memory_space=pl.ANY),
        scratch_shapes=[pltpu.SMEM((per,), jnp.int32), pltpu.SMEM((per,), jnp.int32),
                        pltpu.VMEM((tm * ROW_SUBLANES, LANES), F32),
                        pltpu.VMEM((2 * per * ROW_SUBLANES, LANES), F32),
                        pltpu.VMEM((bm * ROW_SUBLANES, LANES), F32),
                        pltpu.SemaphoreType.DMA((5,))])
    return pl.pallas_call(
        _dispatch_kernel,
        grid_spec=grid_spec,
        out_shape=jax.ShapeDtypeStruct(((nb * bm + bm) * ROW_SUBLANES, LANES), F32),
        compiler_params=_cparams(("arbitrary",), 44 << 20),
        name="moe_dispatch",
    )(pad_lo, pend, n_used, run_len, run_dst, lpos, x)


def _expert_kernel(be_ref, nused_ref, gblk_ref, xs_ref, w1_hbm, b1g_ref, b1l_ref, w2_hbm, b2_ref, y_ref,
                   w1gt, w1lt, w2b, tbuf, w1s, w2s, slot_ref, sems, *, layer):
    i = pl.program_id(0)
    d, de = w2b.shape[1], w2b.shape[0]
    nt = (((1,), (1,)), ((), ()))

    def weight_copies(e, s):
        return (pltpu.make_async_copy(w1_hbm.at[layer, e], w1s.at[s], sems.at[0, s]),
                pltpu.make_async_copy(w2_hbm.at[layer, e], w2s.at[s], sems.at[1, s]))

    @pl.when(i < nused_ref[0])
    def _():
        e = be_ref[i]
        changed = jnp.logical_or(i == 0, e != be_ref[jnp.maximum(i - 1, 0)])

        @pl.when(changed)
        def _():
            @pl.when(i == 0)
            def _():
                slot_ref[0] = 0
                for cp in weight_copies(e, 0):
                    cp.start()

            s = slot_ref[0]
            for cp in weight_copies(e, s):
                cp.wait()
            j = i + gblk_ref[e]

            @pl.when(j < nused_ref[0])
            def _():
                for cp in weight_copies(be_ref[jnp.minimum(j, be_ref.shape[0] - 1)], 1 - s):
                    cp.start()

            for a in range(d // LANES):
                cols = slice(a * LANES, (a + 1) * LANES)
                tbuf[...] = w1s[s, cols, :].T
                w1gt[:, cols] = tbuf[pl.ds(0, de, stride=2), :].astype(BF16)
                w1lt[:, cols] = tbuf[pl.ds(1, de, stride=2), :].astype(BF16)
            w2b[...] = w2s[s].astype(BF16)
            slot_ref[0] = 1 - s

        bm = xs_ref.shape[0] // ROW_SUBLANES
        x = _load_row_tiles(xs_ref, 0, bm).astype(BF16)
        hg = lax.dot_general(x, w1gt[...], nt, preferred_element_type=F32) + b1g_ref[...]
        hl = lax.dot_general(x, w1lt[...], nt, preferred_element_type=F32) + b1l_ref[...]
        hg = jnp.minimum(hg, SWIGLU_LIMIT)
        hl = jnp.clip(hl, -SWIGLU_LIMIT, SWIGLU_LIMIT)
        act = hg * (1.0 / (1.0 + jnp.exp(-SWIGLU_ALPHA * hg))) * (hl + 1.0)
        y = jnp.dot(act.astype(BF16), w2b[...], preferred_element_type=F32) + b2_ref[...]
        _store_row_tiles(y_ref, 0, y)

    @pl.when(i >= nused_ref[0])
    def _():
        y_ref[...] = jnp.zeros_like(y_ref)


def _experts(xs, blk_exp, n_used, grp_blocks, layer, w1, b1g, b1l, w2, b2, bm):
    d = ROW_SUBLANES * LANES
    nb = blk_exp.shape[0]
    de = w2.shape[2]
    blk = (bm * ROW_SUBLANES, LANES)
    bmap = lambda i, be, nu, gb: (layer, be[i], 0, 0)
    any_spec = pl.BlockSpec(memory_space=pl.ANY)
    grid_spec = pltpu.PrefetchScalarGridSpec(
        num_scalar_prefetch=3,
        grid=(nb,),
        in_specs=[pl.BlockSpec(blk, lambda i, be, nu, gb: (jnp.minimum(i, nu[0] - 1), 0)),
                  any_spec,
                  pl.BlockSpec((None, None, 1, de), bmap), pl.BlockSpec((None, None, 1, de), bmap),
                  any_spec,
                  pl.BlockSpec((None, None, 1, d), bmap)],
        out_specs=pl.BlockSpec(blk, lambda i, be, nu, gb: (i, 0)),
        scratch_shapes=[pltpu.VMEM((de, d), BF16), pltpu.VMEM((de, d), BF16), pltpu.VMEM((de, d), BF16),
                        pltpu.VMEM((2 * de, LANES), F32),
                        pltpu.VMEM((2, d, 2 * de), F32), pltpu.VMEM((2, de, d), F32),
                        pltpu.SMEM((1,), jnp.int32), pltpu.SemaphoreType.DMA((2, 2))])
    return pl.pallas_call(
        functools.partial(_expert_kernel, layer=layer),
        grid_spec=grid_spec,
        out_shape=jax.ShapeDtypeStruct((nb * blk[0], LANES), F32),
        compiler_params=_cparams(("arbitrary",), 54 << 20),
        name="moe_experts",
    )(blk_exp, n_used, grp_blocks, xs, w1, b1g, b1l, w2, b2)


def _combine_kernel(dest_hbm, gate_ref, x_ref, g_ref, b_ref, y_hbm, o_ref, idx0, idx1, ybuf, sems, *, blk_off):
    i = pl.program_id(0)
    n = pl.num_programs(0)
    tm = x_ref.shape[0]
    idx = (idx0, idx1)
    slab_rows = tm * ROW_SUBLANES

    def slab(s, k):
        return (s * TOP_K + k) * tm

    def idx_copy(step, s):
        return pltpu.make_async_copy(dest_hbm.at[blk_off + step], idx[s], sems.at[s])

    def issue_rows(s):
        def body(j, c):
            for k in range(TOP_K):
                pltpu.make_async_copy(_tile_rows(y_hbm, idx[s][j * TOP_K + k]), _tile_rows(ybuf, slab(s, k) + j),
                                      sems.at[2 + s]).start(priority=k % DMA_QUEUES)
            return c
        lax.fori_loop(0, tm, body, 0, unroll=DMA_UNROLL)

    @pl.when(i == 0)
    def _():
        cp = idx_copy(0, 0)
        cp.start()
        cp.wait()
        issue_rows(0)

    @pl.when(jnp.logical_and(i == 0, n >= 2))
    def _():
        idx_copy(1, 1).start()

    def step(s):
        @pl.when(i + 1 < n)
        def _():
            idx_copy(i + 1, 1 - s).wait()
            issue_rows(1 - s)

        @pl.when(i + 2 < n)
        def _():
            idx_copy(i + 2, s).start()

        for k in range(TOP_K):
            pltpu.make_async_copy(y_hbm.at[pl.ds(0, slab_rows), :],
                                  ybuf.at[pl.ds(slab(s, k) * ROW_SUBLANES, slab_rows), :], sems.at[2 + s]).wait()
        gate = gate_ref[...]
        h = None
        for k in range(TOP_K):
            term = gate[:, k:k + 1] * _load_row_tiles(ybuf, slab(s, k) * ROW_SUBLANES, tm)
            h = term if h is None else h + term
        o_ref[...] = _layer_norm(DEEPNORM_ALPHA * x_ref[...] + h, g_ref[...], b_ref[...])

    for s in range(2):
        pl.when(i % 2 == s)(functools.partial(step, s))


def _combine(y, dest, gate, x, ln_g, ln_b, row_lo, rows):
    _, d = x.shape
    per = dest.shape[1]
    tm = per // TOP_K
    off = row_lo // tm
    vec = pl.BlockSpec((1, d), lambda i: (0, 0))
    return pl.pallas_call(
        functools.partial(_combine_kernel, blk_off=off),
        grid=(rows // tm,),
        in_specs=[pl.BlockSpec(memory_space=pl.ANY),
                  pl.BlockSpec((tm, TOP_K), lambda i: (off + i, 0)),
                  pl.BlockSpec((tm, d), lambda i: (off + i, 0)), vec, vec,
                  pl.BlockSpec(memory_space=pl.ANY)],
        out_specs=pl.BlockSpec((tm, d), lambda i: (i, 0)),
        out_shape=jax.ShapeDtypeStruct((rows, d), F32),
        scratch_shapes=[pltpu.SMEM((per,), jnp.int32), pltpu.SMEM((per,), jnp.int32),
                        pltpu.VMEM((2 * TOP_K * tm * ROW_SUBLANES, LANES), F32),
                        pltpu.SemaphoreType.DMA((4,))],
        compiler_params=_cparams(("arbitrary",), 40 << 20),
        name="moe_combine_ln",
    )(dest, gate, x, ln_g, ln_b, y)


def _moe_block(x, layer, router_w, router_b, w1, b1g, b1l, w2, b2):
    e, gate, rank, lpos, blk_cnt, blk_start, counts = _router(x, router_w, router_b)
    dest, blk_exp, n_used, pad_lo, pend, grp_blocks, run_len, run_dst = _route_plan(
        e, rank, blk_cnt, blk_start, counts, EXPERT_ROWS, ROW_TILE)
    xs = _dispatch(x, lpos.reshape(dest.shape), run_len, run_dst, pad_lo, pend, n_used, EXPERT_ROWS)
    y = _experts(xs, blk_exp, n_used, grp_blocks, layer, w1, b1g, b1l, w2, b2, EXPERT_ROWS)
    return y, dest, gate


def _gelu(z):
    return 0.5 * z * (1.0 + jnp.tanh(0.7978845608028654 * (z + 0.044715 * (z * z * z))))


def _sgu_kernel(x_ref, win_ref, bin_ref, ng_ref, nb_ref, ws_ref, bs_ref, wout_ref, bout_ref,
                g_ref, b_ref, y_ref):
    tm = x_ref.shape[0]
    x = x_ref[...]
    xb = x.astype(BF16)
    dot = functools.partial(jnp.dot, preferred_element_type=F32)
    v = _gelu(dot(xb, win_ref[:, SGU_HALF:]) + bin_ref[:, SGU_HALF:])
    v = _layer_norm(v, ng_ref[...], nb_ref[...]).astype(BF16)
    acc = jnp.zeros((tm, D_MODEL), F32)
    for g in range(SGU_GROUPS):
        cols = slice(g * SGU_GROUP_DIM, (g + 1) * SGU_GROUP_DIM)
        u = _gelu(dot(xb, win_ref[:, cols]) + bin_ref[:, cols])
        mixed = [dot(ws_ref[g], v[c * SGU_CHUNK:(c + 1) * SGU_CHUNK, cols]) + bs_ref[:, g:g + 1]
                 for c in range(tm // SGU_CHUNK)]
        mixed = jnp.concatenate(mixed, axis=0)
        acc = acc + dot((u * mixed).astype(BF16), wout_ref[cols, :])
    h = acc + bout_ref[...]
    y_ref[...] = _layer_norm(DEEPNORM_ALPHA * x + h, g_ref[...], b_ref[...])


def _sgu_block(x, w_in, b_in, norm_g, norm_b, w_s, b_s_t, w_out, b_out, ln_g, ln_b):
    t, d = x.shape
    tm = SGU_ROWS
    row = lambda i: (i, 0)
    c2 = lambda i: (0, 0)
    c3 = lambda i: (0, 0, 0)
    once = dict(pipeline_mode=pl.Buffered(1))
    vec = pl.BlockSpec((1, d), c2)
    return pl.pallas_call(
        _sgu_kernel,
        grid=(t // tm,),
        in_specs=[pl.BlockSpec((tm, d), row),
                  pl.BlockSpec((d, 2 * SGU_HALF), c2, **once), pl.BlockSpec((1, 2 * SGU_HALF), c2),
                  pl.BlockSpec((1, SGU_HALF), c2), pl.BlockSpec((1, SGU_HALF), c2),
                  pl.BlockSpec((SGU_GROUPS, SGU_CHUNK, SGU_CHUNK), c3),
                  pl.BlockSpec((SGU_CHUNK, SGU_GROUPS), c2),
                  pl.BlockSpec((SGU_HALF, d), c2, **once), vec, vec, vec],
        out_specs=pl.BlockSpec((tm, d), row),
        out_shape=jax.ShapeDtypeStruct((t, d), F32),
        compiler_params=_cparams(("parallel",), 52 << 20),
        name="sgu_block",
    )(x, w_in, b_in, norm_g, norm_b, w_s, b_s_t, w_out, b_out, ln_g, ln_b)


def _attention_block(x, w_qkv, w_o, b_o, ln_g, ln_b, seq_rows):
    n_groups = N_GROUPS * HEADS
    slopes = (2.0 ** (-8.0 * jnp.arange(1, n_groups + 1, dtype=F32) / n_groups)).reshape(N_GROUPS, HEADS)
    scale = jnp.concatenate([jnp.full((GROUP_DIM,), HEAD_DIM ** -0.5, F32), jnp.ones((2 * GROUP_DIM,), F32)])
    os_, lses = [], []
    for g, (_, dil) in enumerate(DIL_CONFIGS):
        cols = [w_qkv[:, (j * N_GROUPS + g) * GROUP_DIM:(j * N_GROUPS + g + 1) * GROUP_DIM] for j in range(3)]
        w_g = (jnp.concatenate(cols, axis=1) * scale).astype(BF16)
        qkv = _qkv_project(x, w_g, dil)
        segs = tuple((n // dil, length // dil) for n, length in seq_rows)
        o, lse = _attention(qkv, _band_bias(slopes[g], dil), dil, segs)
        t = x.shape[0]
        os_.append(o.reshape(t, GROUP_DIM))
        lses.append(lse.reshape(t, LANES))
    return _merge_project(os_, lses, x, w_o.astype(BF16), b_o[None], ln_g[None], ln_b[None])


def kernel(x_prompt, x_sample, attn_w_qkv, attn_w_o, attn_b_o, sgu_w_in, sgu_b_in, sgu_norm_g, sgu_norm_b,
           sgu_w_s, sgu_b_s, sgu_w_out, sgu_b_out, router_w, router_b, exp_w1, exp_b1, exp_w2, exp_b2,
           ln_mix_g, ln_mix_b, ln_ffn_g, ln_ffn_b):
    d = x_prompt.shape[-1]
    tp = x_prompt.shape[0] * x_prompt.shape[1]
    ts = x_sample.shape[0] * x_sample.shape[1]
    seq_rows = ((tp, x_prompt.shape[1]), (ts, x_sample.shape[1]))
    x = jnp.concatenate([x_prompt.reshape(tp, d), x_sample.reshape(ts, d)], axis=0)

    b1g = exp_b1[:, :, None, 0::2]
    b1l = exp_b1[:, :, None, 1::2]
    b2 = exp_b2[:, :, None, :]

    outs = None
    for i in range(DEPTH):
        j = i // 2
        if i % 2 == 0:
            x = _attention_block(x, attn_w_qkv[j], attn_w_o[j], attn_b_o[j], ln_mix_g[i], ln_mix_b[i], seq_rows)
        else:
            x = _sgu_block(x, sgu_w_in[j].astype(BF16), sgu_b_in[j][None], sgu_norm_g[j][None], sgu_norm_b[j][None],
                           sgu_w_s[j].astype(BF16), jnp.transpose(sgu_b_s[j]), sgu_w_out[j].astype(BF16),
                           sgu_b_out[j][None], ln_mix_g[i][None], ln_mix_b[i][None])
        y, dest, gate = _moe_block(x, i, router_w[i], router_b[i][None], exp_w1, b1g, b1l, exp_w2, b2)
        g_ln, b_ln = ln_ffn_g[i][None], ln_ffn_b[i][None]
        if i + 1 < DEPTH:
            x = _combine(y, dest, gate, x, g_ln, b_ln, 0, tp + ts)
        else:
            outs = (_combine(y, dest, gate, x, g_ln, b_ln, 0, tp).reshape(x_prompt.shape),
                    _combine(y, dest, gate, x, g_ln, b_ln, tp, ts).reshape(x_sample.shape))
    return outs
```

```python
import functools

import jax
import jax.numpy as jnp
from jax import lax
from jax.experimental import pallas as pl
from jax.experimental.pallas import tpu as pltpu

F32 = jnp.float32
BF16 = jnp.bfloat16

LANES = 128
SUBLANES = 8
VMEM_LIMIT_CAP = 56 * 1024 * 1024

D_MODEL = 1024
DIL_CONFIGS = ((128, 1), (512, 4), (2048, 16))
N_GROUPS = len(DIL_CONFIGS)
HEADS = 8
HEAD_DIM = 64
GROUP_DIM = HEADS * HEAD_DIM
N_SIDE = 64
Q_BLOCK = 128
KV_BLOCK = N_SIDE
KV_WIN = Q_BLOCK + 2 * N_SIDE
ATTN_Q_STEPS = (8, 4, 2, 1)
SGU_CHUNK = 128
SGU_HALF = 3 * D_MODEL
SGU_GROUPS = 8
SGU_GROUP_DIM = SGU_HALF // SGU_GROUPS
N_EXPERTS = 32
TOP_K = 4
D_EXPERT = D_MODEL
SWIGLU_ALPHA = 1.702
SWIGLU_LIMIT = 7.0
LN_EPS = 1e-5
DEPTH = 2
DEEPNORM_ALPHA = (2 * DEPTH) ** 0.25
NEG = -1e30

ROW_TILE = 512
QKV_ROWS = 1024
RANK_CHUNK = 256
SGU_ROWS = 512
EXPERT_ROWS = 512
DMA_UNROLL = 8
DMA_QUEUES = 2


def _cparams(semantics, vmem_bytes):
    return pltpu.CompilerParams(dimension_semantics=semantics,
                                vmem_limit_bytes=min(int(vmem_bytes), VMEM_LIMIT_CAP))


def _layer_norm(y, g, b):
    mu = jnp.mean(y, axis=-1, keepdims=True)
    yc = y - mu
    var = jnp.mean(yc * yc, axis=-1, keepdims=True)
    return yc * lax.rsqrt(var + LN_EPS) * g + b


def _qkv_kernel(x_ref, w_ref, o_ref):
    o_ref[...] = jnp.dot(x_ref[...].astype(BF16), w_ref[...],
                         preferred_element_type=F32).astype(BF16)


def _qkv_project(x, w, dil):
    t, d = x.shape
    n = t // dil
    tm = min(QKV_ROWS, n)
    width = w.shape[1]
    xv = x.reshape(n, dil * d)
    return pl.pallas_call(
        _qkv_kernel,
        grid=(dil, n // tm),
        in_specs=[pl.BlockSpec((tm, d), lambda r, i: (i, r)),
                  pl.BlockSpec((d, width), lambda r, i: (0, 0))],
        out_specs=pl.BlockSpec((None, tm, width), lambda r, i: (r, i, 0)),
        out_shape=jax.ShapeDtypeStruct((dil, n, width), BF16),
        compiler_params=_cparams(("parallel", "parallel"),
                                 2 * (tm * d * 4 + d * width * 2 + tm * width * 2) + tm * width * 4
                                 + (8 << 20)),
        name=f"qkv_project_d{dil}",
    )(xv, w)


def _attn_kernel(q_ref, kp_ref, kb_ref, kn_ref, vp_ref, vb_ref, vn_ref, bias_ref, o_ref, lse_ref, *, segs):
    (n_a, len_a), (_, len_b) = segs
    step_rows = q_ref.shape[0]
    lane = lax.broadcasted_iota(jnp.int32, (1, LANES), 1)
    first_head = lane < HEAD_DIM
    nt = (((1,), (1,)), ((), ()))

    def window(prev, body, nxt, first, sl):
        parts = []
        if first < 0:
            parts.append(prev[:, sl])
        parts.append(body[max(first, 0):min(first + KV_WIN, step_rows), sl])
        if first + KV_WIN > step_rows:
            parts.append(nxt[:, sl])
        return jnp.concatenate(parts, axis=0)

    for sub in range(step_rows // Q_BLOCK):
        rows = slice(sub * Q_BLOCK, (sub + 1) * Q_BLOCK)
        row0 = pl.program_id(1) * step_rows + sub * Q_BLOCK
        in_a = row0 < n_a
        lo = jnp.where(in_a, (row0 // len_a) * len_a, n_a + ((row0 - n_a) // len_b) * len_b)
        hi = lo + jnp.where(in_a, len_a, len_b)
        krow = lax.broadcasted_iota(jnp.int32, (1, KV_WIN), 1) + (row0 - N_SIDE)
        col_bias = jnp.where((krow >= lo) & (krow < hi), 0.0, NEG).astype(F32)
        lse_tile = jnp.zeros((Q_BLOCK, LANES), F32)
        for p in range(HEADS // 2):
            sl = slice(p * LANES, (p + 1) * LANES)
            q = q_ref[rows, sl]
            k = window(kp_ref, kb_ref, kn_ref, sub * Q_BLOCK - N_SIDE, sl)
            v = window(vp_ref, vb_ref, vn_ref, sub * Q_BLOCK - N_SIDE, sl)
            zero = jnp.zeros_like(q)
            outs, inv_ls = [], []
            for j, qh in enumerate((jnp.where(first_head, q, zero), jnp.where(first_head, zero, q))):
                h = 2 * p + j
                s = lax.dot_general(qh, k, nt, preferred_element_type=F32)
                s = s + bias_ref[h] + col_bias
                m = jnp.max(s, axis=-1, keepdims=True)
                e = jnp.exp(s - m)
                l = jnp.sum(e, axis=-1, keepdims=True)
                outs.append(jnp.dot(e.astype(BF16), v, preferred_element_type=F32))
                inv_ls.append(1.0 / l)
                lse_tile = jnp.where(lane == h, m + jnp.log(l), lse_tile)
            o = jnp.where(first_head, outs[0] * inv_ls[0], outs[1] * inv_ls[1])
            o_ref[rows, sl] = o.astype(BF16)
        lse_ref[rows, :] = lse_tile


def _band_bias(slopes, dil):
    qi = jnp.arange(Q_BLOCK)[:, None]
    kc = jnp.arange(KV_WIN)[None, :]
    dist = jnp.abs(kc - N_SIDE - qi)
    pen = -slopes[:, None, None] * (dist * dil).astype(F32)[None]
    return jnp.where((dist <= N_SIDE)[None], pen, NEG).astype(F32)


def _attention(qkv, bias, dil, segs):
    _, n, _ = qkv.shape
    nkb = n // KV_BLOCK
    q_step = max(s for s in ATTN_Q_STEPS if (n // Q_BLOCK) % s == 0)
    step_rows = q_step * Q_BLOCK
    per_step = step_rows // KV_BLOCK

    def body_spec(col):
        return pl.BlockSpec((None, step_rows, GROUP_DIM), lambda r, i: (r, i, col))

    def edge_spec(col, after):
        def imap(r, i):
            return (r, jnp.clip((i + after) * per_step - 1 + after, 0, nkb - 1), col)
        return pl.BlockSpec((None, KV_BLOCK, GROUP_DIM), imap)

    in_specs = ([body_spec(0)]
                + [edge_spec(1, 0), body_spec(1), edge_spec(1, 1)]
                + [edge_spec(2, 0), body_spec(2), edge_spec(2, 1)]
                + [pl.BlockSpec((HEADS, Q_BLOCK, KV_WIN), lambda r, i: (0, 0, 0))])
    return pl.pallas_call(
        functools.partial(_attn_kernel, segs=segs),
        grid=(dil, n // step_rows),
        in_specs=in_specs,
        out_specs=[pl.BlockSpec((step_rows, GROUP_DIM), lambda r, i: (i, r)),
                   pl.BlockSpec((step_rows, LANES), lambda r, i: (i, r))],
        out_shape=[jax.ShapeDtypeStruct((n, dil * GROUP_DIM), BF16),
                   jax.ShapeDtypeStruct((n, dil * LANES), F32)],
        compiler_params=_cparams(("parallel", "parallel"), 32 << 20),
        name=f"band_attention_d{dil}",
    )(*([qkv] * 7), bias)


def _merge_kernel(o0, o1, o2, l0, l1, l2, x_ref, wo_ref, bo_ref, g_ref, b_ref, y_ref):
    lses = [l0[...], l1[...], l2[...]]
    m = jnp.maximum(jnp.maximum(lses[0], lses[1]), lses[2])
    es = [jnp.exp(l - m) for l in lses]
    inv = 1.0 / (es[0] + es[1] + es[2])
    ws = [e * inv for e in es]
    lane = lax.broadcasted_iota(jnp.int32, (1, LANES), 1)
    first_head = lane < HEAD_DIM
    parts = []
    for p in range(HEADS // 2):
        sl = slice(p * LANES, (p + 1) * LANES)
        acc = None
        for w, o in zip(ws, (o0, o1, o2)):
            wp = jnp.where(first_head, w[:, 2 * p:2 * p + 1], w[:, 2 * p + 1:2 * p + 2])
            term = wp * o[:, sl].astype(F32)
            acc = term if acc is None else acc + term
        parts.append(acc)
    merged = jnp.concatenate(parts, axis=-1).astype(BF16)
    h = jnp.dot(merged, wo_ref[...], preferred_element_type=F32) + bo_ref[...]
    y_ref[...] = _layer_norm(DEEPNORM_ALPHA * x_ref[...] + h, g_ref[...], b_ref[...])


def _merge_project(os_, lses, x, w_o, b_o, ln_g, ln_b):
    t, d = x.shape
    tm = ROW_TILE
    row = lambda i: (i, 0)
    const = lambda i: (0, 0)
    vec = pl.BlockSpec((1, d), const)
    return pl.pallas_call(
        _merge_kernel,
        grid=(t // tm,),
        in_specs=([pl.BlockSpec((tm, GROUP_DIM), row)] * 3 + [pl.BlockSpec((tm, LANES), row)] * 3
                  + [pl.BlockSpec((tm, d), row), pl.BlockSpec((GROUP_DIM, d), const), vec, vec, vec]),
        out_specs=pl.BlockSpec((tm, d), row),
        out_shape=jax.ShapeDtypeStruct((t, d), F32),
        compiler_params=_cparams(("parallel",), 32 << 20),
        name="merge_outproj_ln",
    )(*os_, *lses, x, w_o, b_o, ln_g, ln_b)


def _split_bf16(a):
    hi = a.astype(BF16)
    return hi, (a - hi.astype(F32)).astype(BF16)


def _router_kernel(x_ref, w_ref, b_ref, e_ref, g_ref, r_ref, cnt_ref, carry_ref):
    tm = x_ref.shape[0]

    @pl.when(pl.program_id(0) == 0)
    def _():
        carry_ref[...] = jnp.zeros_like(carry_ref)

    xh, xl = _split_bf16(x_ref[...])
    wh, wl = _split_bf16(w_ref[...])
    dot = functools.partial(jnp.dot, preferred_element_type=F32)
    logits = dot(xh, wh) + (dot(xl, wh) + dot(xh, wl)) + b_ref[...]
    lane = lax.broadcasted_iota(jnp.int32, logits.shape, 1).astype(F32)
    col4 = lax.broadcasted_iota(jnp.int32, (tm, TOP_K), 1)
    work = logits
    picks, vals, idxs = [], [], []
    for _k in range(TOP_K):
        mx = jnp.max(work, axis=-1, keepdims=True)
        idx = jnp.min(jnp.where(work == mx, lane, float(N_EXPERTS)), axis=-1, keepdims=True)
        pick = lane == idx
        work = jnp.where(pick, -jnp.inf, work)
        picks.append(pick)
        vals.append(mx)
        idxs.append(idx)
    ex = [jnp.exp(v - vals[0]) for v in vals]
    inv = 1.0 / (ex[0] + ex[1] + ex[2] + ex[3])
    chosen = jnp.zeros(logits.shape, F32)
    for pick in picks:
        chosen = chosen + pick.astype(F32)
    ch = min(RANK_CHUNK, tm)
    r_i = lax.broadcasted_iota(jnp.int32, (ch, ch), 0)
    c_i = lax.broadcasted_iota(jnp.int32, (ch, ch), 1)
    tri = (c_i < r_i).astype(BF16)
    carry = carry_ref[...]
    before = []
    for c in range(tm // ch):
        part = chosen[c * ch:(c + 1) * ch]
        before.append(dot(tri, part.astype(BF16)) + carry)
        carry = carry + jnp.sum(part, axis=0, keepdims=True)
    before = jnp.concatenate(before, axis=0)
    e_out = jnp.zeros((tm, TOP_K), jnp.int32)
    g_out = jnp.zeros((tm, TOP_K), F32)
    r_out = jnp.zeros((tm, TOP_K), jnp.int32)
    for k in range(TOP_K):
        rank = jnp.sum(jnp.where(picks[k], before, 0.0), axis=-1, keepdims=True)
        e_out = jnp.where(col4 == k, idxs[k].astype(jnp.int32), e_out)
        g_out = jnp.where(col4 == k, ex[k] * inv, g_out)
        r_out = jnp.where(col4 == k, rank.astype(jnp.int32), r_out)
    e_ref[...] = e_out
    g_ref[...] = g_out
    r_ref[...] = r_out
    carry_ref[...] = carry
    cnt_ref[...] = carry


def _router(x, w, b):
    t, d = x.shape
    tm = ROW_TILE
    row = lambda i: (i, 0)
    const = lambda i: (0, 0)
    k_spec = pl.BlockSpec((tm, TOP_K), row)
    return pl.pallas_call(
        _router_kernel,
        grid=(t // tm,),
        in_specs=[pl.BlockSpec((tm, d), row), pl.BlockSpec((d, N_EXPERTS), const),
                  pl.BlockSpec((1, N_EXPERTS), const)],
        out_specs=[k_spec, k_spec, k_spec, pl.BlockSpec((1, N_EXPERTS), const)],
        out_shape=[jax.ShapeDtypeStruct((t, TOP_K), jnp.int32), jax.ShapeDtypeStruct((t, TOP_K), F32),
                   jax.ShapeDtypeStruct((t, TOP_K), jnp.int32), jax.ShapeDtypeStruct((1, N_EXPERTS), F32)],
        scratch_shapes=[pltpu.VMEM((1, N_EXPERTS), F32)],
        compiler_params=_cparams(("arbitrary",), 32 << 20),
        name="router_topk",
    )(x, w, b)


def _route_plan(e, rank, counts, bm, tm):
    t = e.shape[0]
    nb = TOP_K * t // bm + N_EXPERTS
    counts = counts.reshape(N_EXPERTS).astype(jnp.int32)
    padded = (counts + bm - 1) // bm * bm
    pend = jnp.cumsum(padded)
    pstart = pend - padded
    onehot = e[..., None] == jnp.arange(N_EXPERTS, dtype=jnp.int32)
    dest = jnp.sum(jnp.where(onehot, pstart, 0), axis=-1) + rank
    blk_row0 = jnp.arange(nb, dtype=jnp.int32) * bm
    blk_exp = jnp.minimum(jnp.sum((pend[None, :] <= blk_row0[:, None]).astype(jnp.int32), axis=1),
                          N_EXPERTS - 1)
    n_used = pend[-1:] // bm
    return dest.reshape(t // tm, tm * TOP_K), blk_exp, n_used, pstart + counts, pend, padded // bm


ROW_SUBLANES = D_MODEL // LANES


def _store_row_tiles(ref, base, x):
    for c in range(ROW_SUBLANES):
        ref[pl.ds(base + c, x.shape[0], stride=ROW_SUBLANES), :] = x[:, c * LANES:(c + 1) * LANES]


def _load_row_tiles(ref, base, n):
    return jnp.concatenate([ref[pl.ds(base + c, n, stride=ROW_SUBLANES), :] for c in range(ROW_SUBLANES)], axis=1)


def _tile_rows(ref, row):
    return ref.at[pl.ds(pl.multiple_of(row * ROW_SUBLANES, ROW_SUBLANES), ROW_SUBLANES), :]


def _dispatch_kernel(padlo_ref, pend_ref, nused_ref, dest_hbm, x_ref, xs_hbm, idx0, idx1, pk, zbuf, sems):
    i = pl.program_id(0)
    n = pl.num_programs(0)
    tm = x_ref.shape[0]
    bm = zbuf.shape[0] // ROW_SUBLANES
    dump = xs_hbm.shape[0] // ROW_SUBLANES - bm
    idx = (idx0, idx1)
    slot_rows = tm * ROW_SUBLANES

    def idx_copy(step, s):
        return pltpu.make_async_copy(dest_hbm.at[step], idx[s], sems.at[s])

    def rows_wait(s):
        for _ in range(TOP_K):
            pltpu.make_async_copy(pk.at[pl.ds(s * slot_rows, slot_rows), :], xs_hbm.at[pl.ds(0, slot_rows), :],
                                  sems.at[2 + s]).wait()

    @pl.when(i == 0)
    def _():
        idx_copy(0, 0).start()
        zbuf[...] = jnp.zeros_like(zbuf)

        def zero_block(b, c):
            cp = pltpu.make_async_copy(zbuf, xs_hbm.at[pl.ds(pl.multiple_of(b * zbuf.shape[0], ROW_SUBLANES),
                                                             zbuf.shape[0]), :], sems.at[4])
            cp.start()
            cp.wait()
            return c
        lax.fori_loop(nused_ref[0], dump // bm + 1, zero_block, 0)
        for e in range(N_EXPERTS):
            def zero_row(jj, c, e=e):
                for u in range(DMA_QUEUES):
                    j = jj * DMA_QUEUES + u
                    r = padlo_ref[e] + j
                    r = jnp.where(r < pend_ref[e], r, dump + j)
                    pltpu.make_async_copy(_tile_rows(zbuf, j), _tile_rows(xs_hbm, r), sems.at[4]).start(priority=u)
                return c
            lax.fori_loop(0, bm // DMA_QUEUES, zero_row, 0, unroll=DMA_UNROLL // DMA_QUEUES)
            pltpu.make_async_copy(zbuf, xs_hbm.at[pl.ds(0, zbuf.shape[0]), :], sems.at[4]).wait()

    def step(s):
        idx_copy(i, s).wait()

        @pl.when(i + 1 < n)
        def _():
            idx_copy(i + 1, 1 - s).start()

        @pl.when(i >= 2)
        def _():
            rows_wait(s)

        _store_row_tiles(pk, s * slot_rows, x_ref[...])

        def issue(j, c):
            for k in range(TOP_K):
                pltpu.make_async_copy(_tile_rows(pk, s * tm + j), _tile_rows(xs_hbm, idx[s][j * TOP_K + k]),
                                      sems.at[2 + s]).start(priority=k % DMA_QUEUES)
            return c
        lax.fori_loop(0, tm, issue, 0, unroll=DMA_UNROLL)

        @pl.when(i == n - 1)
        def _():
            rows_wait(s)

        @pl.when(jnp.logical_and(i == n - 1, n >= 2))
        def _():
            rows_wait(1 - s)

    for s in range(2):
        pl.when(i % 2 == s)(functools.partial(step, s))


def _dispatch(x, dest, pad_lo, pend, n_used, bm):
    t, d = x.shape
    nblk, per = dest.shape
    tm = per // TOP_K
    nb = TOP_K * t // bm + N_EXPERTS
    grid_spec = pltpu.PrefetchScalarGridSpec(
        num_scalar_prefetch=3,
        grid=(nblk,),
        in_specs=[pl.BlockSpec(memory_space=pl.ANY), pl.BlockSpec((tm, d), lambda i, a, b, c: (i, 0))],
        out_specs=pl.BlockSpec(memory_space=pl.ANY),
        scratch_shapes=[pltpu.SMEM((per,), jnp.int32), pltpu.SMEM((per,), jnp.int32),
                        pltpu.VMEM((2 * tm * ROW_SUBLANES, LANES), F32),
                        pltpu.VMEM((bm * ROW_SUBLANES, LANES), F32),
                        pltpu.SemaphoreType.DMA((5,))])
    return pl.pallas_call(
        _dispatch_kernel,
        grid_spec=grid_spec,
        out_shape=jax.ShapeDtypeStruct(((nb * bm + bm) * ROW_SUBLANES, LANES), F32),
        compiler_params=_cparams(("arbitrary",), 32 << 20),
        name="moe_dispatch",
    )(pad_lo, pend, n_used, dest, x)


def _expert_kernel(be_ref, nused_ref, gblk_ref, xs_ref, w1_hbm, b1g_ref, b1l_ref, w2_hbm, b2_ref, y_ref,
                   w1gt, w1lt, w2b, tbuf, w1s, w2s, slot_ref, sems, *, layer):
    i = pl.program_id(0)
    d, de = w2b.shape[1], w2b.shape[0]
    nt = (((1,), (1,)), ((), ()))

    def weight_copies(e, s):
        return (pltpu.make_async_copy(w1_hbm.at[layer, e], w1s.at[s], sems.at[0, s]),
                pltpu.make_async_copy(w2_hbm.at[layer, e], w2s.at[s], sems.at[1, s]))

    @pl.when(i < nused_ref[0])
    def _():
        e = be_ref[i]
        changed = jnp.logical_or(i == 0, e != be_ref[jnp.maximum(i - 1, 0)])

        @pl.when(changed)
        def _():
            @pl.when(i == 0)
            def _():
                slot_ref[0] = 0
                for cp in weight_copies(e, 0):
                    cp.start()

            s = slot_ref[0]
            for cp in weight_copies(e, s):
                cp.wait()
            j = i + gblk_ref[e]

            @pl.when(j < nused_ref[0])
            def _():
                for cp in weight_copies(be_ref[jnp.minimum(j, be_ref.shape[0] - 1)], 1 - s):
                    cp.start()

            for a in range(d // LANES):
                cols = slice(a * LANES, (a + 1) * LANES)
                tbuf[...] = w1s[s, cols, :].T
                w1gt[:, cols] = tbuf[pl.ds(0, de, stride=2), :].astype(BF16)
                w1lt[:, cols] = tbuf[pl.ds(1, de, stride=2), :].astype(BF16)
            w2b[...] = w2s[s].astype(BF16)
            slot_ref[0] = 1 - s

        bm = xs_ref.shape[0] // ROW_SUBLANES
        x = _load_row_tiles(xs_ref, 0, bm).astype(BF16)
        hg = lax.dot_general(x, w1gt[...], nt, preferred_element_type=F32) + b1g_ref[...]
        hl = lax.dot_general(x, w1lt[...], nt, preferred_element_type=F32) + b1l_ref[...]
        hg = jnp.minimum(hg, SWIGLU_LIMIT)
        hl = jnp.clip(hl, -SWIGLU_LIMIT, SWIGLU_LIMIT)
        act = hg * (1.0 / (1.0 + jnp.exp(-SWIGLU_ALPHA * hg))) * (hl + 1.0)
        y = jnp.dot(act.astype(BF16), w2b[...], preferred_element_type=F32) + b2_ref[...]
        _store_row_tiles(y_ref, 0, y)

    @pl.when(i >= nused_ref[0])
    def _():
        y_ref[...] = jnp.zeros_like(y_ref)


def _experts(xs, blk_exp, n_used, grp_blocks, layer, w1, b1g, b1l, w2, b2, bm):
    d = ROW_SUBLANES * LANES
    nb = blk_exp.shape[0]
    de = w2.shape[2]
    blk = (bm * ROW_SUBLANES, LANES)
    bmap = lambda i, be, nu, gb: (layer, be[i], 0, 0)
    any_spec = pl.BlockSpec(memory_space=pl.ANY)
    grid_spec = pltpu.PrefetchScalarGridSpec(
        num_scalar_prefetch=3,
        grid=(nb,),
        in_specs=[pl.BlockSpec(blk, lambda i, be, nu, gb: (jnp.minimum(i, nu[0] - 1), 0)),
                  any_spec,
                  pl.BlockSpec((None, None, 1, de), bmap), pl.BlockSpec((None, None, 1, de), bmap),
                  any_spec,
                  pl.BlockSpec((None, None, 1, d), bmap)],
        out_specs=pl.BlockSpec(blk, lambda i, be, nu, gb: (i, 0)),
        scratch_shapes=[pltpu.VMEM((de, d), BF16), pltpu.VMEM((de, d), BF16), pltpu.VMEM((de, d), BF16),
                        pltpu.VMEM((2 * de, LANES), F32),
                        pltpu.VMEM((2, d, 2 * de), F32), pltpu.VMEM((2, de, d), F32),
                        pltpu.SMEM((1,), jnp.int32), pltpu.SemaphoreType.DMA((2, 2))])
    return pl.pallas_call(
        functools.partial(_expert_kernel, layer=layer),
        grid_spec=grid_spec,
        out_shape=jax.ShapeDtypeStruct((nb * blk[0], LANES), F32),
        compiler_params=_cparams(("arbitrary",), 54 << 20),
        name="moe_experts",
    )(blk_exp, n_used, grp_blocks, xs, w1, b1g, b1l, w2, b2)


def _combine_kernel(dest_hbm, gate_ref, x_ref, g_ref, b_ref, y_hbm, o_ref, idx0, idx1, ybuf, sems, *, blk_off):
    i = pl.program_id(0)
    n = pl.num_programs(0)
    tm = x_ref.shape[0]
    idx = (idx0, idx1)
    slab_rows = tm * ROW_SUBLANES

    def slab(s, k):
        return (s * TOP_K + k) * tm

    def idx_copy(step, s):
        return pltpu.make_async_copy(dest_hbm.at[blk_off + step], idx[s], sems.at[s])

    def issue_rows(s):
        def body(j, c):
            for k in range(TOP_K):
                pltpu.make_async_copy(_tile_rows(y_hbm, idx[s][j * TOP_K + k]), _tile_rows(ybuf, slab(s, k) + j),
                                      sems.at[2 + s]).start(priority=k % DMA_QUEUES)
            return c
        lax.fori_loop(0, tm, body, 0, unroll=DMA_UNROLL)

    @pl.when(i == 0)
    def _():
        cp = idx_copy(0, 0)
        cp.start()
        cp.wait()
        issue_rows(0)

    @pl.when(jnp.logical_and(i == 0, n >= 2))
    def _():
        idx_copy(1, 1).start()

    def step(s):
        @pl.when(i + 1 < n)
        def _():
            idx_copy(i + 1, 1 - s).wait()
            issue_rows(1 - s)

        @pl.when(i + 2 < n)
        def _():
            idx_copy(i + 2, s).start()

        for k in range(TOP_K):
            pltpu.make_async_copy(y_hbm.at[pl.ds(0, slab_rows), :],
                                  ybuf.at[pl.ds(slab(s, k) * ROW_SUBLANES, slab_rows), :], sems.at[2 + s]).wait()
        gate = gate_ref[...]
        h = None
        for k in range(TOP_K):
            term = gate[:, k:k + 1] * _load_row_tiles(ybuf, slab(s, k) * ROW_SUBLANES, tm)
            h = term if h is None else h + term
        o_ref[...] = _layer_norm(DEEPNORM_ALPHA * x_ref[...] + h, g_ref[...], b_ref[...])

    for s in range(2):
        pl.when(i % 2 == s)(functools.partial(step, s))


def _combine(y, dest, gate, x, ln_g, ln_b, row_lo, rows):
    _, d = x.shape
    per = dest.shape[1]
    tm = per // TOP_K
    off = row_lo // tm
    vec = pl.BlockSpec((1, d), lambda i: (0, 0))
    return pl.pallas_call(
        functools.partial(_combine_kernel, blk_off=off),
        grid=(rows // tm,),
        in_specs=[pl.BlockSpec(memory_space=pl.ANY),
                  pl.BlockSpec((tm, TOP_K), lambda i: (off + i, 0)),
                  pl.BlockSpec((tm, d), lambda i: (off + i, 0)), vec, vec,
                  pl.BlockSpec(memory_space=pl.ANY)],
        out_specs=pl.BlockSpec((tm, d), lambda i: (i, 0)),
        out_shape=jax.ShapeDtypeStruct((rows, d), F32),
        scratch_shapes=[pltpu.SMEM((per,), jnp.int32), pltpu.SMEM((per,), jnp.int32),
                        pltpu.VMEM((2 * TOP_K * tm * ROW_SUBLANES, LANES), F32),
                        pltpu.SemaphoreType.DMA((4,))],
        compiler_params=_cparams(("arbitrary",), 40 << 20),
        name="moe_combine_ln",
    )(dest, gate, x, ln_g, ln_b, y)


def _moe_block(x, layer, router_w, router_b, w1, b1g, b1l, w2, b2):
    e, gate, rank, counts = _router(x, router_w, router_b)
    dest, blk_exp, n_used, pad_lo, pend, grp_blocks = _route_plan(e, rank, counts, EXPERT_ROWS, ROW_TILE)
    xs = _dispatch(x, dest, pad_lo, pend, n_used, EXPERT_ROWS)
    y = _experts(xs, blk_exp, n_used, grp_blocks, layer, w1, b1g, b1l, w2, b2, EXPERT_ROWS)
    return y, dest, gate


def _gelu(z):
    return 0.5 * z * (1.0 + jnp.tanh(0.7978845608028654 * (z + 0.044715 * (z * z * z))))


def _sgu_kernel(x_ref, win_ref, bin_ref, ng_ref, nb_ref, ws_ref, bs_ref, wout_ref, bout_ref,
                g_ref, b_ref, y_ref):
    tm = x_ref.shape[0]
    x = x_ref[...]
    xb = x.astype(BF16)
    dot = functools.partial(jnp.dot, preferred_element_type=F32)
    v = _gelu(dot(xb, win_ref[:, SGU_HALF:]) + bin_ref[:, SGU_HALF:])
    v = _layer_norm(v, ng_ref[...], nb_ref[...]).astype(BF16)
    acc = jnp.zeros((tm, D_MODEL), F32)
    for g in range(SGU_GROUPS):
        cols = slice(g * SGU_GROUP_DIM, (g + 1) * SGU_GROUP_DIM)
        u = _gelu(dot(xb, win_ref[:, cols]) + bin_ref[:, cols])
        mixed = [dot(ws_ref[g], v[c * SGU_CHUNK:(c + 1) * SGU_CHUNK, cols]) + bs_ref[:, g:g + 1]
                 for c in range(tm // SGU_CHUNK)]
        mixed = jnp.concatenate(mixed, axis=0)
        acc = acc + dot((u * mixed).astype(BF16), wout_ref[cols, :])
    h = acc + bout_ref[...]
    y_ref[...] = _layer_norm(DEEPNORM_ALPHA * x + h, g_ref[...], b_ref[...])


def _sgu_block(x, w_in, b_in, norm_g, norm_b, w_s, b_s_t, w_out, b_out, ln_g, ln_b):
    t, d = x.shape
    tm = SGU_ROWS
    row = lambda i: (i, 0)
    c2 = lambda i: (0, 0)
    c3 = lambda i: (0, 0, 0)
    once = dict(pipeline_mode=pl.Buffered(1))
    vec = pl.BlockSpec((1, d), c2)
    return pl.pallas_call(
        _sgu_kernel,
        grid=(t // tm,),
        in_specs=[pl.BlockSpec((tm, d), row),
                  pl.BlockSpec((d, 2 * SGU_HALF), c2, **once), pl.BlockSpec((1, 2 * SGU_HALF), c2),
                  pl.BlockSpec((1, SGU_HALF), c2), pl.BlockSpec((1, SGU_HALF), c2),
                  pl.BlockSpec((SGU_GROUPS, SGU_CHUNK, SGU_CHUNK), c3),
                  pl.BlockSpec((SGU_CHUNK, SGU_GROUPS), c2),
                  pl.BlockSpec((SGU_HALF, d), c2, **once), vec, vec, vec],
        out_specs=pl.BlockSpec((tm, d), row),
        out_shape=jax.ShapeDtypeStruct((t, d), F32),
        compiler_params=_cparams(("parallel",), 52 << 20),
        name="sgu_block",
    )(x, w_in, b_in, norm_g, norm_b, w_s, b_s_t, w_out, b_out, ln_g, ln_b)


def _attention_block(x, w_qkv, w_o, b_o, ln_g, ln_b, seq_rows):
    n_groups = N_GROUPS * HEADS
    slopes = (2.0 ** (-8.0 * jnp.arange(1, n_groups + 1, dtype=F32) / n_groups)).reshape(N_GROUPS, HEADS)
    scale = jnp.concatenate([jnp.full((GROUP_DIM,), HEAD_DIM ** -0.5, F32), jnp.ones((2 * GROUP_DIM,), F32)])
    os_, lses = [], []
    for g, (_, dil) in enumerate(DIL_CONFIGS):
        cols = [w_qkv[:, (j * N_GROUPS + g) * GROUP_DIM:(j * N_GROUPS + g + 1) * GROUP_DIM] for j in range(3)]
        w_g = (jnp.concatenate(cols, axis=1) * scale).astype(BF16)
        qkv = _qkv_project(x, w_g, dil)
        segs = tuple((n // dil, length // dil) for n, length in seq_rows)
        o, lse = _attention(qkv, _band_bias(slopes[g], dil), dil, segs)
        t = x.shape[0]
        os_.append(o.reshape(t, GROUP_DIM))
        lses.append(lse.reshape(t, LANES))
    return _merge_project(os_, lses, x, w_o.astype(BF16), b_o[None], ln_g[None], ln_b[None])


def kernel(x_prompt, x_sample, attn_w_qkv, attn_w_o, attn_b_o, sgu_w_in, sgu_b_in, sgu_norm_g, sgu_norm_b,
           sgu_w_s, sgu_b_s, sgu_w_out, sgu_b_out, router_w, router_b, exp_w1, exp_b1, exp_w2, exp_b2,
           ln_mix_g, ln_mix_b, ln_ffn_g, ln_ffn_b):
    d = x_prompt.shape[-1]
    tp = x_prompt.shape[0] * x_prompt.shape[1]
    ts = x_sample.shape[0] * x_sample.shape[1]
    seq_rows = ((tp, x_prompt.shape[1]), (ts, x_sample.shape[1]))
    x = jnp.concatenate([x_prompt.reshape(tp, d), x_sample.reshape(ts, d)], axis=0)

    b1g = exp_b1[:, :, None, 0::2]
    b1l = exp_b1[:, :, None, 1::2]
    b2 = exp_b2[:, :, None, :]

    outs = None
    for i in range(DEPTH):
        j = i // 2
        if i % 2 == 0:
            x = _attention_block(x, attn_w_qkv[j], attn_w_o[j], attn_b_o[j], ln_mix_g[i], ln_mix_b[i], seq_rows)
        else:
            x = _sgu_block(x, sgu_w_in[j].astype(BF16), sgu_b_in[j][None], sgu_norm_g[j][None], sgu_norm_b[j][None],
                           sgu_w_s[j].astype(BF16), jnp.transpose(sgu_b_s[j]), sgu_w_out[j].astype(BF16),
                           sgu_b_out[j][None], ln_mix_g[i][None], ln_mix_b[i][None])
        y, dest, gate = _moe_block(x, i, router_w[i], router_b[i][None], exp_w1, b1g, b1l, exp_w2, b2)
        g_ln, b_ln = ln_ffn_g[i][None], ln_ffn_b[i][None]
        if i + 1 < DEPTH:
            x = _combine(y, dest, gate, x, g_ln, b_ln, 0, tp + ts)
        else:
            outs = (_combine(y, dest, gate, x, g_ln, b_ln, 0, tp).reshape(x_prompt.shape),
                    _combine(y, dest, gate, x, g_ln, b_ln, tp, ts).reshape(x_sample.shape))
    return outs
```

```python
import functools

import jax
import jax.numpy as jnp
from jax import lax
from jax.experimental import pallas as pl
from jax.experimental.pallas import tpu as pltpu

F32 = jnp.float32
BF16 = jnp.bfloat16

LANES = 128
SUBLANES = 8
VMEM_LIMIT_CAP = 56 * 1024 * 1024

D_MODEL = 1024
DIL_CONFIGS = ((128, 1), (512, 4), (2048, 16))
N_GROUPS = len(DIL_CONFIGS)
HEADS = 8
HEAD_DIM = 64
GROUP_DIM = HEADS * HEAD_DIM
N_SIDE = 64
Q_BLOCK = 128
KV_BLOCK = N_SIDE
KV_WIN = Q_BLOCK + 2 * N_SIDE
ATTN_Q_STEPS = (8, 4, 2, 1)
SGU_CHUNK = 128
SGU_HALF = 3 * D_MODEL
SGU_GROUPS = 8
SGU_GROUP_DIM = SGU_HALF // SGU_GROUPS
N_EXPERTS = 32
TOP_K = 4
D_EXPERT = D_MODEL
SWIGLU_ALPHA = 1.702
SWIGLU_LIMIT = 7.0
LN_EPS = 1e-5
DEPTH = 2
DEEPNORM_ALPHA = (2 * DEPTH) ** 0.25
NEG = -1e30

ROW_TILE = 512
QKV_ROWS = 1024
RANK_CHUNK = 256
SGU_ROWS = 512
EXPERT_ROWS = 512
DMA_UNROLL = 8
DMA_QUEUES = 2


def _cparams(semantics, vmem_bytes):
    return pltpu.CompilerParams(dimension_semantics=semantics,
                                vmem_limit_bytes=min(int(vmem_bytes), VMEM_LIMIT_CAP))


def _layer_norm(y, g, b):
    mu = jnp.mean(y, axis=-1, keepdims=True)
    yc = y - mu
    var = jnp.mean(yc * yc, axis=-1, keepdims=True)
    return yc * lax.rsqrt(var + LN_EPS) * g + b


def _views_kernel(xa_ref, xb_ref, *refs, na, dils):
    x_ref, view_refs, slab = refs[0], refs[1:-1], refs[-1]
    tm, d = x_ref.shape
    x = jnp.where(pl.program_id(0) < na, xa_ref[...], xb_ref[...])
    x_ref[...] = x
    for c in range(d // LANES):
        slab[c] = x[:, c * LANES:(c + 1) * LANES]
    for dil, v_ref in zip(dils, view_refs):
        for r in range(dil):
            for c in range(d // LANES):
                v_ref[:, r * d + c * LANES:r * d + (c + 1) * LANES] = slab[c, pl.ds(r, tm // dil, stride=dil), :]


def _token_views(xa, xb, dils):
    ta, d = xa.shape
    tb = xb.shape[0]
    t = ta + tb
    tm = ROW_TILE
    na, nb = ta // tm, tb // tm
    out_shapes = [jax.ShapeDtypeStruct((t, d), F32)] + [jax.ShapeDtypeStruct((t // dil, dil * d), F32) for dil in dils]
    out_specs = [pl.BlockSpec((tm, d), lambda i: (i, 0))] + [pl.BlockSpec((tm // dil, dil * d), lambda i: (i, 0))
                                                             for dil in dils]
    return pl.pallas_call(
        functools.partial(_views_kernel, na=na, dils=dils),
        grid=(na + nb,),
        in_specs=[pl.BlockSpec((tm, d), lambda i: (jnp.minimum(i, na - 1), 0)),
                  pl.BlockSpec((tm, d), lambda i: (jnp.maximum(i - na, 0), 0))],
        out_specs=out_specs,
        out_shape=out_shapes,
        scratch_shapes=[pltpu.VMEM((d // LANES, tm, LANES), F32)],
        compiler_params=_cparams(("arbitrary",), 40 << 20),
        name="token_views",
    )(xa, xb)


def _qkv_kernel(x_ref, w_ref, o_ref):
    o_ref[...] = jnp.dot(x_ref[...].astype(BF16), w_ref[...],
                         preferred_element_type=F32).astype(BF16)


def _qkv_project(xv, w, dil):
    n = xv.shape[0]
    d = xv.shape[1] // dil
    tm = min(QKV_ROWS, n)
    width = w.shape[1]
    return pl.pallas_call(
        _qkv_kernel,
        grid=(dil, n // tm),
        in_specs=[pl.BlockSpec((tm, d), lambda r, i: (i, r)),
                  pl.BlockSpec((d, width), lambda r, i: (0, 0))],
        out_specs=pl.BlockSpec((None, tm, width), lambda r, i: (r, i, 0)),
        out_shape=jax.ShapeDtypeStruct((dil, n, width), BF16),
        compiler_params=_cparams(("parallel", "parallel"),
                                 2 * (tm * d * 4 + d * width * 2 + tm * width * 2) + tm * width * 4
                                 + (8 << 20)),
        name=f"qkv_project_d{dil}",
    )(xv, w)


def _attn_kernel(q_ref, kp_ref, kb_ref, kn_ref, vp_ref, vb_ref, vn_ref, bias_ref, o_ref, lse_ref, *, segs):
    (n_a, len_a), (_, len_b) = segs
    step_rows = q_ref.shape[0]
    lane = lax.broadcasted_iota(jnp.int32, (1, LANES), 1)
    first_head = lane < HEAD_DIM
    nt = (((1,), (1,)), ((), ()))

    def window(prev, body, nxt, first, sl):
        parts = []
        if first < 0:
            parts.append(prev[:, sl])
        parts.append(body[max(first, 0):min(first + KV_WIN, step_rows), sl])
        if first + KV_WIN > step_rows:
            parts.append(nxt[:, sl])
        return jnp.concatenate(parts, axis=0)

    for sub in range(step_rows // Q_BLOCK):
        rows = slice(sub * Q_BLOCK, (sub + 1) * Q_BLOCK)
        row0 = pl.program_id(1) * step_rows + sub * Q_BLOCK
        in_a = row0 < n_a
        lo = jnp.where(in_a, (row0 // len_a) * len_a, n_a + ((row0 - n_a) // len_b) * len_b)
        hi = lo + jnp.where(in_a, len_a, len_b)
        krow = lax.broadcasted_iota(jnp.int32, (1, KV_WIN), 1) + (row0 - N_SIDE)
        col_bias = jnp.where((krow >= lo) & (krow < hi), 0.0, NEG).astype(F32)
        lse_tile = jnp.zeros((Q_BLOCK, LANES), F32)
        for p in range(HEADS // 2):
            sl = slice(p * LANES, (p + 1) * LANES)
            q = q_ref[rows, sl]
            k = window(kp_ref, kb_ref, kn_ref, sub * Q_BLOCK - N_SIDE, sl)
            v = window(vp_ref, vb_ref, vn_ref, sub * Q_BLOCK - N_SIDE, sl)
            zero = jnp.zeros_like(q)
            outs, inv_ls = [], []
            for j, qh in enumerate((jnp.where(first_head, q, zero), jnp.where(first_head, zero, q))):
                h = 2 * p + j
                s = lax.dot_general(qh, k, nt, preferred_element_type=F32)
                s = s + bias_ref[h] + col_bias
                m = jnp.max(s, axis=-1, keepdims=True)
                e = jnp.exp(s - m)
                l = jnp.sum(e, axis=-1, keepdims=True)
                outs.append(jnp.dot(e.astype(BF16), v, preferred_element_type=F32))
                inv_ls.append(1.0 / l)
                lse_tile = jnp.where(lane == h, m + jnp.log(l), lse_tile)
            o = jnp.where(first_head, outs[0] * inv_ls[0], outs[1] * inv_ls[1])
            o_ref[rows, sl] = o.astype(BF16)
        lse_ref[rows, :] = lse_tile


def _band_bias(slopes, dil):
    qi = jnp.arange(Q_BLOCK)[:, None]
    kc = jnp.arange(KV_WIN)[None, :]
    dist = jnp.abs(kc - N_SIDE - qi)
    pen = -slopes[:, None, None] * (dist * dil).astype(F32)[None]
    return jnp.where((dist <= N_SIDE)[None], pen, NEG).astype(F32)


def _attention(qkv, bias, dil, segs):
    _, n, _ = qkv.shape
    nkb = n // KV_BLOCK
    q_step = max(s for s in ATTN_Q_STEPS if (n // Q_BLOCK) % s == 0)
    step_rows = q_step * Q_BLOCK
    per_step = step_rows // KV_BLOCK

    def body_spec(col):
        return pl.BlockSpec((None, step_rows, GROUP_DIM), lambda r, i: (r, i, col))

    def edge_spec(col, after):
        def imap(r, i):
            return (r, jnp.clip((i + after) * per_step - 1 + after, 0, nkb - 1), col)
        return pl.BlockSpec((None, KV_BLOCK, GROUP_DIM), imap)

    in_specs = ([body_spec(0)]
                + [edge_spec(1, 0), body_spec(1), edge_spec(1, 1)]
                + [edge_spec(2, 0), body_spec(2), edge_spec(2, 1)]
                + [pl.BlockSpec((HEADS, Q_BLOCK, KV_WIN), lambda r, i: (0, 0, 0))])
    return pl.pallas_call(
        functools.partial(_attn_kernel, segs=segs),
        grid=(dil, n // step_rows),
        in_specs=in_specs,
        out_specs=[pl.BlockSpec((step_rows, GROUP_DIM), lambda r, i: (i, r)),
                   pl.BlockSpec((step_rows, LANES), lambda r, i: (i, r))],
        out_shape=[jax.ShapeDtypeStruct((n, dil * GROUP_DIM), BF16),
                   jax.ShapeDtypeStruct((n, dil * LANES), F32)],
        compiler_params=_cparams(("parallel", "parallel"), 32 << 20),
        name=f"band_attention_d{dil}",
    )(*([qkv] * 7), bias)


def _merge_kernel(o0, o1, o2, l0, l1, l2, x_ref, wo_ref, bo_ref, g_ref, b_ref, y_ref):
    lses = [l0[...], l1[...], l2[...]]
    m = jnp.maximum(jnp.maximum(lses[0], lses[1]), lses[2])
    es = [jnp.exp(l - m) for l in lses]
    inv = 1.0 / (es[0] + es[1] + es[2])
    ws = [e * inv for e in es]
    lane = lax.broadcasted_iota(jnp.int32, (1, LANES), 1)
    first_head = lane < HEAD_DIM
    parts = []
    for p in range(HEADS // 2):
        sl = slice(p * LANES, (p + 1) * LANES)
        acc = None
        for w, o in zip(ws, (o0, o1, o2)):
            wp = jnp.where(first_head, w[:, 2 * p:2 * p + 1], w[:, 2 * p + 1:2 * p + 2])
            term = wp * o[:, sl].astype(F32)
            acc = term if acc is None else acc + term
        parts.append(acc)
    merged = jnp.concatenate(parts, axis=-1).astype(BF16)
    h = jnp.dot(merged, wo_ref[...], preferred_element_type=F32) + bo_ref[...]
    y_ref[...] = _layer_norm(DEEPNORM_ALPHA * x_ref[...] + h, g_ref[...], b_ref[...])


def _merge_project(os_, lses, x, w_o, b_o, ln_g, ln_b):
    t, d = x.shape
    tm = ROW_TILE
    row = lambda i: (i, 0)
    const = lambda i: (0, 0)
    vec = pl.BlockSpec((1, d), const)
    return pl.pallas_call(
        _merge_kernel,
        grid=(t // tm,),
        in_specs=([pl.BlockSpec((tm, GROUP_DIM), row)] * 3 + [pl.BlockSpec((tm, LANES), row)] * 3
                  + [pl.BlockSpec((tm, d), row), pl.BlockSpec((GROUP_DIM, d), const), vec, vec, vec]),
        out_specs=pl.BlockSpec((tm, d), row),
        out_shape=jax.ShapeDtypeStruct((t, d), F32),
        compiler_params=_cparams(("parallel",), 32 << 20),
        name="merge_outproj_ln",
    )(*os_, *lses, x, w_o, b_o, ln_g, ln_b)


def _split_bf16(a):
    hi = a.astype(BF16)
    return hi, (a - hi.astype(F32)).astype(BF16)


def _router_kernel(x_ref, w_ref, b_ref, e_ref, g_ref, r_ref, cnt_ref, carry_ref):
    tm = x_ref.shape[0]

    @pl.when(pl.program_id(0) == 0)
    def _():
        carry_ref[...] = jnp.zeros_like(carry_ref)

    xh, xl = _split_bf16(x_ref[...])
    wh, wl = _split_bf16(w_ref[...])
    dot = functools.partial(jnp.dot, preferred_element_type=F32)
    logits = dot(xh, wh) + (dot(xl, wh) + dot(xh, wl)) + b_ref[...]
    lane = lax.broadcasted_iota(jnp.int32, logits.shape, 1).astype(F32)
    col4 = lax.broadcasted_iota(jnp.int32, (tm, TOP_K), 1)
    work = logits
    picks, vals, idxs = [], [], []
    for _k in range(TOP_K):
        mx = jnp.max(work, axis=-1, keepdims=True)
        idx = jnp.min(jnp.where(work == mx, lane, float(N_EXPERTS)), axis=-1, keepdims=True)
        pick = lane == idx
        work = jnp.where(pick, -jnp.inf, work)
        picks.append(pick)
        vals.append(mx)
        idxs.append(idx)
    ex = [jnp.exp(v - vals[0]) for v in vals]
    inv = 1.0 / (ex[0] + ex[1] + ex[2] + ex[3])
    chosen = jnp.zeros(logits.shape, F32)
    for pick in picks:
        chosen = chosen + pick.astype(F32)
    ch = min(RANK_CHUNK, tm)
    r_i = lax.broadcasted_iota(jnp.int32, (ch, ch), 0)
    c_i = lax.broadcasted_iota(jnp.int32, (ch, ch), 1)
    tri = (c_i < r_i).astype(BF16)
    carry = carry_ref[...]
    before = []
    for c in range(tm // ch):
        part = chosen[c * ch:(c + 1) * ch]
        before.append(dot(tri, part.astype(BF16)) + carry)
        carry = carry + jnp.sum(part, axis=0, keepdims=True)
    before = jnp.concatenate(before, axis=0)
    e_out = jnp.zeros((tm, TOP_K), jnp.int32)
    g_out = jnp.zeros((tm, TOP_K), F32)
    r_out = jnp.zeros((tm, TOP_K), jnp.int32)
    for k in range(TOP_K):
        rank = jnp.sum(jnp.where(picks[k], before, 0.0), axis=-1, keepdims=True)
        e_out = jnp.where(col4 == k, idxs[k].astype(jnp.int32), e_out)
        g_out = jnp.where(col4 == k, ex[k] * inv, g_out)
        r_out = jnp.where(col4 == k, rank.astype(jnp.int32), r_out)
    e_ref[...] = e_out
    g_ref[...] = g_out
    r_ref[...] = r_out
    carry_ref[...] = carry
    cnt_ref[...] = carry


def _router(x, w, b):
    t, d = x.shape
    tm = ROW_TILE
    row = lambda i: (i, 0)
    const = lambda i: (0, 0)
    k_spec = pl.BlockSpec((tm, TOP_K), row)
    return pl.pallas_call(
        _router_kernel,
        grid=(t // tm,),
        in_specs=[pl.BlockSpec((tm, d), row), pl.BlockSpec((d, N_EXPERTS), const),
                  pl.BlockSpec((1, N_EXPERTS), const)],
        out_specs=[k_spec, k_spec, k_spec, pl.BlockSpec((1, N_EXPERTS), const)],
        out_shape=[jax.ShapeDtypeStruct((t, TOP_K), jnp.int32), jax.ShapeDtypeStruct((t, TOP_K), F32),
                   jax.ShapeDtypeStruct((t, TOP_K), jnp.int32), jax.ShapeDtypeStruct((1, N_EXPERTS), F32)],
        scratch_shapes=[pltpu.VMEM((1, N_EXPERTS), F32)],
        compiler_params=_cparams(("arbitrary",), 32 << 20),
        name="router_topk",
    )(x, w, b)


def _route_plan(e, rank, counts, bm, tm):
    t = e.shape[0]
    nb = TOP_K * t // bm + N_EXPERTS
    counts = counts.reshape(N_EXPERTS).astype(jnp.int32)
    padded = (counts + bm - 1) // bm * bm
    pend = jnp.cumsum(padded)
    pstart = pend - padded
    onehot = e[..., None] == jnp.arange(N_EXPERTS, dtype=jnp.int32)
    dest = jnp.sum(jnp.where(onehot, pstart, 0), axis=-1) + rank
    blk_row0 = jnp.arange(nb, dtype=jnp.int32) * bm
    blk_exp = jnp.minimum(jnp.sum((pend[None, :] <= blk_row0[:, None]).astype(jnp.int32), axis=1),
                          N_EXPERTS - 1)
    n_used = pend[-1:] // bm
    return dest.reshape(t // tm, tm * TOP_K), blk_exp, n_used, pstart + counts, pend, padded // bm


ROW_SUBLANES = D_MODEL // LANES


def _store_row_tiles(ref, base, x):
    for c in range(ROW_SUBLANES):
        ref[pl.ds(base + c, x.shape[0], stride=ROW_SUBLANES), :] = x[:, c * LANES:(c + 1) * LANES]


def _load_row_tiles(ref, base, n):
    return jnp.concatenate([ref[pl.ds(base + c, n, stride=ROW_SUBLANES), :] for c in range(ROW_SUBLANES)], axis=1)


def _tile_rows(ref, row):
    return ref.at[pl.ds(pl.multiple_of(row * ROW_SUBLANES, ROW_SUBLANES), ROW_SUBLANES), :]


def _dispatch_kernel(padlo_ref, pend_ref, nused_ref, dest_hbm, x_ref, xs_hbm, idx0, idx1, pk, zbuf, sems):
    i = pl.program_id(0)
    n = pl.num_programs(0)
    tm = x_ref.shape[0]
    bm = zbuf.shape[0] // ROW_SUBLANES
    dump = xs_hbm.shape[0] // ROW_SUBLANES - bm
    idx = (idx0, idx1)
    slot_rows = tm * ROW_SUBLANES

    def idx_copy(step, s):
        return pltpu.make_async_copy(dest_hbm.at[step], idx[s], sems.at[s])

    def rows_wait(s):
        for _ in range(TOP_K):
            pltpu.make_async_copy(pk.at[pl.ds(s * slot_rows, slot_rows), :], xs_hbm.at[pl.ds(0, slot_rows), :],
                                  sems.at[2 + s]).wait()

    @pl.when(i == 0)
    def _():
        idx_copy(0, 0).start()
        zbuf[...] = jnp.zeros_like(zbuf)

        def zero_block(b, c):
            cp = pltpu.make_async_copy(zbuf, xs_hbm.at[pl.ds(pl.multiple_of(b * zbuf.shape[0], ROW_SUBLANES),
                                                             zbuf.shape[0]), :], sems.at[4])
            cp.start()
            cp.wait()
            return c
        lax.fori_loop(nused_ref[0], dump // bm + 1, zero_block, 0)
        for e in range(N_EXPERTS):
            def zero_row(jj, c, e=e):
                for u in range(DMA_QUEUES):
                    j = jj * DMA_QUEUES + u
                    r = padlo_ref[e] + j
                    r = jnp.where(r < pend_ref[e], r, dump + j)
                    pltpu.make_async_copy(_tile_rows(zbuf, j), _tile_rows(xs_hbm, r), sems.at[4]).start(priority=u)
                return c
            lax.fori_loop(0, bm // DMA_QUEUES, zero_row, 0, unroll=DMA_UNROLL // DMA_QUEUES)
            pltpu.make_async_copy(zbuf, xs_hbm.at[pl.ds(0, zbuf.shape[0]), :], sems.at[4]).wait()

    def step(s):
        idx_copy(i, s).wait()

        @pl.when(i + 1 < n)
        def _():
            idx_copy(i + 1, 1 - s).start()

        @pl.when(i >= 2)
        def _():
            rows_wait(s)

        _store_row_tiles(pk, s * slot_rows, x_ref[...])

        def issue(j, c):
            for k in range(TOP_K):
                pltpu.make_async_copy(_tile_rows(pk, s * tm + j), _tile_rows(xs_hbm, idx[s][j * TOP_K + k]),
                                      sems.at[2 + s]).start(priority=k % DMA_QUEUES)
            return c
        lax.fori_loop(0, tm, issue, 0, unroll=DMA_UNROLL)

        @pl.when(i == n - 1)
        def _():
            rows_wait(s)

        @pl.when(jnp.logical_and(i == n - 1, n >= 2))
        def _():
            rows_wait(1 - s)

    for s in range(2):
        pl.when(i % 2 == s)(functools.partial(step, s))


def _dispatch(x, dest, pad_lo, pend, n_used, bm):
    t, d = x.shape
    nblk, per = dest.shape
    tm = per // TOP_K
    nb = TOP_K * t // bm + N_EXPERTS
    grid_spec = pltpu.PrefetchScalarGridSpec(
        num_scalar_prefetch=3,
        grid=(nblk,),
        in_specs=[pl.BlockSpec(memory_space=pl.ANY), pl.BlockSpec((tm, d), lambda i, a, b, c: (i, 0))],
        out_specs=pl.BlockSpec(memory_space=pl.ANY),
        scratch_shapes=[pltpu.SMEM((per,), jnp.int32), pltpu.SMEM((per,), jnp.int32),
                        pltpu.VMEM((2 * tm * ROW_SUBLANES, LANES), F32),
                        pltpu.VMEM((bm * ROW_SUBLANES, LANES), F32),
                        pltpu.SemaphoreType.DMA((5,))])
    return pl.pallas_call(
        _dispatch_kernel,
        grid_spec=grid_spec,
        out_shape=jax.ShapeDtypeStruct(((nb * bm + bm) * ROW_SUBLANES, LANES), F32),
        compiler_params=_cparams(("arbitrary",), 32 << 20),
        name="moe_dispatch",
    )(pad_lo, pend, n_used, dest, x)


def _expert_kernel(be_ref, nused_ref, gblk_ref, xs_ref, w1_hbm, b1g_ref, b1l_ref, w2_hbm, b2_ref, y_ref,
                   w1gt, w1lt, w2b, tbuf, w1s, w2s, slot_ref, sems, *, layer):
    i = pl.program_id(0)
    d, de = w2b.shape[1], w2b.shape[0]
    nt = (((1,), (1,)), ((), ()))

    def weight_copies(e, s):
        return (pltpu.make_async_copy(w1_hbm.at[layer, e], w1s.at[s], sems.at[0, s]),
                pltpu.make_async_copy(w2_hbm.at[layer, e], w2s.at[s], sems.at[1, s]))

    @pl.when(i < nused_ref[0])
    def _():
        e = be_ref[i]
        changed = jnp.logical_or(i == 0, e != be_ref[jnp.maximum(i - 1, 0)])

        @pl.when(changed)
        def _():
            @pl.when(i == 0)
            def _():
                slot_ref[0] = 0
                for cp in weight_copies(e, 0):
                    cp.start()

            s = slot_ref[0]
            for cp in weight_copies(e, s):
                cp.wait()
            j = i + gblk_ref[e]

            @pl.when(j < nused_ref[0])
            def _():
                for cp in weight_copies(be_ref[jnp.minimum(j, be_ref.shape[0] - 1)], 1 - s):
                    cp.start()

            for a in range(d // LANES):
                cols = slice(a * LANES, (a + 1) * LANES)
                tbuf[...] = w1s[s, cols, :].T
                w1gt[:, cols] = tbuf[pl.ds(0, de, stride=2), :].astype(BF16)
                w1lt[:, cols] = tbuf[pl.ds(1, de, stride=2), :].astype(BF16)
            w2b[...] = w2s[s].astype(BF16)
            slot_ref[0] = 1 - s

        bm = xs_ref.shape[0] // ROW_SUBLANES
        x = _load_row_tiles(xs_ref, 0, bm).astype(BF16)
        hg = lax.dot_general(x, w1gt[...], nt, preferred_element_type=F32) + b1g_ref[...]
        hl = lax.dot_general(x, w1lt[...], nt, preferred_element_type=F32) + b1l_ref[...]
        hg = jnp.minimum(hg, SWIGLU_LIMIT)
        hl = jnp.clip(hl, -SWIGLU_LIMIT, SWIGLU_LIMIT)
        act = hg * (1.0 / (1.0 + jnp.exp(-SWIGLU_ALPHA * hg))) * (hl + 1.0)
        y = jnp.dot(act.astype(BF16), w2b[...], preferred_element_type=F32) + b2_ref[...]
        _store_row_tiles(y_ref, 0, y)

    @pl.when(i >= nused_ref[0])
    def _():
        y_ref[...] = jnp.zeros_like(y_ref)


def _experts(xs, blk_exp, n_used, grp_blocks, layer, w1, b1g, b1l, w2, b2, bm):
    d = ROW_SUBLANES * LANES
    nb = blk_exp.shape[0]
    de = w2.shape[2]
    blk = (bm * ROW_SUBLANES, LANES)
    bmap = lambda i, be, nu, gb: (layer, be[i], 0, 0)
    any_spec = pl.BlockSpec(memory_space=pl.ANY)
    grid_spec = pltpu.PrefetchScalarGridSpec(
        num_scalar_prefetch=3,
        grid=(nb,),
        in_specs=[pl.BlockSpec(blk, lambda i, be, nu, gb: (jnp.minimum(i, nu[0] - 1), 0)),
                  any_spec,
                  pl.BlockSpec((None, None, 1, de), bmap), pl.BlockSpec((None, None, 1, de), bmap),
                  any_spec,
                  pl.BlockSpec((None, None, 1, d), bmap)],
        out_specs=pl.BlockSpec(blk, lambda i, be, nu, gb: (i, 0)),
        scratch_shapes=[pltpu.VMEM((de, d), BF16), pltpu.VMEM((de, d), BF16), pltpu.VMEM((de, d), BF16),
                        pltpu.VMEM((2 * de, LANES), F32),
                        pltpu.VMEM((2, d, 2 * de), F32), pltpu.VMEM((2, de, d), F32),
                        pltpu.SMEM((1,), jnp.int32), pltpu.SemaphoreType.DMA((2, 2))])
    return pl.pallas_call(
        functools.partial(_expert_kernel, layer=layer),
        grid_spec=grid_spec,
        out_shape=jax.ShapeDtypeStruct((nb * blk[0], LANES), F32),
        compiler_params=_cparams(("arbitrary",), 54 << 20),
        name="moe_experts",
    )(blk_exp, n_used, grp_blocks, xs, w1, b1g, b1l, w2, b2)


def _combine_kernel(dest_hbm, gate_ref, x_ref, g_ref, b_ref, y_hbm, o_ref, idx0, idx1, ybuf, sems, *, blk_off):
    i = pl.program_id(0)
    n = pl.num_programs(0)
    tm = x_ref.shape[0]
    idx = (idx0, idx1)
    slab_rows = tm * ROW_SUBLANES

    def slab(s, k):
        return (s * TOP_K + k) * tm

    def idx_copy(step, s):
        return pltpu.make_async_copy(dest_hbm.at[blk_off + step], idx[s], sems.at[s])

    def issue_rows(s):
        def body(j, c):
            for k in range(TOP_K):
                pltpu.make_async_copy(_tile_rows(y_hbm, idx[s][j * TOP_K + k]), _tile_rows(ybuf, slab(s, k) + j),
                                      sems.at[2 + s]).start(priority=k % DMA_QUEUES)
            return c
        lax.fori_loop(0, tm, body, 0, unroll=DMA_UNROLL)

    @pl.when(i == 0)
    def _():
        cp = idx_copy(0, 0)
        cp.start()
        cp.wait()
        issue_rows(0)

    @pl.when(jnp.logical_and(i == 0, n >= 2))
    def _():
        idx_copy(1, 1).start()

    def step(s):
        @pl.when(i + 1 < n)
        def _():
            idx_copy(i + 1, 1 - s).wait()
            issue_rows(1 - s)

        @pl.when(i + 2 < n)
        def _():
            idx_copy(i + 2, s).start()

        for k in range(TOP_K):
            pltpu.make_async_copy(y_hbm.at[pl.ds(0, slab_rows), :],
                                  ybuf.at[pl.ds(slab(s, k) * ROW_SUBLANES, slab_rows), :], sems.at[2 + s]).wait()
        gate = gate_ref[...]
        h = None
        for k in range(TOP_K):
            term = gate[:, k:k + 1] * _load_row_tiles(ybuf, slab(s, k) * ROW_SUBLANES, tm)
            h = term if h is None else h + term
        o_ref[...] = _layer_norm(DEEPNORM_ALPHA * x_ref[...] + h, g_ref[...], b_ref[...])

    for s in range(2):
        pl.when(i % 2 == s)(functools.partial(step, s))


def _combine(y, dest, gate, x, ln_g, ln_b, row_lo, rows):
    _, d = x.shape
    per = dest.shape[1]
    tm = per // TOP_K
    off = row_lo // tm
    vec = pl.BlockSpec((1, d), lambda i: (0, 0))
    return pl.pallas_call(
        functools.partial(_combine_kernel, blk_off=off),
        grid=(rows // tm,),
        in_specs=[pl.BlockSpec(memory_space=pl.ANY),
                  pl.BlockSpec((tm, TOP_K), lambda i: (off + i, 0)),
                  pl.BlockSpec((tm, d), lambda i: (off + i, 0)), vec, vec,
                  pl.BlockSpec(memory_space=pl.ANY)],
        out_specs=pl.BlockSpec((tm, d), lambda i: (i, 0)),
        out_shape=jax.ShapeDtypeStruct((rows, d), F32),
        scratch_shapes=[pltpu.SMEM((per,), jnp.int32), pltpu.SMEM((per,), jnp.int32),
                        pltpu.VMEM((2 * TOP_K * tm * ROW_SUBLANES, LANES), F32),
                        pltpu.SemaphoreType.DMA((4,))],
        compiler_params=_cparams(("arbitrary",), 40 << 20),
        name="moe_combine_ln",
    )(dest, gate, x, ln_g, ln_b, y)


def _moe_block(x, layer, router_w, router_b, w1, b1g, b1l, w2, b2):
    e, gate, rank, counts = _router(x, router_w, router_b)
    dest, blk_exp, n_used, pad_lo, pend, grp_blocks = _route_plan(e, rank, counts, EXPERT_ROWS, ROW_TILE)
    xs = _dispatch(x, dest, pad_lo, pend, n_used, EXPERT_ROWS)
    y = _experts(xs, blk_exp, n_used, grp_blocks, layer, w1, b1g, b1l, w2, b2, EXPERT_ROWS)
    return y, dest, gate


def _gelu(z):
    return 0.5 * z * (1.0 + jnp.tanh(0.7978845608028654 * (z + 0.044715 * (z * z * z))))


def _sgu_kernel(x_ref, win_ref, bin_ref, ng_ref, nb_ref, ws_ref, bs_ref, wout_ref, bout_ref,
                g_ref, b_ref, y_ref):
    tm = x_ref.shape[0]
    x = x_ref[...]
    xb = x.astype(BF16)
    dot = functools.partial(jnp.dot, preferred_element_type=F32)
    v = _gelu(dot(xb, win_ref[:, SGU_HALF:]) + bin_ref[:, SGU_HALF:])
    v = _layer_norm(v, ng_ref[...], nb_ref[...]).astype(BF16)
    acc = jnp.zeros((tm, D_MODEL), F32)
    for g in range(SGU_GROUPS):
        cols = slice(g * SGU_GROUP_DIM, (g + 1) * SGU_GROUP_DIM)
        u = _gelu(dot(xb, win_ref[:, cols]) + bin_ref[:, cols])
        mixed = [dot(ws_ref[g], v[c * SGU_CHUNK:(c + 1) * SGU_CHUNK, cols]) + bs_ref[:, g:g + 1]
                 for c in range(tm // SGU_CHUNK)]
        mixed = jnp.concatenate(mixed, axis=0)
        acc = acc + dot((u * mixed).astype(BF16), wout_ref[cols, :])
    h = acc + bout_ref[...]
    y_ref[...] = _layer_norm(DEEPNORM_ALPHA * x + h, g_ref[...], b_ref[...])


def _sgu_block(x, w_in, b_in, norm_g, norm_b, w_s, b_s_t, w_out, b_out, ln_g, ln_b):
    t, d = x.shape
    tm = SGU_ROWS
    row = lambda i: (i, 0)
    c2 = lambda i: (0, 0)
    c3 = lambda i: (0, 0, 0)
    once = dict(pipeline_mode=pl.Buffered(1))
    vec = pl.BlockSpec((1, d), c2)
    return pl.pallas_call(
        _sgu_kernel,
        grid=(t // tm,),
        in_specs=[pl.BlockSpec((tm, d), row),
                  pl.BlockSpec((d, 2 * SGU_HALF), c2, **once), pl.BlockSpec((1, 2 * SGU_HALF), c2),
                  pl.BlockSpec((1, SGU_HALF), c2), pl.BlockSpec((1, SGU_HALF), c2),
                  pl.BlockSpec((SGU_GROUPS, SGU_CHUNK, SGU_CHUNK), c3),
                  pl.BlockSpec((SGU_CHUNK, SGU_GROUPS), c2),
                  pl.BlockSpec((SGU_HALF, d), c2, **once), vec, vec, vec],
        out_specs=pl.BlockSpec((tm, d), row),
        out_shape=jax.ShapeDtypeStruct((t, d), F32),
        compiler_params=_cparams(("parallel",), 52 << 20),
        name="sgu_block",
    )(x, w_in, b_in, norm_g, norm_b, w_s, b_s_t, w_out, b_out, ln_g, ln_b)


def _attention_block(x, views, w_qkv, w_o, b_o, ln_g, ln_b, seq_rows):
    n_groups = N_GROUPS * HEADS
    slopes = (2.0 ** (-8.0 * jnp.arange(1, n_groups + 1, dtype=F32) / n_groups)).reshape(N_GROUPS, HEADS)
    scale = jnp.concatenate([jnp.full((GROUP_DIM,), HEAD_DIM ** -0.5, F32), jnp.ones((2 * GROUP_DIM,), F32)])
    os_, lses = [], []
    for g, (_, dil) in enumerate(DIL_CONFIGS):
        cols = [w_qkv[:, (j * N_GROUPS + g) * GROUP_DIM:(j * N_GROUPS + g + 1) * GROUP_DIM] for j in range(3)]
        w_g = (jnp.concatenate(cols, axis=1) * scale).astype(BF16)
        qkv = _qkv_project(views[dil], w_g, dil)
        segs = tuple((n // dil, length // dil) for n, length in seq_rows)
        o, lse = _attention(qkv, _band_bias(slopes[g], dil), dil, segs)
        t = x.shape[0]
        os_.append(o.reshape(t, GROUP_DIM))
        lses.append(lse.reshape(t, LANES))
    return _merge_project(os_, lses, x, w_o.astype(BF16), b_o[None], ln_g[None], ln_b[None])


def kernel(x_prompt, x_sample, attn_w_qkv, attn_w_o, attn_b_o, sgu_w_in, sgu_b_in, sgu_norm_g, sgu_norm_b,
           sgu_w_s, sgu_b_s, sgu_w_out, sgu_b_out, router_w, router_b, exp_w1, exp_b1, exp_w2, exp_b2,
           ln_mix_g, ln_mix_b, ln_ffn_g, ln_ffn_b):
    d = x_prompt.shape[-1]
    tp = x_prompt.shape[0] * x_prompt.shape[1]
    ts = x_sample.shape[0] * x_sample.shape[1]
    seq_rows = ((tp, x_prompt.shape[1]), (ts, x_sample.shape[1]))
    dils = tuple(dil for _, dil in DIL_CONFIGS if dil > 1)
    x, *xviews = _token_views(x_prompt.reshape(tp, d), x_sample.reshape(ts, d), dils)
    views = dict(zip(dils, xviews))
    views[1] = x

    b1g = exp_b1[:, :, None, 0::2]
    b1l = exp_b1[:, :, None, 1::2]
    b2 = exp_b2[:, :, None, :]

    outs = None
    for i in range(DEPTH):
        j = i // 2
        if i % 2 == 0:
            x = _attention_block(x, views, attn_w_qkv[j], attn_w_o[j], attn_b_o[j], ln_mix_g[i], ln_mix_b[i],
                                 seq_rows)
        else:
            x = _sgu_block(x, sgu_w_in[j].astype(BF16), sgu_b_in[j][None], sgu_norm_g[j][None], sgu_norm_b[j][None],
                           sgu_w_s[j].astype(BF16), jnp.transpose(sgu_b_s[j]), sgu_w_out[j].astype(BF16),
                           sgu_b_out[j][None], ln_mix_g[i][None], ln_mix_b[i][None])
        y, dest, gate = _moe_block(x, i, router_w[i], router_b[i][None], exp_w1, b1g, b1l, exp_w2, b2)
        g_ln, b_ln = ln_ffn_g[i][None], ln_ffn_b[i][None]
        if i + 1 < DEPTH:
            x = _combine(y, dest, gate, x, g_ln, b_ln, 0, tp + ts)
        else:
            outs = (_combine(y, dest, gate, x, g_ln, b_ln, 0, tp).reshape(x_prompt.shape),
                    _combine(y, dest, gate, x, g_ln, b_ln, tp, ts).reshape(x_sample.shape))
    return outs
```

```python
import functools

import jax
import jax.numpy as jnp
from jax import lax
from jax.experimental import pallas as pl
from jax.experimental.pallas import tpu as pltpu

F32 = jnp.float32
BF16 = jnp.bfloat16

LANES = 128
SUBLANES = 8
VMEM_LIMIT_CAP = 56 * 1024 * 1024

D_MODEL = 1024
DIL_CONFIGS = ((128, 1), (512, 4), (2048, 16))
N_GROUPS = len(DIL_CONFIGS)
HEADS = 8
HEAD_DIM = 64
GROUP_DIM = HEADS * HEAD_DIM
N_SIDE = 64
Q_BLOCK = 128
KV_BLOCK = N_SIDE
KV_WIN = Q_BLOCK + 2 * N_SIDE
ATTN_Q_STEPS = (8, 4, 2, 1)
SGU_CHUNK = 128
SGU_HALF = 3 * D_MODEL
SGU_GROUPS = 8
SGU_GROUP_DIM = SGU_HALF // SGU_GROUPS
N_EXPERTS = 32
TOP_K = 4
D_EXPERT = D_MODEL
SWIGLU_ALPHA = 1.702
SWIGLU_LIMIT = 7.0
LN_EPS = 1e-5
DEPTH = 2
DEEPNORM_ALPHA = (2 * DEPTH) ** 0.25
NEG = -1e30

ROW_TILE = 512
QKV_ROWS = 1024
RANK_CHUNK = 256
SGU_ROWS = 512
EXPERT_ROWS = 512
DMA_UNROLL = 8
DMA_QUEUES = 2


def _cparams(semantics, vmem_bytes):
    return pltpu.CompilerParams(dimension_semantics=semantics,
                                vmem_limit_bytes=min(int(vmem_bytes), VMEM_LIMIT_CAP))


def _layer_norm(y, g, b):
    mu = jnp.mean(y, axis=-1, keepdims=True)
    yc = y - mu
    var = jnp.mean(yc * yc, axis=-1, keepdims=True)
    return yc * lax.rsqrt(var + LN_EPS) * g + b


def _views_kernel(xa_ref, xb_ref, *refs, na, dils):
    x_ref, view_refs, slab = refs[0], refs[1:-1], refs[-1]
    tm, d = x_ref.shape
    x = jnp.where(pl.program_id(0) < na, xa_ref[...], xb_ref[...])
    x_ref[...] = x
    for c in range(d // LANES):
        slab[c] = x[:, c * LANES:(c + 1) * LANES]
    for dil, v_ref in zip(dils, view_refs):
        for r in range(dil):
            for c in range(d // LANES):
                v_ref[:, r * d + c * LANES:r * d + (c + 1) * LANES] = slab[c, pl.ds(r, tm // dil, stride=dil), :]


def _token_views(xa, xb, dils):
    ta, d = xa.shape
    tb = xb.shape[0]
    t = ta + tb
    tm = ROW_TILE
    na, nb = ta // tm, tb // tm
    out_shapes = [jax.ShapeDtypeStruct((t, d), F32)] + [jax.ShapeDtypeStruct((t // dil, dil * d), F32) for dil in dils]
    out_specs = [pl.BlockSpec((tm, d), lambda i: (i, 0))] + [pl.BlockSpec((tm // dil, dil * d), lambda i: (i, 0))
                                                             for dil in dils]
    return pl.pallas_call(
        functools.partial(_views_kernel, na=na, dils=dils),
        grid=(na + nb,),
        in_specs=[pl.BlockSpec((tm, d), lambda i: (jnp.minimum(i, na - 1), 0)),
                  pl.BlockSpec((tm, d), lambda i: (jnp.maximum(i - na, 0), 0))],
        out_specs=out_specs,
        out_shape=out_shapes,
        scratch_shapes=[pltpu.VMEM((d // LANES, tm, LANES), F32)],
        compiler_params=_cparams(("arbitrary",), 40 << 20),
        name="token_views",
    )(xa, xb)


def _qkv_kernel(x_ref, w_ref, o_ref):
    o_ref[...] = jnp.dot(x_ref[...].astype(BF16), w_ref[...],
                         preferred_element_type=F32).astype(BF16)


def _qkv_project(xv, w, dil):
    n = xv.shape[0]
    d = xv.shape[1] // dil
    tm = min(QKV_ROWS, n)
    width = w.shape[1]
    return pl.pallas_call(
        _qkv_kernel,
        grid=(dil, n // tm),
        in_specs=[pl.BlockSpec((tm, d), lambda r, i: (i, r)),
                  pl.BlockSpec((d, width), lambda r, i: (0, 0))],
        out_specs=pl.BlockSpec((None, tm, width), lambda r, i: (r, i, 0)),
        out_shape=jax.ShapeDtypeStruct((dil, n, width), BF16),
        compiler_params=_cparams(("parallel", "parallel"),
                                 2 * (tm * d * 4 + d * width * 2 + tm * width * 2) + tm * width * 4
                                 + (8 << 20)),
        name=f"qkv_project_d{dil}",
    )(xv, w)


def _attn_kernel(q_ref, kp_ref, kb_ref, kn_ref, vp_ref, vb_ref, vn_ref, bias_ref, o_ref, lse_ref, *, segs):
    (n_a, len_a), (_, len_b) = segs
    step_rows = q_ref.shape[0]
    lane = lax.broadcasted_iota(jnp.int32, (1, LANES), 1)
    first_head = lane < HEAD_DIM
    nt = (((1,), (1,)), ((), ()))

    def window(prev, body, nxt, first, sl):
        parts = []
        if first < 0:
            parts.append(prev[:, sl])
        parts.append(body[max(first, 0):min(first + KV_WIN, step_rows), sl])
        if first + KV_WIN > step_rows:
            parts.append(nxt[:, sl])
        return jnp.concatenate(parts, axis=0)

    for sub in range(step_rows // Q_BLOCK):
        rows = slice(sub * Q_BLOCK, (sub + 1) * Q_BLOCK)
        row0 = pl.program_id(1) * step_rows + sub * Q_BLOCK
        in_a = row0 < n_a
        lo = jnp.where(in_a, (row0 // len_a) * len_a, n_a + ((row0 - n_a) // len_b) * len_b)
        hi = lo + jnp.where(in_a, len_a, len_b)
        krow = lax.broadcasted_iota(jnp.int32, (1, KV_WIN), 1) + (row0 - N_SIDE)
        col_bias = jnp.where((krow >= lo) & (krow < hi), 0.0, NEG).astype(F32)
        lse_tile = jnp.zeros((Q_BLOCK, LANES), F32)
        for p in range(HEADS // 2):
            sl = slice(p * LANES, (p + 1) * LANES)
            q = q_ref[rows, sl]
            k = window(kp_ref, kb_ref, kn_ref, sub * Q_BLOCK - N_SIDE, sl)
            v = window(vp_ref, vb_ref, vn_ref, sub * Q_BLOCK - N_SIDE, sl)
            zero = jnp.zeros_like(q)
            outs, inv_ls = [], []
            for j, qh in enumerate((jnp.where(first_head, q, zero), jnp.where(first_head, zero, q))):
                h = 2 * p + j
                s = lax.dot_general(qh, k, nt, preferred_element_type=F32)
                s = s + bias_ref[h] + col_bias
                m = jnp.max(s, axis=-1, keepdims=True)
                e = jnp.exp(s - m)
                l = jnp.sum(e, axis=-1, keepdims=True)
                outs.append(jnp.dot(e.astype(BF16), v, preferred_element_type=F32))
                inv_ls.append(1.0 / l)
                lse_tile = jnp.where(lane == h, m + jnp.log(l), lse_tile)
            o = jnp.where(first_head, outs[0] * inv_ls[0], outs[1] * inv_ls[1])
            o_ref[rows, sl] = o.astype(BF16)
        lse_ref[rows, :] = lse_tile


def _band_bias(slopes, dil):
    qi = jnp.arange(Q_BLOCK)[:, None]
    kc = jnp.arange(KV_WIN)[None, :]
    dist = jnp.abs(kc - N_SIDE - qi)
    pen = -slopes[:, None, None] * (dist * dil).astype(F32)[None]
    return jnp.where((dist <= N_SIDE)[None], pen, NEG).astype(F32)


def _attention(qkv, bias, dil, segs):
    _, n, _ = qkv.shape
    nkb = n // KV_BLOCK
    q_step = max(s for s in ATTN_Q_STEPS if (n // Q_BLOCK) % s == 0)
    step_rows = q_step * Q_BLOCK
    per_step = step_rows // KV_BLOCK

    def body_spec(col):
        return pl.BlockSpec((None, step_rows, GROUP_DIM), lambda r, i: (r, i, col))

    def edge_spec(col, after):
        def imap(r, i):
            return (r, jnp.clip((i + after) * per_step - 1 + after, 0, nkb - 1), col)
        return pl.BlockSpec((None, KV_BLOCK, GROUP_DIM), imap)

    in_specs = ([body_spec(0)]
                + [edge_spec(1, 0), body_spec(1), edge_spec(1, 1)]
                + [edge_spec(2, 0), body_spec(2), edge_spec(2, 1)]
                + [pl.BlockSpec((HEADS, Q_BLOCK, KV_WIN), lambda r, i: (0, 0, 0))])
    return pl.pallas_call(
        functools.partial(_attn_kernel, segs=segs),
        grid=(dil, n // step_rows),
        in_specs=in_specs,
        out_specs=[pl.BlockSpec((step_rows, GROUP_DIM), lambda r, i: (i, r)),
                   pl.BlockSpec((step_rows, LANES), lambda r, i: (i, r))],
        out_shape=[jax.ShapeDtypeStruct((n, dil * GROUP_DIM), BF16),
                   jax.ShapeDtypeStruct((n, dil * LANES), F32)],
        compiler_params=_cparams(("parallel", "parallel"), 32 << 20),
        name=f"band_attention_d{dil}",
    )(*([qkv] * 7), bias)


def _merge_kernel(o0, o1, o2, l0, l1, l2, x_ref, wo_ref, bo_ref, g_ref, b_ref, y_ref):
    lses = [l0[...], l1[...], l2[...]]
    m = jnp.maximum(jnp.maximum(lses[0], lses[1]), lses[2])
    es = [jnp.exp(l - m) for l in lses]
    inv = 1.0 / (es[0] + es[1] + es[2])
    ws = [e * inv for e in es]
    lane = lax.broadcasted_iota(jnp.int32, (1, LANES), 1)
    first_head = lane < HEAD_DIM
    parts = []
    for p in range(HEADS // 2):
        sl = slice(p * LANES, (p + 1) * LANES)
        acc = None
        for w, o in zip(ws, (o0, o1, o2)):
            wp = jnp.where(first_head, w[:, 2 * p:2 * p + 1], w[:, 2 * p + 1:2 * p + 2])
            term = wp * o[:, sl].astype(F32)
            acc = term if acc is None else acc + term
        parts.append(acc)
    merged = jnp.concatenate(parts, axis=-1).astype(BF16)
    h = jnp.dot(merged, wo_ref[...], preferred_element_type=F32) + bo_ref[...]
    y_ref[...] = _layer_norm(DEEPNORM_ALPHA * x_ref[...] + h, g_ref[...], b_ref[...])


def _merge_project(os_, lses, x, w_o, b_o, ln_g, ln_b):
    t, d = x.shape
    tm = ROW_TILE
    row = lambda i: (i, 0)
    const = lambda i: (0, 0)
    vec = pl.BlockSpec((1, d), const)
    return pl.pallas_call(
        _merge_kernel,
        grid=(t // tm,),
        in_specs=([pl.BlockSpec((tm, GROUP_DIM), row)] * 3 + [pl.BlockSpec((tm, LANES), row)] * 3
                  + [pl.BlockSpec((tm, d), row), pl.BlockSpec((GROUP_DIM, d), const), vec, vec, vec]),
        out_specs=pl.BlockSpec((tm, d), row),
        out_shape=jax.ShapeDtypeStruct((t, d), F32),
        compiler_params=_cparams(("parallel",), 32 << 20),
        name="merge_outproj_ln",
    )(*os_, *lses, x, w_o, b_o, ln_g, ln_b)


def _split_bf16(a):
    hi = a.astype(BF16)
    return hi, (a - hi.astype(F32)).astype(BF16)


def _router_kernel(x_ref, w_ref, b_ref, e_ref, g_ref, r_ref, cnt_ref, carry_ref):
    tm = x_ref.shape[0]

    @pl.when(pl.program_id(0) == 0)
    def _():
        carry_ref[...] = jnp.zeros_like(carry_ref)

    xh, xl = _split_bf16(x_ref[...])
    wh, wl = _split_bf16(w_ref[...])
    dot = functools.partial(jnp.dot, preferred_element_type=F32)
    logits = dot(xh, wh) + (dot(xl, wh) + dot(xh, wl)) + b_ref[...]
    lane = lax.broadcasted_iota(jnp.int32, logits.shape, 1).astype(F32)
    col4 = lax.broadcasted_iota(jnp.int32, (tm, TOP_K), 1)
    work = logits
    picks, vals, idxs = [], [], []
    for _k in range(TOP_K):
        mx = jnp.max(work, axis=-1, keepdims=True)
        idx = jnp.min(jnp.where(work == mx, lane, float(N_EXPERTS)), axis=-1, keepdims=True)
        pick = lane == idx
        work = jnp.where(pick, -jnp.inf, work)
        picks.append(pick)
        vals.append(mx)
        idxs.append(idx)
    ex = [jnp.exp(v - vals[0]) for v in vals]
    inv = 1.0 / (ex[0] + ex[1] + ex[2] + ex[3])
    chosen = jnp.zeros(logits.shape, F32)
    for pick in picks:
        chosen = chosen + pick.astype(F32)
    ch = min(RANK_CHUNK, tm)
    r_i = lax.broadcasted_iota(jnp.int32, (ch, ch), 0)
    c_i = lax.broadcasted_iota(jnp.int32, (ch, ch), 1)
    tri = (c_i < r_i).astype(BF16)
    carry = carry_ref[...]
    before = []
    for c in range(tm // ch):
        part = chosen[c * ch:(c + 1) * ch]
        before.append(dot(tri, part.astype(BF16)) + carry)
        carry = carry + jnp.sum(part, axis=0, keepdims=True)
    before = jnp.concatenate(before, axis=0)
    e_out = jnp.zeros((tm, TOP_K), jnp.int32)
    g_out = jnp.zeros((tm, TOP_K), F32)
    r_out = jnp.zeros((tm, TOP_K), jnp.int32)
    for k in range(TOP_K):
        rank = jnp.sum(jnp.where(picks[k], before, 0.0), axis=-1, keepdims=True)
        e_out = jnp.where(col4 == k, idxs[k].astype(jnp.int32), e_out)
        g_out = jnp.where(col4 == k, ex[k] * inv, g_out)
        r_out = jnp.where(col4 == k, rank.astype(jnp.int32), r_out)
    e_ref[...] = e_out
    g_ref[...] = g_out
    r_ref[...] = r_out
    carry_ref[...] = carry
    cnt_ref[...] = carry


def _router(x, w, b):
    t, d = x.shape
    tm = ROW_TILE
    row = lambda i: (i, 0)
    const = lambda i: (0, 0)
    k_spec = pl.BlockSpec((tm, TOP_K), row)
    return pl.pallas_call(
        _router_kernel,
        grid=(t // tm,),
        in_specs=[pl.BlockSpec((tm, d), row), pl.BlockSpec((d, N_EXPERTS), const),
                  pl.BlockSpec((1, N_EXPERTS), const)],
        out_specs=[k_spec, k_spec, k_spec, pl.BlockSpec((1, N_EXPERTS), const)],
        out_shape=[jax.ShapeDtypeStruct((t, TOP_K), jnp.int32), jax.ShapeDtypeStruct((t, TOP_K), F32),
                   jax.ShapeDtypeStruct((t, TOP_K), jnp.int32), jax.ShapeDtypeStruct((1, N_EXPERTS), F32)],
        scratch_shapes=[pltpu.VMEM((1, N_EXPERTS), F32)],
        compiler_params=_cparams(("arbitrary",), 32 << 20),
        name="router_topk",
    )(x, w, b)


def _route_plan(e, rank, counts, bm, tm):
    t = e.shape[0]
    nb = TOP_K * t // bm + N_EXPERTS
    counts = counts.reshape(N_EXPERTS).astype(jnp.int32)
    padded = (counts + bm - 1) // bm * bm
    pend = jnp.cumsum(padded)
    pstart = pend - padded
    onehot = e[..., None] == jnp.arange(N_EXPERTS, dtype=jnp.int32)
    dest = jnp.sum(jnp.where(onehot, pstart, 0), axis=-1) + rank
    blk_row0 = jnp.arange(nb, dtype=jnp.int32) * bm
    blk_exp = jnp.minimum(jnp.sum((pend[None, :] <= blk_row0[:, None]).astype(jnp.int32), axis=1),
                          N_EXPERTS - 1)
    n_used = pend[-1:] // bm
    return dest.reshape(t // tm, tm * TOP_K), blk_exp, n_used, pstart + counts, pend, padded // bm


ROW_SUBLANES = D_MODEL // LANES


def _store_row_tiles(ref, base, x):
    for c in range(ROW_SUBLANES):
        ref[pl.ds(base + c, x.shape[0], stride=ROW_SUBLANES), :] = x[:, c * LANES:(c + 1) * LANES]


def _load_row_tiles(ref, base, n):
    return jnp.concatenate([ref[pl.ds(base + c, n, stride=ROW_SUBLANES), :] for c in range(ROW_SUBLANES)], axis=1)


def _tile_rows(ref, row):
    return ref.at[pl.ds(pl.multiple_of(row * ROW_SUBLANES, ROW_SUBLANES), ROW_SUBLANES), :]


def _dispatch_kernel(padlo_ref, pend_ref, nused_ref, dest_hbm, x_ref, xs_hbm, idx0, idx1, pk, zbuf, sems):
    i = pl.program_id(0)
    n = pl.num_programs(0)
    tm = x_ref.shape[0]
    bm = zbuf.shape[0] // ROW_SUBLANES
    dump = xs_hbm.shape[0] // ROW_SUBLANES - bm
    idx = (idx0, idx1)
    slot_rows = tm * ROW_SUBLANES

    def idx_copy(step, s):
        return pltpu.make_async_copy(dest_hbm.at[step], idx[s], sems.at[s])

    def rows_wait(s):
        for _ in range(TOP_K):
            pltpu.make_async_copy(pk.at[pl.ds(s * slot_rows, slot_rows), :], xs_hbm.at[pl.ds(0, slot_rows), :],
                                  sems.at[2 + s]).wait()

    @pl.when(i == 0)
    def _():
        idx_copy(0, 0).start()
        zbuf[...] = jnp.zeros_like(zbuf)

        def zero_block(b):
            return pltpu.make_async_copy(zbuf, xs_hbm.at[pl.ds(pl.multiple_of(b * zbuf.shape[0], ROW_SUBLANES),
                                                               zbuf.shape[0]), :], sems.at[4])

        def zero_row(e, j):
            return pltpu.make_async_copy(_tile_rows(zbuf, j), _tile_rows(xs_hbm, padlo_ref[e] + j), sems.at[4])

        def for_all_zero_copies(act):
            def blk(b, c):
                act(zero_block(b))
                return c
            lax.fori_loop(nused_ref[0], dump // bm + 1, blk, 0)
            for e in range(N_EXPERTS):
                def row(j, c, e=e):
                    act(zero_row(e, j))
                    return c
                lax.fori_loop(0, pend_ref[e] - padlo_ref[e], row, 0)

        for_all_zero_copies(lambda cp: cp.start())
        for_all_zero_copies(lambda cp: cp.wait())

    def step(s):
        idx_copy(i, s).wait()

        @pl.when(i + 1 < n)
        def _():
            idx_copy(i + 1, 1 - s).start()

        @pl.when(i >= 2)
        def _():
            rows_wait(s)

        _store_row_tiles(pk, s * slot_rows, x_ref[...])

        def issue(j, c):
            for k in range(TOP_K):
                pltpu.make_async_copy(_tile_rows(pk, s * tm + j), _tile_rows(xs_hbm, idx[s][j * TOP_K + k]),
                                      sems.at[2 + s]).start(priority=k % DMA_QUEUES)
            return c
        lax.fori_loop(0, tm, issue, 0, unroll=DMA_UNROLL)

        @pl.when(i == n - 1)
        def _():
            rows_wait(s)

        @pl.when(jnp.logical_and(i == n - 1, n >= 2))
        def _():
            rows_wait(1 - s)

    for s in range(2):
        pl.when(i % 2 == s)(functools.partial(step, s))


def _dispatch(x, dest, pad_lo, pend, n_used, bm):
    t, d = x.shape
    nblk, per = dest.shape
    tm = per // TOP_K
    nb = TOP_K * t // bm + N_EXPERTS
    grid_spec = pltpu.PrefetchScalarGridSpec(
        num_scalar_prefetch=3,
        grid=(nblk,),
        in_specs=[pl.BlockSpec(memory_space=pl.ANY), pl.BlockSpec((tm, d), lambda i, a, b, c: (i, 0))],
        out_specs=pl.BlockSpec(memory_space=pl.ANY),
        scratch_shapes=[pltpu.SMEM((per,), jnp.int32), pltpu.SMEM((per,), jnp.int32),
                        pltpu.VMEM((2 * tm * ROW_SUBLANES, LANES), F32),
                        pltpu.VMEM((bm * ROW_SUBLANES, LANES), F32),
                        pltpu.SemaphoreType.DMA((5,))])
    return pl.pallas_call(
        _dispatch_kernel,
        grid_spec=grid_spec,
        out_shape=jax.ShapeDtypeStruct(((nb * bm + bm) * ROW_SUBLANES, LANES), F32),
        compiler_params=_cparams(("arbitrary",), 32 << 20),
        name="moe_dispatch",
    )(pad_lo, pend, n_used, dest, x)


def _expert_kernel(be_ref, nused_ref, gblk_ref, xs_ref, w1_hbm, b1g_ref, b1l_ref, w2_hbm, b2_ref, y_ref,
                   w1gt, w1lt, w2b, tbuf, w1s, w2s, slot_ref, sems, *, layer):
    i = pl.program_id(0)
    d, de = w2b.shape[1], w2b.shape[0]
    nt = (((1,), (1,)), ((), ()))

    def weight_copies(e, s):
        return (pltpu.make_async_copy(w1_hbm.at[layer, e], w1s.at[s], sems.at[0, s]),
                pltpu.make_async_copy(w2_hbm.at[layer, e], w2s.at[s], sems.at[1, s]))

    @pl.when(i < nused_ref[0])
    def _():
        e = be_ref[i]
        changed = jnp.logical_or(i == 0, e != be_ref[jnp.maximum(i - 1, 0)])

        @pl.when(changed)
        def _():
            @pl.when(i == 0)
            def _():
                slot_ref[0] = 0
                for cp in weight_copies(e, 0):
                    cp.start()

            s = slot_ref[0]
            for cp in weight_copies(e, s):
                cp.wait()
            j = i + gblk_ref[e]

            @pl.when(j < nused_ref[0])
            def _():
                for cp in weight_copies(be_ref[jnp.minimum(j, be_ref.shape[0] - 1)], 1 - s):
                    cp.start()

            for a in range(d // LANES):
                cols = slice(a * LANES, (a + 1) * LANES)
                tbuf[...] = w1s[s, cols, :].T
                w1gt[:, cols] = tbuf[pl.ds(0, de, stride=2), :].astype(BF16)
                w1lt[:, cols] = tbuf[pl.ds(1, de, stride=2), :].astype(BF16)
            w2b[...] = w2s[s].astype(BF16)
            slot_ref[0] = 1 - s

        bm = xs_ref.shape[0] // ROW_SUBLANES
        x = _load_row_tiles(xs_ref, 0, bm).astype(BF16)
        hg = lax.dot_general(x, w1gt[...], nt, preferred_element_type=F32) + b1g_ref[...]
        hl = lax.dot_general(x, w1lt[...], nt, preferred_element_type=F32) + b1l_ref[...]
        hg = jnp.minimum(hg, SWIGLU_LIMIT)
        hl = jnp.clip(hl, -SWIGLU_LIMIT, SWIGLU_LIMIT)
        act = hg * (1.0 / (1.0 + jnp.exp(-SWIGLU_ALPHA * hg))) * (hl + 1.0)
        y = jnp.dot(act.astype(BF16), w2b[...], preferred_element_type=F32) + b2_ref[...]
        _store_row_tiles(y_ref, 0, y)

    @pl.when(i >= nused_ref[0])
    def _():
        y_ref[...] = jnp.zeros_like(y_ref)


def _experts(xs, blk_exp, n_used, grp_blocks, layer, w1, b1g, b1l, w2, b2, bm):
    d = ROW_SUBLANES * LANES
    nb = blk_exp.shape[0]
    de = w2.shape[2]
    blk = (bm * ROW_SUBLANES, LANES)
    bmap = lambda i, be, nu, gb: (layer, be[i], 0, 0)
    any_spec = pl.BlockSpec(memory_space=pl.ANY)
    grid_spec = pltpu.PrefetchScalarGridSpec(
        num_scalar_prefetch=3,
        grid=(nb,),
        in_specs=[pl.BlockSpec(blk, lambda i, be, nu, gb: (jnp.minimum(i, nu[0] - 1), 0)),
                  any_spec,
                  pl.BlockSpec((None, None, 1, de), bmap), pl.BlockSpec((None, None, 1, de), bmap),
                  any_spec,
                  pl.BlockSpec((None, None, 1, d), bmap)],
        out_specs=pl.BlockSpec(blk, lambda i, be, nu, gb: (i, 0)),
        scratch_shapes=[pltpu.VMEM((de, d), BF16), pltpu.VMEM((de, d), BF16), pltpu.VMEM((de, d), BF16),
                        pltpu.VMEM((2 * de, LANES), F32),
                        pltpu.VMEM((2, d, 2 * de), F32), pltpu.VMEM((2, de, d), F32),
                        pltpu.SMEM((1,), jnp.int32), pltpu.SemaphoreType.DMA((2, 2))])
    return pl.pallas_call(
        functools.partial(_expert_kernel, layer=layer),
        grid_spec=grid_spec,
        out_shape=jax.ShapeDtypeStruct((nb * blk[0], LANES), F32),
        compiler_params=_cparams(("arbitrary",), 54 << 20),
        name="moe_experts",
    )(blk_exp, n_used, grp_blocks, xs, w1, b1g, b1l, w2, b2)


def _combine_kernel(dest_hbm, gate_ref, x_ref, g_ref, b_ref, y_hbm, o_ref, idx0, idx1, ybuf, sems, *, blk_off):
    i = pl.program_id(0)
    n = pl.num_programs(0)
    tm = x_ref.shape[0]
    idx = (idx0, idx1)
    slab_rows = tm * ROW_SUBLANES

    def slab(s, k):
        return (s * TOP_K + k) * tm

    def idx_copy(step, s):
        return pltpu.make_async_copy(dest_hbm.at[blk_off + step], idx[s], sems.at[s])

    def issue_rows(s):
        def body(j, c):
            for k in range(TOP_K):
                pltpu.make_async_copy(_tile_rows(y_hbm, idx[s][j * TOP_K + k]), _tile_rows(ybuf, slab(s, k) + j),
                                      sems.at[2 + s]).start(priority=k % DMA_QUEUES)
            return c
        lax.fori_loop(0, tm, body, 0, unroll=DMA_UNROLL)

    @pl.when(i == 0)
    def _():
        cp = idx_copy(0, 0)
        cp.start()
        cp.wait()
        issue_rows(0)

    @pl.when(jnp.logical_and(i == 0, n >= 2))
    def _():
        idx_copy(1, 1).start()

    def step(s):
        @pl.when(i + 1 < n)
        def _():
            idx_copy(i + 1, 1 - s).wait()
            issue_rows(1 - s)

        @pl.when(i + 2 < n)
        def _():
            idx_copy(i + 2, s).start()

        for k in range(TOP_K):
            pltpu.make_async_copy(y_hbm.at[pl.ds(0, slab_rows), :],
                                  ybuf.at[pl.ds(slab(s, k) * ROW_SUBLANES, slab_rows), :], sems.at[2 + s]).wait()
        gate = gate_ref[...]
        h = None
        for k in range(TOP_K):
            term = gate[:, k:k + 1] * _load_row_tiles(ybuf, slab(s, k) * ROW_SUBLANES, tm)
            h = term if h is None else h + term
        o_ref[...] = _layer_norm(DEEPNORM_ALPHA * x_ref[...] + h, g_ref[...], b_ref[...])

    for s in range(2):
        pl.when(i % 2 == s)(functools.partial(step, s))


def _combine(y, dest, gate, x, ln_g, ln_b, row_lo, rows):
    _, d = x.shape
    per = dest.shape[1]
    tm = per // TOP_K
    off = row_lo // tm
    vec = pl.BlockSpec((1, d), lambda i: (0, 0))
    return pl.pallas_call(
        functools.partial(_combine_kernel, blk_off=off),
        grid=(rows // tm,),
        in_specs=[pl.BlockSpec(memory_space=pl.ANY),
                  pl.BlockSpec((tm, TOP_K), lambda i: (off + i, 0)),
                  pl.BlockSpec((tm, d), lambda i: (off + i, 0)), vec, vec,
                  pl.BlockSpec(memory_space=pl.ANY)],
        out_specs=pl.BlockSpec((tm, d), lambda i: (i, 0)),
        out_shape=jax.ShapeDtypeStruct((rows, d), F32),
        scratch_shapes=[pltpu.SMEM((per,), jnp.int32), pltpu.SMEM((per,), jnp.int32),
                        pltpu.VMEM((2 * TOP_K * tm * ROW_SUBLANES, LANES), F32),
                        pltpu.SemaphoreType.DMA((4,))],
        compiler_params=_cparams(("arbitrary",), 40 << 20),
        name="moe_combine_ln",
    )(dest, gate, x, ln_g, ln_b, y)


def _moe_block(x, layer, router_w, router_b, w1, b1g, b1l, w2, b2):
    e, gate, rank, counts = _router(x, router_w, router_b)
    dest, blk_exp, n_used, pad_lo, pend, grp_blocks = _route_plan(e, rank, counts, EXPERT_ROWS, ROW_TILE)
    xs = _dispatch(x, dest, pad_lo, pend, n_used, EXPERT_ROWS)
    y = _experts(xs, blk_exp, n_used, grp_blocks, layer, w1, b1g, b1l, w2, b2, EXPERT_ROWS)
    return y, dest, gate


def _gelu(z):
    return 0.5 * z * (1.0 + jnp.tanh(0.7978845608028654 * (z + 0.044715 * (z * z * z))))


def _sgu_kernel(x_ref, win_ref, bin_ref, ng_ref, nb_ref, ws_ref, bs_ref, wout_ref, bout_ref,
                g_ref, b_ref, y_ref):
    tm = x_ref.shape[0]
    x = x_ref[...]
    xb = x.astype(BF16)
    dot = functools.partial(jnp.dot, preferred_element_type=F32)
    v = _gelu(dot(xb, win_ref[:, SGU_HALF:]) + bin_ref[:, SGU_HALF:])
    v = _layer_norm(v, ng_ref[...], nb_ref[...]).astype(BF16)
    acc = jnp.zeros((tm, D_MODEL), F32)
    for g in range(SGU_GROUPS):
        cols = slice(g * SGU_GROUP_DIM, (g + 1) * SGU_GROUP_DIM)
        u = _gelu(dot(xb, win_ref[:, cols]) + bin_ref[:, cols])
        mixed = [dot(ws_ref[g], v[c * SGU_CHUNK:(c + 1) * SGU_CHUNK, cols]) + bs_ref[:, g:g + 1]
                 for c in range(tm // SGU_CHUNK)]
        mixed = jnp.concatenate(mixed, axis=0)
        acc = acc + dot((u * mixed).astype(BF16), wout_ref[cols, :])
    h = acc + bout_ref[...]
    y_ref[...] = _layer_norm(DEEPNORM_ALPHA * x + h, g_ref[...], b_ref[...])


def _sgu_block(x, w_in, b_in, norm_g, norm_b, w_s, b_s_t, w_out, b_out, ln_g, ln_b):
    t, d = x.shape
    tm = SGU_ROWS
    row = lambda i: (i, 0)
    c2 = lambda i: (0, 0)
    c3 = lambda i: (0, 0, 0)
    once = dict(pipeline_mode=pl.Buffered(1))
    vec = pl.BlockSpec((1, d), c2)
    return pl.pallas_call(
        _sgu_kernel,
        grid=(t // tm,),
        in_specs=[pl.BlockSpec((tm, d), row),
                  pl.BlockSpec((d, 2 * SGU_HALF), c2, **once), pl.BlockSpec((1, 2 * SGU_HALF), c2),
                  pl.BlockSpec((1, SGU_HALF), c2), pl.BlockSpec((1, SGU_HALF), c2),
                  pl.BlockSpec((SGU_GROUPS, SGU_CHUNK, SGU_CHUNK), c3),
                  pl.BlockSpec((SGU_CHUNK, SGU_GROUPS), c2),
                  pl.BlockSpec((SGU_HALF, d), c2, **once), vec, vec, vec],
        out_specs=pl.BlockSpec((tm, d), row),
        out_shape=jax.ShapeDtypeStruct((t, d), F32),
        compiler_params=_cparams(("parallel",), 52 << 20),
        name="sgu_block",
    )(x, w_in, b_in, norm_g, norm_b, w_s, b_s_t, w_out, b_out, ln_g, ln_b)


def _attention_block(x, views, w_qkv, w_o, b_o, ln_g, ln_b, seq_rows):
    n_groups = N_GROUPS * HEADS
    slopes = (2.0 ** (-8.0 * jnp.arange(1, n_groups + 1, dtype=F32) / n_groups)).reshape(N_GROUPS, HEADS)
    scale = jnp.concatenate([jnp.full((GROUP_DIM,), HEAD_DIM ** -0.5, F32), jnp.ones((2 * GROUP_DIM,), F32)])
    os_, lses = [], []
    for g, (_, dil) in enumerate(DIL_CONFIGS):
        cols = [w_qkv[:, (j * N_GROUPS + g) * GROUP_DIM:(j * N_GROUPS + g + 1) * GROUP_DIM] for j in range(3)]
        w_g = (jnp.concatenate(cols, axis=1) * scale).astype(BF16)
        qkv = _qkv_project(views[dil], w_g, dil)
        segs = tuple((n // dil, length // dil) for n, length in seq_rows)
        o, lse = _attention(qkv, _band_bias(slopes[g], dil), dil, segs)
        t = x.shape[0]
        os_.append(o.reshape(t, GROUP_DIM))
        lses.append(lse.reshape(t, LANES))
    return _merge_project(os_, lses, x, w_o.astype(BF16), b_o[None], ln_g[None], ln_b[None])


def kernel(x_prompt, x_sample, attn_w_qkv, attn_w_o, attn_b_o, sgu_w_in, sgu_b_in, sgu_norm_g, sgu_norm_b,
           sgu_w_s, sgu_b_s, sgu_w_out, sgu_b_out, router_w, router_b, exp_w1, exp_b1, exp_w2, exp_b2,
           ln_mix_g, ln_mix_b, ln_ffn_g, ln_ffn_b):
    d = x_prompt.shape[-1]
    tp = x_prompt.shape[0] * x_prompt.shape[1]
    ts = x_sample.shape[0] * x_sample.shape[1]
    seq_rows = ((tp, x_prompt.shape[1]), (ts, x_sample.shape[1]))
    dils = tuple(dil for _, dil in DIL_CONFIGS if dil > 1)
    x, *xviews = _token_views(x_prompt.reshape(tp, d), x_sample.reshape(ts, d), dils)
    views = dict(zip(dils, xviews))
    views[1] = x

    b1g = exp_b1[:, :, None, 0::2]
    b1l = exp_b1[:, :, None, 1::2]
    b2 = exp_b2[:, :, None, :]

    outs = None
    for i in range(DEPTH):
        j = i // 2
        if i % 2 == 0:
            x = _attention_block(x, views, attn_w_qkv[j], attn_w_o[j], attn_b_o[j], ln_mix_g[i], ln_mix_b[i],
                                 seq_rows)
        else:
            x = _sgu_block(x, sgu_w_in[j].astype(BF16), sgu_b_in[j][None], sgu_norm_g[j][None], sgu_norm_b[j][None],
                           sgu_w_s[j].astype(BF16), jnp.transpose(sgu_b_s[j]), sgu_w_out[j].astype(BF16),
                           sgu_b_out[j][None], ln_mix_g[i][None], ln_mix_b[i][None])
        y, dest, gate = _moe_block(x, i, router_w[i], router_b[i][None], exp_w1, b1g, b1l, exp_w2, b2)
        g_ln, b_ln = ln_ffn_g[i][None], ln_ffn_b[i][None]
        if i + 1 < DEPTH:
            x = _combine(y, dest, gate, x, g_ln, b_ln, 0, tp + ts)
        else:
            outs = (_combine(y, dest, gate, x, g_ln, b_ln, 0, tp).reshape(x_prompt.shape),
                    _combine(y, dest, gate, x, g_ln, b_ln, tp, ts).reshape(x_sample.shape))
    return outs
```

```python
import functools

import jax
import jax.numpy as jnp
from jax import lax
from jax.experimental import pallas as pl
from jax.experimental.pallas import tpu as pltpu

F32 = jnp.float32
BF16 = jnp.bfloat16

LANES = 128
SUBLANES = 8
VMEM_LIMIT_CAP = 56 * 1024 * 1024

D_MODEL = 1024
DIL_CONFIGS = ((128, 1), (512, 4), (2048, 16))
N_GROUPS = len(DIL_CONFIGS)
HEADS = 8
HEAD_DIM = 64
GROUP_DIM = HEADS * HEAD_DIM
N_SIDE = 64
Q_BLOCK = 128
KV_BLOCK = N_SIDE
KV_WIN = Q_BLOCK + 2 * N_SIDE
ATTN_Q_STEPS = (8, 4, 2, 1)
SGU_CHUNK = 128
SGU_HALF = 3 * D_MODEL
SGU_GROUPS = 8
SGU_GROUP_DIM = SGU_HALF // SGU_GROUPS
N_EXPERTS = 32
TOP_K = 4
D_EXPERT = D_MODEL
SWIGLU_ALPHA = 1.702
SWIGLU_LIMIT = 7.0
LN_EPS = 1e-5
DEPTH = 2
DEEPNORM_ALPHA = (2 * DEPTH) ** 0.25
NEG = -1e30

ROW_TILE = 512
QKV_ROWS = 1024
RANK_CHUNK = 256
SGU_ROWS = 512
EXPERT_ROWS = 512
DMA_UNROLL = 8
DMA_QUEUES = 2


def _cparams(semantics, vmem_bytes):
    return pltpu.CompilerParams(dimension_semantics=semantics,
                                vmem_limit_bytes=min(int(vmem_bytes), VMEM_LIMIT_CAP))


def _layer_norm(y, g, b):
    mu = jnp.mean(y, axis=-1, keepdims=True)
    yc = y - mu
    var = jnp.mean(yc * yc, axis=-1, keepdims=True)
    return yc * lax.rsqrt(var + LN_EPS) * g + b


def _views_kernel(xa_ref, xb_ref, *refs, na, dils):
    x_ref, view_refs, slab = refs[0], refs[1:-1], refs[-1]
    tm, d = x_ref.shape
    x = jnp.where(pl.program_id(0) < na, xa_ref[...], xb_ref[...])
    x_ref[...] = x
    for c in range(d // LANES):
        slab[c] = x[:, c * LANES:(c + 1) * LANES]
    for dil, v_ref in zip(dils, view_refs):
        for r in range(dil):
            for c in range(d // LANES):
                v_ref[:, r * d + c * LANES:r * d + (c + 1) * LANES] = slab[c, pl.ds(r, tm // dil, stride=dil), :]


def _token_views(xa, xb, dils):
    ta, d = xa.shape
    tb = xb.shape[0]
    t = ta + tb
    tm = ROW_TILE
    na, nb = ta // tm, tb // tm
    out_shapes = [jax.ShapeDtypeStruct((t, d), F32)] + [jax.ShapeDtypeStruct((t // dil, dil * d), F32) for dil in dils]
    out_specs = [pl.BlockSpec((tm, d), lambda i: (i, 0))] + [pl.BlockSpec((tm // dil, dil * d), lambda i: (i, 0))
                                                             for dil in dils]
    return pl.pallas_call(
        functools.partial(_views_kernel, na=na, dils=dils),
        grid=(na + nb,),
        in_specs=[pl.BlockSpec((tm, d), lambda i: (jnp.minimum(i, na - 1), 0)),
                  pl.BlockSpec((tm, d), lambda i: (jnp.maximum(i - na, 0), 0))],
        out_specs=out_specs,
        out_shape=out_shapes,
        scratch_shapes=[pltpu.VMEM((d // LANES, tm, LANES), F32)],
        compiler_params=_cparams(("arbitrary",), 40 << 20),
        name="token_views",
    )(xa, xb)


def _qkv_kernel(x_ref, w_ref, o_ref):
    o_ref[...] = jnp.dot(x_ref[...].astype(BF16), w_ref[...],
                         preferred_element_type=F32).astype(BF16)


def _qkv_project(xv, w, dil):
    n = xv.shape[0]
    d = xv.shape[1] // dil
    tm = min(QKV_ROWS, n)
    width = w.shape[1]
    return pl.pallas_call(
        _qkv_kernel,
        grid=(dil, n // tm),
        in_specs=[pl.BlockSpec((tm, d), lambda r, i: (i, r)),
                  pl.BlockSpec((d, width), lambda r, i: (0, 0))],
        out_specs=pl.BlockSpec((None, tm, width), lambda r, i: (r, i, 0)),
        out_shape=jax.ShapeDtypeStruct((dil, n, width), BF16),
        compiler_params=_cparams(("parallel", "parallel"),
                                 2 * (tm * d * 4 + d * width * 2 + tm * width * 2) + tm * width * 4
                                 + (8 << 20)),
        name=f"qkv_project_d{dil}",
    )(xv, w)


def _attn_kernel(q_ref, kp_ref, kb_ref, kn_ref, vp_ref, vb_ref, vn_ref, bias_ref, o_ref, lse_ref, *, segs):
    (n_a, len_a), (_, len_b) = segs
    step_rows = q_ref.shape[0]
    lane = lax.broadcasted_iota(jnp.int32, (1, LANES), 1)
    first_head = lane < HEAD_DIM
    nt = (((1,), (1,)), ((), ()))

    def window(prev, body, nxt, first, sl):
        parts = []
        if first < 0:
            parts.append(prev[:, sl])
        parts.append(body[max(first, 0):min(first + KV_WIN, step_rows), sl])
        if first + KV_WIN > step_rows:
            parts.append(nxt[:, sl])
        return jnp.concatenate(parts, axis=0)

    for sub in range(step_rows // Q_BLOCK):
        rows = slice(sub * Q_BLOCK, (sub + 1) * Q_BLOCK)
        row0 = pl.program_id(1) * step_rows + sub * Q_BLOCK
        in_a = row0 < n_a
        lo = jnp.where(in_a, (row0 // len_a) * len_a, n_a + ((row0 - n_a) // len_b) * len_b)
        hi = lo + jnp.where(in_a, len_a, len_b)
        krow = lax.broadcasted_iota(jnp.int32, (1, KV_WIN), 1) + (row0 - N_SIDE)
        col_bias = jnp.where((krow >= lo) & (krow < hi), 0.0, NEG).astype(F32)
        lse_tile = jnp.zeros((Q_BLOCK, LANES), F32)
        for p in range(HEADS // 2):
            sl = slice(p * LANES, (p + 1) * LANES)
            q = q_ref[rows, sl]
            k = window(kp_ref, kb_ref, kn_ref, sub * Q_BLOCK - N_SIDE, sl)
            v = window(vp_ref, vb_ref, vn_ref, sub * Q_BLOCK - N_SIDE, sl)
            zero = jnp.zeros_like(q)
            outs, inv_ls = [], []
            for j, qh in enumerate((jnp.where(first_head, q, zero), jnp.where(first_head, zero, q))):
                h = 2 * p + j
                s = lax.dot_general(qh, k, nt, preferred_element_type=F32)
                s = s + bias_ref[h] + col_bias
                m = jnp.max(s, axis=-1, keepdims=True)
                e = jnp.exp(s - m)
                l = jnp.sum(e, axis=-1, keepdims=True)
                outs.append(jnp.dot(e.astype(BF16), v, preferred_element_type=F32))
                inv_ls.append(1.0 / l)
                lse_tile = jnp.where(lane == h, m + jnp.log(l), lse_tile)
            o = jnp.where(first_head, outs[0] * inv_ls[0], outs[1] * inv_ls[1])
            o_ref[rows, sl] = o.astype(BF16)
        lse_ref[rows, :] = lse_tile


def _band_bias(slopes, dil):
    qi = jnp.arange(Q_BLOCK)[:, None]
    kc = jnp.arange(KV_WIN)[None, :]
    dist = jnp.abs(kc - N_SIDE - qi)
    pen = -slopes[:, None, None] * (dist * dil).astype(F32)[None]
    return jnp.where((dist <= N_SIDE)[None], pen, NEG).astype(F32)


def _attention(qkv, bias, dil, segs):
    _, n, _ = qkv.shape
    nkb = n // KV_BLOCK
    q_step = max(s for s in ATTN_Q_STEPS if (n // Q_BLOCK) % s == 0)
    step_rows = q_step * Q_BLOCK
    per_step = step_rows // KV_BLOCK

    def body_spec(col):
        return pl.BlockSpec((None, step_rows, GROUP_DIM), lambda r, i: (r, i, col))

    def edge_spec(col, after):
        def imap(r, i):
            return (r, jnp.clip((i + after) * per_step - 1 + after, 0, nkb - 1), col)
        return pl.BlockSpec((None, KV_BLOCK, GROUP_DIM), imap)

    in_specs = ([body_spec(0)]
                + [edge_spec(1, 0), body_spec(1), edge_spec(1, 1)]
                + [edge_spec(2, 0), body_spec(2), edge_spec(2, 1)]
                + [pl.BlockSpec((HEADS, Q_BLOCK, KV_WIN), lambda r, i: (0, 0, 0))])
    return pl.pallas_call(
        functools.partial(_attn_kernel, segs=segs),
        grid=(dil, n // step_rows),
        in_specs=in_specs,
        out_specs=[pl.BlockSpec((step_rows, GROUP_DIM), lambda r, i: (i, r)),
                   pl.BlockSpec((step_rows, LANES), lambda r, i: (i, r))],
        out_shape=[jax.ShapeDtypeStruct((n, dil * GROUP_DIM), BF16),
                   jax.ShapeDtypeStruct((n, dil * LANES), F32)],
        compiler_params=_cparams(("parallel", "parallel"), 32 << 20),
        name=f"band_attention_d{dil}",
    )(*([qkv] * 7), bias)


def _merge_kernel(o0, o1, o2, l0, l1, l2, x_ref, wo_ref, bo_ref, g_ref, b_ref, y_ref, o_tok, l_tok, *, dils):
    tm = x_ref.shape[0]
    n_slabs = GROUP_DIM // LANES
    for g, (dil, o_ref, l_ref) in enumerate(zip(dils, (o0, o1, o2), (l0, l1, l2))):
        for r in range(dil):
            rows = pl.ds(r, tm // dil, stride=dil)
            l_tok[g, rows, :] = l_ref[:, r * LANES:(r + 1) * LANES]
            for c in range(n_slabs):
                lo = r * GROUP_DIM + c * LANES
                o_tok[g, c, rows, :] = o_ref[:, lo:lo + LANES].astype(F32)
    lses = [l_tok[g] for g in range(len(dils))]
    m = jnp.maximum(jnp.maximum(lses[0], lses[1]), lses[2])
    es = [jnp.exp(l - m) for l in lses]
    inv = 1.0 / (es[0] + es[1] + es[2])
    ws = [e * inv for e in es]
    lane = lax.broadcasted_iota(jnp.int32, (1, LANES), 1)
    first_head = lane < HEAD_DIM
    parts = []
    for p in range(HEADS // 2):
        acc = None
        for g, w in enumerate(ws):
            wp = jnp.where(first_head, w[:, 2 * p:2 * p + 1], w[:, 2 * p + 1:2 * p + 2])
            term = wp * o_tok[g, p]
            acc = term if acc is None else acc + term
        parts.append(acc)
    merged = jnp.concatenate(parts, axis=-1).astype(BF16)
    h = jnp.dot(merged, wo_ref[...], preferred_element_type=F32) + bo_ref[...]
    y_ref[...] = _layer_norm(DEEPNORM_ALPHA * x_ref[...] + h, g_ref[...], b_ref[...])


def _merge_project(os_, lses, dils, x, w_o, b_o, ln_g, ln_b):
    t, d = x.shape
    tm = ROW_TILE
    row = lambda i: (i, 0)
    const = lambda i: (0, 0)
    vec = pl.BlockSpec((1, d), const)
    return pl.pallas_call(
        functools.partial(_merge_kernel, dils=dils),
        grid=(t // tm,),
        in_specs=([pl.BlockSpec((tm // dil, dil * GROUP_DIM), row) for dil in dils]
                  + [pl.BlockSpec((tm // dil, dil * LANES), row) for dil in dils]
                  + [pl.BlockSpec((tm, d), row), pl.BlockSpec((GROUP_DIM, d), const), vec, vec, vec]),
        out_specs=pl.BlockSpec((tm, d), row),
        out_shape=jax.ShapeDtypeStruct((t, d), F32),
        scratch_shapes=[pltpu.VMEM((len(dils), GROUP_DIM // LANES, tm, LANES), F32),
                        pltpu.VMEM((len(dils), tm, LANES), F32)],
        compiler_params=_cparams(("parallel",), 40 << 20),
        name="merge_outproj_ln",
    )(*os_, *lses, x, w_o, b_o, ln_g, ln_b)


def _split_bf16(a):
    hi = a.astype(BF16)
    return hi, (a - hi.astype(F32)).astype(BF16)


def _router_kernel(x_ref, w_ref, b_ref, e_ref, g_ref, r_ref, cnt_ref, carry_ref):
    tm = x_ref.shape[0]

    @pl.when(pl.program_id(0) == 0)
    def _():
        carry_ref[...] = jnp.zeros_like(carry_ref)

    xh, xl = _split_bf16(x_ref[...])
    wh, wl = _split_bf16(w_ref[...])
    dot = functools.partial(jnp.dot, preferred_element_type=F32)
    logits = dot(xh, wh) + (dot(xl, wh) + dot(xh, wl)) + b_ref[...]
    lane = lax.broadcasted_iota(jnp.int32, logits.shape, 1).astype(F32)
    col4 = lax.broadcasted_iota(jnp.int32, (tm, TOP_K), 1)
    work = logits
    picks, vals, idxs = [], [], []
    for _k in range(TOP_K):
        mx = jnp.max(work, axis=-1, keepdims=True)
        idx = jnp.min(jnp.where(work == mx, lane, float(N_EXPERTS)), axis=-1, keepdims=True)
        pick = lane == idx
        work = jnp.where(pick, -jnp.inf, work)
        picks.append(pick)
        vals.append(mx)
        idxs.append(idx)
    ex = [jnp.exp(v - vals[0]) for v in vals]
    inv = 1.0 / (ex[0] + ex[1] + ex[2] + ex[3])
    chosen = jnp.zeros(logits.shape, F32)
    for pick in picks:
        chosen = chosen + pick.astype(F32)
    ch = min(RANK_CHUNK, tm)
    r_i = lax.broadcasted_iota(jnp.int32, (ch, ch), 0)
    c_i = lax.broadcasted_iota(jnp.int32, (ch, ch), 1)
    tri = (c_i < r_i).astype(BF16)
    carry = carry_ref[...]
    before = []
    for c in range(tm // ch):
        part = chosen[c * ch:(c + 1) * ch]
        before.append(dot(tri, part.astype(BF16)) + carry)
        carry = carry + jnp.sum(part, axis=0, keepdims=True)
    before = jnp.concatenate(before, axis=0)
    e_out = jnp.zeros((tm, TOP_K), jnp.int32)
    g_out = jnp.zeros((tm, TOP_K), F32)
    r_out = jnp.zeros((tm, TOP_K), jnp.int32)
    for k in range(TOP_K):
        rank = jnp.sum(jnp.where(picks[k], before, 0.0), axis=-1, keepdims=True)
        e_out = jnp.where(col4 == k, idxs[k].astype(jnp.int32), e_out)
        g_out = jnp.where(col4 == k, ex[k] * inv, g_out)
        r_out = jnp.where(col4 == k, rank.astype(jnp.int32), r_out)
    e_ref[...] = e_out
    g_ref[...] = g_out
    r_ref[...] = r_out
    carry_ref[...] = carry
    cnt_ref[...] = carry


def _router(x, w, b):
    t, d = x.shape
    tm = ROW_TILE
    row = lambda i: (i, 0)
    const = lambda i: (0, 0)
    k_spec = pl.BlockSpec((tm, TOP_K), row)
    return pl.pallas_call(
        _router_kernel,
        grid=(t // tm,),
        in_specs=[pl.BlockSpec((tm, d), row), pl.BlockSpec((d, N_EXPERTS), const),
                  pl.BlockSpec((1, N_EXPERTS), const)],
        out_specs=[k_spec, k_spec, k_spec, pl.BlockSpec((1, N_EXPERTS), const)],
        out_shape=[jax.ShapeDtypeStruct((t, TOP_K), jnp.int32), jax.ShapeDtypeStruct((t, TOP_K), F32),
                   jax.ShapeDtypeStruct((t, TOP_K), jnp.int32), jax.ShapeDtypeStruct((1, N_EXPERTS), F32)],
        scratch_shapes=[pltpu.VMEM((1, N_EXPERTS), F32)],
        compiler_params=_cparams(("arbitrary",), 32 << 20),
        name="router_topk",
    )(x, w, b)


def _route_plan(e, rank, counts, bm, tm):
    t = e.shape[0]
    nb = TOP_K * t // bm + N_EXPERTS
    counts = counts.reshape(N_EXPERTS).astype(jnp.int32)
    padded = (counts + bm - 1) // bm * bm
    pend = jnp.cumsum(padded)
    pstart = pend - padded
    onehot = e[..., None] == jnp.arange(N_EXPERTS, dtype=jnp.int32)
    dest = jnp.sum(jnp.where(onehot, pstart, 0), axis=-1) + rank
    blk_row0 = jnp.arange(nb, dtype=jnp.int32) * bm
    blk_exp = jnp.minimum(jnp.sum((pend[None, :] <= blk_row0[:, None]).astype(jnp.int32), axis=1),
                          N_EXPERTS - 1)
    n_used = pend[-1:] // bm
    return dest.reshape(t // tm, tm * TOP_K), blk_exp, n_used, pstart + counts, pend, padded // bm


ROW_SUBLANES = D_MODEL // LANES


def _store_row_tiles(ref, base, x):
    for c in range(ROW_SUBLANES):
        ref[pl.ds(base + c, x.shape[0], stride=ROW_SUBLANES), :] = x[:, c * LANES:(c + 1) * LANES]


def _load_row_tiles(ref, base, n):
    return jnp.concatenate([ref[pl.ds(base + c, n, stride=ROW_SUBLANES), :] for c in range(ROW_SUBLANES)], axis=1)


def _tile_rows(ref, row):
    return ref.at[pl.ds(pl.multiple_of(row * ROW_SUBLANES, ROW_SUBLANES), ROW_SUBLANES), :]


def _dispatch_kernel(padlo_ref, pend_ref, nused_ref, dest_hbm, x_ref, xs_hbm, idx0, idx1, pk, zbuf, sems):
    i = pl.program_id(0)
    n = pl.num_programs(0)
    tm = x_ref.shape[0]
    bm = zbuf.shape[0] // ROW_SUBLANES
    dump = xs_hbm.shape[0] // ROW_SUBLANES - bm
    idx = (idx0, idx1)
    slot_rows = tm * ROW_SUBLANES

    def idx_copy(step, s):
        return pltpu.make_async_copy(dest_hbm.at[step], idx[s], sems.at[s])

    def rows_wait(s):
        for _ in range(TOP_K):
            pltpu.make_async_copy(pk.at[pl.ds(s * slot_rows, slot_rows), :], xs_hbm.at[pl.ds(0, slot_rows), :],
                                  sems.at[2 + s]).wait()

    @pl.when(i == 0)
    def _():
        idx_copy(0, 0).start()
        zbuf[...] = jnp.zeros_like(zbuf)

        def zero_block(b, c):
            cp = pltpu.make_async_copy(zbuf, xs_hbm.at[pl.ds(pl.multiple_of(b * zbuf.shape[0], ROW_SUBLANES),
                                                             zbuf.shape[0]), :], sems.at[4])
            cp.start()
            cp.wait()
            return c
        lax.fori_loop(nused_ref[0], dump // bm + 1, zero_block, 0)
        for e in range(N_EXPERTS):
            def zero_row(jj, c, e=e):
                for u in range(DMA_QUEUES):
                    j = jj * DMA_QUEUES + u
                    r = padlo_ref[e] + j
                    r = jnp.where(r < pend_ref[e], r, dump + j)
                    pltpu.make_async_copy(_tile_rows(zbuf, j), _tile_rows(xs_hbm, r), sems.at[4]).start(priority=u)
                return c
            lax.fori_loop(0, bm // DMA_QUEUES, zero_row, 0, unroll=DMA_UNROLL // DMA_QUEUES)
            pltpu.make_async_copy(zbuf, xs_hbm.at[pl.ds(0, zbuf.shape[0]), :], sems.at[4]).wait()

    def step(s):
        idx_copy(i, s).wait()

        @pl.when(i + 1 < n)
        def _():
            idx_copy(i + 1, 1 - s).start()

        @pl.when(i >= 2)
        def _():
            rows_wait(s)

        _store_row_tiles(pk, s * slot_rows, x_ref[...])

        def issue(j, c):
            for k in range(TOP_K):
                pltpu.make_async_copy(_tile_rows(pk, s * tm + j), _tile_rows(xs_hbm, idx[s][j * TOP_K + k]),
                                      sems.at[2 + s]).start(priority=k % DMA_QUEUES)
            return c
        lax.fori_loop(0, tm, issue, 0, unroll=DMA_UNROLL)

        @pl.when(i == n - 1)
        def _():
            rows_wait(s)

        @pl.when(jnp.logical_and(i == n - 1, n >= 2))
        def _():
            rows_wait(1 - s)

    for s in range(2):
        pl.when(i % 2 == s)(functools.partial(step, s))


def _dispatch(x, dest, pad_lo, pend, n_used, bm):
    t, d = x.shape
    nblk, per = dest.shape
    tm = per // TOP_K
    nb = TOP_K * t // bm + N_EXPERTS
    grid_spec = pltpu.PrefetchScalarGridSpec(
        num_scalar_prefetch=3,
        grid=(nblk,),
        in_specs=[pl.BlockSpec(memory_space=pl.ANY), pl.BlockSpec((tm, d), lambda i, a, b, c: (i, 0))],
        out_specs=pl.BlockSpec(memory_space=pl.ANY),
        scratch_shapes=[pltpu.SMEM((per,), jnp.int32), pltpu.SMEM((per,), jnp.int32),
                        pltpu.VMEM((2 * tm * ROW_SUBLANES, LANES), F32),
                        pltpu.VMEM((bm * ROW_SUBLANES, LANES), F32),
                        pltpu.SemaphoreType.DMA((5,))])
    return pl.pallas_call(
        _dispatch_kernel,
        grid_spec=grid_spec,
        out_shape=jax.ShapeDtypeStruct(((nb * bm + bm) * ROW_SUBLANES, LANES), F32),
        compiler_params=_cparams(("arbitrary",), 32 << 20),
        name="moe_dispatch",
    )(pad_lo, pend, n_used, dest, x)


def _expert_kernel(be_ref, nused_ref, gblk_ref, xs_ref, w1_hbm, b1g_ref, b1l_ref, w2_hbm, b2_ref, y_ref,
                   w1gt, w1lt, w2b, tbuf, w1s, w2s, slot_ref, sems, *, layer):
    i = pl.program_id(0)
    d, de = w2b.shape[1], w2b.shape[0]
    nt = (((1,), (1,)), ((), ()))

    def weight_copies(e, s):
        return (pltpu.make_async_copy(w1_hbm.at[layer, e], w1s.at[s], sems.at[0, s]),
                pltpu.make_async_copy(w2_hbm.at[layer, e], w2s.at[s], sems.at[1, s]))

    @pl.when(i < nused_ref[0])
    def _():
        e = be_ref[i]
        changed = jnp.logical_or(i == 0, e != be_ref[jnp.maximum(i - 1, 0)])

        @pl.when(changed)
        def _():
            @pl.when(i == 0)
            def _():
                slot_ref[0] = 0
                for cp in weight_copies(e, 0):
                    cp.start()

            s = slot_ref[0]
            for cp in weight_copies(e, s):
                cp.wait()
            j = i + gblk_ref[e]

            @pl.when(j < nused_ref[0])
            def _():
                for cp in weight_copies(be_ref[jnp.minimum(j, be_ref.shape[0] - 1)], 1 - s):
                    cp.start()

            for a in range(d // LANES):
                cols = slice(a * LANES, (a + 1) * LANES)
                tbuf[...] = w1s[s, cols, :].T
                w1gt[:, cols] = tbuf[pl.ds(0, de, stride=2), :].astype(BF16)
                w1lt[:, cols] = tbuf[pl.ds(1, de, stride=2), :].astype(BF16)
            w2b[...] = w2s[s].astype(BF16)
            slot_ref[0] = 1 - s

        bm = xs_ref.shape[0] // ROW_SUBLANES
        x = _load_row_tiles(xs_ref, 0, bm).astype(BF16)
        hg = lax.dot_general(x, w1gt[...], nt, preferred_element_type=F32) + b1g_ref[...]
        hl = lax.dot_general(x, w1lt[...], nt, preferred_element_type=F32) + b1l_ref[...]
        hg = jnp.minimum(hg, SWIGLU_LIMIT)
        hl = jnp.clip(hl, -SWIGLU_LIMIT, SWIGLU_LIMIT)
        act = hg * (1.0 / (1.0 + jnp.exp(-SWIGLU_ALPHA * hg))) * (hl + 1.0)
        y = jnp.dot(act.astype(BF16), w2b[...], preferred_element_type=F32) + b2_ref[...]
        _store_row_tiles(y_ref, 0, y)

    @pl.when(i >= nused_ref[0])
    def _():
        y_ref[...] = jnp.zeros_like(y_ref)


def _experts(xs, blk_exp, n_used, grp_blocks, layer, w1, b1g, b1l, w2, b2, bm):
    d = ROW_SUBLANES * LANES
    nb = blk_exp.shape[0]
    de = w2.shape[2]
    blk = (bm * ROW_SUBLANES, LANES)
    bmap = lambda i, be, nu, gb: (layer, be[i], 0, 0)
    any_spec = pl.BlockSpec(memory_space=pl.ANY)
    grid_spec = pltpu.PrefetchScalarGridSpec(
        num_scalar_prefetch=3,
        grid=(nb,),
        in_specs=[pl.BlockSpec(blk, lambda i, be, nu, gb: (jnp.minimum(i, nu[0] - 1), 0)),
                  any_spec,
                  pl.BlockSpec((None, None, 1, de), bmap), pl.BlockSpec((None, None, 1, de), bmap),
                  any_spec,
                  pl.BlockSpec((None, None, 1, d), bmap)],
        out_specs=pl.BlockSpec(blk, lambda i, be, nu, gb: (i, 0)),
        scratch_shapes=[pltpu.VMEM((de, d), BF16), pltpu.VMEM((de, d), BF16), pltpu.VMEM((de, d), BF16),
                        pltpu.VMEM((2 * de, LANES), F32),
                        pltpu.VMEM((2, d, 2 * de), F32), pltpu.VMEM((2, de, d), F32),
                        pltpu.SMEM((1,), jnp.int32), pltpu.SemaphoreType.DMA((2, 2))])
    return pl.pallas_call(
        functools.partial(_expert_kernel, layer=layer),
        grid_spec=grid_spec,
        out_shape=jax.ShapeDtypeStruct((nb * blk[0], LANES), F32),
        compiler_params=_cparams(("arbitrary",), 54 << 20),
        name="moe_experts",
    )(blk_exp, n_used, grp_blocks, xs, w1, b1g, b1l, w2, b2)


def _combine_kernel(dest_hbm, gate_ref, x_ref, g_ref, b_ref, y_hbm, o_ref, idx0, idx1, ybuf, sems, *, blk_off):
    i = pl.program_id(0)
    n = pl.num_programs(0)
    tm = x_ref.shape[0]
    idx = (idx0, idx1)
    slab_rows = tm * ROW_SUBLANES

    def slab(s, k):
        return (s * TOP_K + k) * tm

    def idx_copy(step, s):
        return pltpu.make_async_copy(dest_hbm.at[blk_off + step], idx[s], sems.at[s])

    def issue_rows(s):
        def body(j, c):
            for k in range(TOP_K):
                pltpu.make_async_copy(_tile_rows(y_hbm, idx[s][j * TOP_K + k]), _tile_rows(ybuf, slab(s, k) + j),
                                      sems.at[2 + s]).start(priority=k % DMA_QUEUES)
            return c
        lax.fori_loop(0, tm, body, 0, unroll=DMA_UNROLL)

    @pl.when(i == 0)
    def _():
        cp = idx_copy(0, 0)
        cp.start()
        cp.wait()
        issue_rows(0)

    @pl.when(jnp.logical_and(i == 0, n >= 2))
    def _():
        idx_copy(1, 1).start()

    def step(s):
        @pl.when(i + 1 < n)
        def _():
            idx_copy(i + 1, 1 - s).wait()
            issue_rows(1 - s)

        @pl.when(i + 2 < n)
        def _():
            idx_copy(i + 2, s).start()

        for k in range(TOP_K):
            pltpu.make_async_copy(y_hbm.at[pl.ds(0, slab_rows), :],
                                  ybuf.at[pl.ds(slab(s, k) * ROW_SUBLANES, slab_rows), :], sems.at[2 + s]).wait()
        gate = gate_ref[...]
        h = None
        for k in range(TOP_K):
            term = gate[:, k:k + 1] * _load_row_tiles(ybuf, slab(s, k) * ROW_SUBLANES, tm)
            h = term if h is None else h + term
        o_ref[...] = _layer_norm(DEEPNORM_ALPHA * x_ref[...] + h, g_ref[...], b_ref[...])

    for s in range(2):
        pl.when(i % 2 == s)(functools.partial(step, s))


def _combine(y, dest, gate, x, ln_g, ln_b, row_lo, rows):
    _, d = x.shape
    per = dest.shape[1]
    tm = per // TOP_K
    off = row_lo // tm
    vec = pl.BlockSpec((1, d), lambda i: (0, 0))
    return pl.pallas_call(
        functools.partial(_combine_kernel, blk_off=off),
        grid=(rows // tm,),
        in_specs=[pl.BlockSpec(memory_space=pl.ANY),
                  pl.BlockSpec((tm, TOP_K), lambda i: (off + i, 0)),
                  pl.BlockSpec((tm, d), lambda i: (off + i, 0)), vec, vec,
                  pl.BlockSpec(memory_space=pl.ANY)],
        out_specs=pl.BlockSpec((tm, d), lambda i: (i, 0)),
        out_shape=jax.ShapeDtypeStruct((rows, d), F32),
        scratch_shapes=[pltpu.SMEM((per,), jnp.int32), pltpu.SMEM((per,), jnp.int32),
                        pltpu.VMEM((2 * TOP_K * tm * ROW_SUBLANES, LANES), F32),
                        pltpu.SemaphoreType.DMA((4,))],
        compiler_params=_cparams(("arbitrary",), 40 << 20),
        name="moe_combine_ln",
    )(dest, gate, x, ln_g, ln_b, y)


def _moe_block(x, layer, router_w, router_b, w1, b1g, b1l, w2, b2):
    e, gate, rank, counts = _router(x, router_w, router_b)
    dest, blk_exp, n_used, pad_lo, pend, grp_blocks = _route_plan(e, rank, counts, EXPERT_ROWS, ROW_TILE)
    xs = _dispatch(x, dest, pad_lo, pend, n_used, EXPERT_ROWS)
    y = _experts(xs, blk_exp, n_used, grp_blocks, layer, w1, b1g, b1l, w2, b2, EXPERT_ROWS)
    return y, dest, gate


def _gelu(z):
    return 0.5 * z * (1.0 + jnp.tanh(0.7978845608028654 * (z + 0.044715 * (z * z * z))))


def _sgu_kernel(x_ref, win_ref, bin_ref, ng_ref, nb_ref, ws_ref, bs_ref, wout_ref, bout_ref,
                g_ref, b_ref, y_ref):
    tm = x_ref.shape[0]
    x = x_ref[...]
    xb = x.astype(BF16)
    dot = functools.partial(jnp.dot, preferred_element_type=F32)
    v = _gelu(dot(xb, win_ref[:, SGU_HALF:]) + bin_ref[:, SGU_HALF:])
    v = _layer_norm(v, ng_ref[...], nb_ref[...]).astype(BF16)
    acc = jnp.zeros((tm, D_MODEL), F32)
    for g in range(SGU_GROUPS):
        cols = slice(g * SGU_GROUP_DIM, (g + 1) * SGU_GROUP_DIM)
        u = _gelu(dot(xb, win_ref[:, cols]) + bin_ref[:, cols])
        mixed = [dot(ws_ref[g], v[c * SGU_CHUNK:(c + 1) * SGU_CHUNK, cols]) + bs_ref[:, g:g + 1]
                 for c in range(tm // SGU_CHUNK)]
        mixed = jnp.concatenate(mixed, axis=0)
        acc = acc + dot((u * mixed).astype(BF16), wout_ref[cols, :])
    h = acc + bout_ref[...]
    y_ref[...] = _layer_norm(DEEPNORM_ALPHA * x + h, g_ref[...], b_ref[...])


def _sgu_block(x, w_in, b_in, norm_g, norm_b, w_s, b_s_t, w_out, b_out, ln_g, ln_b):
    t, d = x.shape
    tm = SGU_ROWS
    row = lambda i: (i, 0)
    c2 = lambda i: (0, 0)
    c3 = lambda i: (0, 0, 0)
    once = dict(pipeline_mode=pl.Buffered(1))
    vec = pl.BlockSpec((1, d), c2)
    return pl.pallas_call(
        _sgu_kernel,
        grid=(t // tm,),
        in_specs=[pl.BlockSpec((tm, d), row),
                  pl.BlockSpec((d, 2 * SGU_HALF), c2, **once), pl.BlockSpec((1, 2 * SGU_HALF), c2),
                  pl.BlockSpec((1, SGU_HALF), c2), pl.BlockSpec((1, SGU_HALF), c2),
                  pl.BlockSpec((SGU_GROUPS, SGU_CHUNK, SGU_CHUNK), c3),
                  pl.BlockSpec((SGU_CHUNK, SGU_GROUPS), c2),
                  pl.BlockSpec((SGU_HALF, d), c2, **once), vec, vec, vec],
        out_specs=pl.BlockSpec((tm, d), row),
        out_shape=jax.ShapeDtypeStruct((t, d), F32),
        compiler_params=_cparams(("parallel",), 52 << 20),
        name="sgu_block",
    )(x, w_in, b_in, norm_g, norm_b, w_s, b_s_t, w_out, b_out, ln_g, ln_b)


def _attention_block(x, views, w_qkv, w_o, b_o, ln_g, ln_b, seq_rows):
    n_groups = N_GROUPS * HEADS
    slopes = (2.0 ** (-8.0 * jnp.arange(1, n_groups + 1, dtype=F32) / n_groups)).reshape(N_GROUPS, HEADS)
    scale = jnp.concatenate([jnp.full((GROUP_DIM,), HEAD_DIM ** -0.5, F32), jnp.ones((2 * GROUP_DIM,), F32)])
    os_, lses = [], []
    for g, (_, dil) in enumerate(DIL_CONFIGS):
        cols = [w_qkv[:, (j * N_GROUPS + g) * GROUP_DIM:(j * N_GROUPS + g + 1) * GROUP_DIM] for j in range(3)]
        w_g = (jnp.concatenate(cols, axis=1) * scale).astype(BF16)
        qkv = _qkv_project(views[dil], w_g, dil)
        segs = tuple((n // dil, length // dil) for n, length in seq_rows)
        o, lse = _attention(qkv, _band_bias(slopes[g], dil), dil, segs)
        os_.append(o)
        lses.append(lse)
    dils = tuple(dil for _, dil in DIL_CONFIGS)
    return _merge_project(os_, lses, dils, x, w_o.astype(BF16), b_o[None], ln_g[None], ln_b[None])


def kernel(x_prompt, x_sample, attn_w_qkv, attn_w_o, attn_b_o, sgu_w_in, sgu_b_in, sgu_norm_g, sgu_norm_b,
           sgu_w_s, sgu_b_s, sgu_w_out, sgu_b_out, router_w, router_b, exp_w1, exp_b1, exp_w2, exp_b2,
           ln_mix_g, ln_mix_b, ln_ffn_g, ln_ffn_b):
    d = x_prompt.shape[-1]
    tp = x_prompt.shape[0] * x_prompt.shape[1]
    ts = x_sample.shape[0] * x_sample.shape[1]
    seq_rows = ((tp, x_prompt.shape[1]), (ts, x_sample.shape[1]))
    dils = tuple(dil for _, dil in DIL_CONFIGS if dil > 1)
    x, *xviews = _token_views(x_prompt.reshape(tp, d), x_sample.reshape(ts, d), dils)
    views = dict(zip(dils, xviews))
    views[1] = x

    b1g = exp_b1[:, :, None, 0::2]
    b1l = exp_b1[:, :, None, 1::2]
    b2 = exp_b2[:, :, None, :]

    outs = None
    for i in range(DEPTH):
        j = i // 2
        if i % 2 == 0:
            x = _attention_block(x, views, attn_w_qkv[j], attn_w_o[j], attn_b_o[j], ln_mix_g[i], ln_mix_b[i],
                                 seq_rows)
        else:
            x = _sgu_block(x, sgu_w_in[j].astype(BF16), sgu_b_in[j][None], sgu_norm_g[j][None], sgu_norm_b[j][None],
                           sgu_w_s[j].astype(BF16), jnp.transpose(sgu_b_s[j]), sgu_w_out[j].astype(BF16),
                           sgu_b_out[j][None], ln_mix_g[i][None], ln_mix_b[i][None])
        y, dest, gate = _moe_block(x, i, router_w[i], router_b[i][None], exp_w1, b1g, b1l, exp_w2, b2)
        g_ln, b_ln = ln_ffn_g[i][None], ln_ffn_b[i][None]
        if i + 1 < DEPTH:
            x = _combine(y, dest, gate, x, g_ln, b_ln, 0, tp + ts)
        else:
            outs = (_combine(y, dest, gate, x, g_ln, b_ln, 0, tp).reshape(x_prompt.shape),
                    _combine(y, dest, gate, x, g_ln, b_ln, tp, ts).reshape(x_sample.shape))
    return outs
```

```python
import functools

import jax
import jax.numpy as jnp
from jax import lax
from jax.experimental import pallas as pl
from jax.experimental.pallas import tpu as pltpu

F32 = jnp.float32
BF16 = jnp.bfloat16

LANES = 128
SUBLANES = 8
VMEM_LIMIT_CAP = 56 * 1024 * 1024

D_MODEL = 1024
DIL_CONFIGS = ((128, 1), (512, 4), (2048, 16))
N_GROUPS = len(DIL_CONFIGS)
HEADS = 8
HEAD_DIM = 64
GROUP_DIM = HEADS * HEAD_DIM
N_SIDE = 64
Q_BLOCK = 128
KV_BLOCK = N_SIDE
KV_WIN = Q_BLOCK + 2 * N_SIDE
ATTN_Q_STEPS = (8, 4, 2, 1)
SGU_CHUNK = 128
SGU_HALF = 3 * D_MODEL
SGU_GROUPS = 8
SGU_GROUP_DIM = SGU_HALF // SGU_GROUPS
N_EXPERTS = 32
TOP_K = 4
D_EXPERT = D_MODEL
SWIGLU_ALPHA = 1.702
SWIGLU_LIMIT = 7.0
LN_EPS = 1e-5
DEPTH = 2
DEEPNORM_ALPHA = (2 * DEPTH) ** 0.25
NEG = -1e30

ROW_TILE = 512
QKV_ROWS = 1024
RANK_CHUNK = 256
SGU_ROWS = 512
EXPERT_ROWS = 512
DMA_UNROLL = 8
DMA_QUEUES = 2


def _cparams(semantics, vmem_bytes):
    return pltpu.CompilerParams(dimension_semantics=semantics,
                                vmem_limit_bytes=min(int(vmem_bytes), VMEM_LIMIT_CAP))


def _layer_norm(y, g, b):
    mu = jnp.mean(y, axis=-1, keepdims=True)
    yc = y - mu
    var = jnp.mean(yc * yc, axis=-1, keepdims=True)
    return yc * lax.rsqrt(var + LN_EPS) * g + b


def _views_kernel(xa_ref, xb_ref, *refs, na, dils):
    x_ref, view_refs, slab = refs[0], refs[1:-1], refs[-1]
    tm, d = x_ref.shape
    x = jnp.where(pl.program_id(0) < na, xa_ref[...], xb_ref[...])
    x_ref[...] = x
    for c in range(d // LANES):
        slab[c] = x[:, c * LANES:(c + 1) * LANES]
    for dil, v_ref in zip(dils, view_refs):
        for r in range(dil):
            for c in range(d // LANES):
                v_ref[:, r * d + c * LANES:r * d + (c + 1) * LANES] = slab[c, pl.ds(r, tm // dil, stride=dil), :]


def _token_views(xa, xb, dils):
    ta, d = xa.shape
    tb = xb.shape[0]
    t = ta + tb
    tm = ROW_TILE
    na, nb = ta // tm, tb // tm
    out_shapes = [jax.ShapeDtypeStruct((t, d), F32)] + [jax.ShapeDtypeStruct((t // dil, dil * d), F32) for dil in dils]
    out_specs = [pl.BlockSpec((tm, d), lambda i: (i, 0))] + [pl.BlockSpec((tm // dil, dil * d), lambda i: (i, 0))
                                                             for dil in dils]
    return pl.pallas_call(
        functools.partial(_views_kernel, na=na, dils=dils),
        grid=(na + nb,),
        in_specs=[pl.BlockSpec((tm, d), lambda i: (jnp.minimum(i, na - 1), 0)),
                  pl.BlockSpec((tm, d), lambda i: (jnp.maximum(i - na, 0), 0))],
        out_specs=out_specs,
        out_shape=out_shapes,
        scratch_shapes=[pltpu.VMEM((d // LANES, tm, LANES), F32)],
        compiler_params=_cparams(("arbitrary",), 40 << 20),
        name="token_views",
    )(xa, xb)


def _qkv_kernel(x_ref, w_ref, o_ref):
    o_ref[...] = jnp.dot(x_ref[...].astype(BF16), w_ref[...],
                         preferred_element_type=F32).astype(BF16)


def _qkv_project(xv, w, dil):
    n = xv.shape[0]
    d = xv.shape[1] // dil
    tm = min(QKV_ROWS, n)
    width = w.shape[1]
    return pl.pallas_call(
        _qkv_kernel,
        grid=(dil, n // tm),
        in_specs=[pl.BlockSpec((tm, d), lambda r, i: (i, r)),
                  pl.BlockSpec((d, width), lambda r, i: (0, 0))],
        out_specs=pl.BlockSpec((None, tm, width), lambda r, i: (r, i, 0)),
        out_shape=jax.ShapeDtypeStruct((dil, n, width), BF16),
        compiler_params=_cparams(("parallel", "parallel"),
                                 2 * (tm * d * 4 + d * width * 2 + tm * width * 2) + tm * width * 4
                                 + (8 << 20)),
        name=f"qkv_project_d{dil}",
    )(xv, w)


def _attn_kernel(q_ref, kp_ref, kb_ref, kn_ref, vp_ref, vb_ref, vn_ref, bias_ref, o_ref, lse_ref, *, segs):
    (n_a, len_a), (_, len_b) = segs
    step_rows = q_ref.shape[0]
    lane = lax.broadcasted_iota(jnp.int32, (1, LANES), 1)
    first_head = lane < HEAD_DIM
    nt = (((1,), (1,)), ((), ()))

    def window(prev, body, nxt, first, sl):
        parts = []
        if first < 0:
            parts.append(prev[:, sl])
        parts.append(body[max(first, 0):min(first + KV_WIN, step_rows), sl])
        if first + KV_WIN > step_rows:
            parts.append(nxt[:, sl])
        return jnp.concatenate(parts, axis=0)

    for sub in range(step_rows // Q_BLOCK):
        rows = slice(sub * Q_BLOCK, (sub + 1) * Q_BLOCK)
        row0 = pl.program_id(1) * step_rows + sub * Q_BLOCK
        in_a = row0 < n_a
        lo = jnp.where(in_a, (row0 // len_a) * len_a, n_a + ((row0 - n_a) // len_b) * len_b)
        hi = lo + jnp.where(in_a, len_a, len_b)
        krow = lax.broadcasted_iota(jnp.int32, (1, KV_WIN), 1) + (row0 - N_SIDE)
        col_bias = jnp.where((krow >= lo) & (krow < hi), 0.0, NEG).astype(F32)
        lse_tile = jnp.zeros((Q_BLOCK, LANES), F32)
        for p in range(HEADS // 2):
            sl = slice(p * LANES, (p + 1) * LANES)
            q = q_ref[rows, sl]
            k = window(kp_ref, kb_ref, kn_ref, sub * Q_BLOCK - N_SIDE, sl)
            v = window(vp_ref, vb_ref, vn_ref, sub * Q_BLOCK - N_SIDE, sl)
            zero = jnp.zeros_like(q)
            outs, inv_ls = [], []
            for j, qh in enumerate((jnp.where(first_head, q, zero), jnp.where(first_head, zero, q))):
                h = 2 * p + j
                s = lax.dot_general(qh, k, nt, preferred_element_type=F32)
                s = s + bias_ref[h] + col_bias
                m = jnp.max(s, axis=-1, keepdims=True)
                e = jnp.exp(s - m)
                l = jnp.sum(e, axis=-1, keepdims=True)
                outs.append(jnp.dot(e.astype(BF16), v, preferred_element_type=F32))
                inv_ls.append(1.0 / l)
                lse_tile = jnp.where(lane == h, m + jnp.log(l), lse_tile)
            o = jnp.where(first_head, outs[0] * inv_ls[0], outs[1] * inv_ls[1])
            o_ref[rows, sl] = o.astype(BF16)
        lse_ref[rows, :] = lse_tile


def _band_bias(slopes, dil):
    qi = jnp.arange(Q_BLOCK)[:, None]
    kc = jnp.arange(KV_WIN)[None, :]
    dist = jnp.abs(kc - N_SIDE - qi)
    pen = -slopes[:, None, None] * (dist * dil).astype(F32)[None]
    return jnp.where((dist <= N_SIDE)[None], pen, NEG).astype(F32)


def _attention(qkv, bias, dil, segs):
    _, n, _ = qkv.shape
    nkb = n // KV_BLOCK
    q_step = max(s for s in ATTN_Q_STEPS if (n // Q_BLOCK) % s == 0)
    step_rows = q_step * Q_BLOCK
    per_step = step_rows // KV_BLOCK

    def body_spec(col):
        return pl.BlockSpec((None, step_rows, GROUP_DIM), lambda r, i: (r, i, col))

    def edge_spec(col, after):
        def imap(r, i):
            return (r, jnp.clip((i + after) * per_step - 1 + after, 0, nkb - 1), col)
        return pl.BlockSpec((None, KV_BLOCK, GROUP_DIM), imap)

    in_specs = ([body_spec(0)]
                + [edge_spec(1, 0), body_spec(1), edge_spec(1, 1)]
                + [edge_spec(2, 0), body_spec(2), edge_spec(2, 1)]
                + [pl.BlockSpec((HEADS, Q_BLOCK, KV_WIN), lambda r, i: (0, 0, 0))])
    return pl.pallas_call(
        functools.partial(_attn_kernel, segs=segs),
        grid=(dil, n // step_rows),
        in_specs=in_specs,
        out_specs=[pl.BlockSpec((step_rows, GROUP_DIM), lambda r, i: (i, r)),
                   pl.BlockSpec((step_rows, LANES), lambda r, i: (i, r))],
        out_shape=[jax.ShapeDtypeStruct((n, dil * GROUP_DIM), BF16),
                   jax.ShapeDtypeStruct((n, dil * LANES), F32)],
        compiler_params=_cparams(("parallel", "parallel"), 32 << 20),
        name=f"band_attention_d{dil}",
    )(*([qkv] * 7), bias)


def _merge_kernel(o0, o1, o2, l0, l1, l2, x_ref, wo_ref, bo_ref, g_ref, b_ref, y_ref, o_tok, l_tok, *, dils):
    tm = x_ref.shape[0]
    n_slabs = GROUP_DIM // LANES
    for g, (dil, o_ref, l_ref) in enumerate(zip(dils, (o0, o1, o2), (l0, l1, l2))):
        for r in range(dil):
            rows = pl.ds(r, tm // dil, stride=dil)
            l_tok[g, rows, :] = l_ref[:, r * LANES:(r + 1) * LANES]
            for c in range(n_slabs):
                lo = r * GROUP_DIM + c * LANES
                o_tok[g, c, rows, :] = o_ref[:, lo:lo + LANES].astype(F32)
    lses = [l_tok[g] for g in range(len(dils))]
    m = jnp.maximum(jnp.maximum(lses[0], lses[1]), lses[2])
    es = [jnp.exp(l - m) for l in lses]
    inv = 1.0 / (es[0] + es[1] + es[2])
    ws = [e * inv for e in es]
    lane = lax.broadcasted_iota(jnp.int32, (1, LANES), 1)
    first_head = lane < HEAD_DIM
    parts = []
    for p in range(HEADS // 2):
        acc = None
        for g, w in enumerate(ws):
            wp = jnp.where(first_head, w[:, 2 * p:2 * p + 1], w[:, 2 * p + 1:2 * p + 2])
            term = wp * o_tok[g, p]
            acc = term if acc is None else acc + term
        parts.append(acc)
    merged = jnp.concatenate(parts, axis=-1).astype(BF16)
    h = jnp.dot(merged, wo_ref[...], preferred_element_type=F32) + bo_ref[...]
    y_ref[...] = _layer_norm(DEEPNORM_ALPHA * x_ref[...] + h, g_ref[...], b_ref[...])


def _merge_project(os_, lses, dils, x, w_o, b_o, ln_g, ln_b):
    t, d = x.shape
    tm = ROW_TILE
    row = lambda i: (i, 0)
    const = lambda i: (0, 0)
    vec = pl.BlockSpec((1, d), const)
    return pl.pallas_call(
        functools.partial(_merge_kernel, dils=dils),
        grid=(t // tm,),
        in_specs=([pl.BlockSpec((tm // dil, dil * GROUP_DIM), row) for dil in dils]
                  + [pl.BlockSpec((tm // dil, dil * LANES), row) for dil in dils]
                  + [pl.BlockSpec((tm, d), row), pl.BlockSpec((GROUP_DIM, d), const), vec, vec, vec]),
        out_specs=pl.BlockSpec((tm, d), row),
        out_shape=jax.ShapeDtypeStruct((t, d), F32),
        scratch_shapes=[pltpu.VMEM((len(dils), GROUP_DIM // LANES, tm, LANES), F32),
                        pltpu.VMEM((len(dils), tm, LANES), F32)],
        compiler_params=_cparams(("parallel",), 40 << 20),
        name="merge_outproj_ln",
    )(*os_, *lses, x, w_o, b_o, ln_g, ln_b)


def _split_bf16(a):
    hi = a.astype(BF16)
    return hi, (a - hi.astype(F32)).astype(BF16)


def _router_kernel(x_ref, w_ref, b_ref, e_ref, g_ref, r_ref, cnt_ref, carry_ref):
    tm = x_ref.shape[0]

    @pl.when(pl.program_id(0) == 0)
    def _():
        carry_ref[...] = jnp.zeros_like(carry_ref)

    xh, xl = _split_bf16(x_ref[...])
    wh = w_ref[...].astype(BF16)
    dot = functools.partial(jnp.dot, preferred_element_type=F32)
    logits = dot(xh, wh) + dot(xl, wh) + b_ref[...]
    lane = lax.broadcasted_iota(jnp.int32, logits.shape, 1).astype(F32)
    col4 = lax.broadcasted_iota(jnp.int32, (tm, TOP_K), 1)
    work = logits
    picks, vals, idxs = [], [], []
    for _k in range(TOP_K):
        mx = jnp.max(work, axis=-1, keepdims=True)
        idx = jnp.min(jnp.where(work == mx, lane, float(N_EXPERTS)), axis=-1, keepdims=True)
        pick = lane == idx
        work = jnp.where(pick, -jnp.inf, work)
        picks.append(pick)
        vals.append(mx)
        idxs.append(idx)
    ex = [jnp.exp(v - vals[0]) for v in vals]
    inv = 1.0 / (ex[0] + ex[1] + ex[2] + ex[3])
    chosen = jnp.zeros(logits.shape, F32)
    for pick in picks:
        chosen = chosen + pick.astype(F32)
    ch = min(RANK_CHUNK, tm)
    r_i = lax.broadcasted_iota(jnp.int32, (ch, ch), 0)
    c_i = lax.broadcasted_iota(jnp.int32, (ch, ch), 1)
    tri = (c_i < r_i).astype(BF16)
    carry = carry_ref[...]
    before = []
    for c in range(tm // ch):
        part = chosen[c * ch:(c + 1) * ch]
        before.append(dot(tri, part.astype(BF16)) + carry)
        carry = carry + jnp.sum(part, axis=0, keepdims=True)
    before = jnp.concatenate(before, axis=0)
    e_out = jnp.zeros((tm, TOP_K), jnp.int32)
    g_out = jnp.zeros((tm, TOP_K), F32)
    r_out = jnp.zeros((tm, TOP_K), jnp.int32)
    for k in range(TOP_K):
        rank = jnp.sum(jnp.where(picks[k], before, 0.0), axis=-1, keepdims=True)
        e_out = jnp.where(col4 == k, idxs[k].astype(jnp.int32), e_out)
        g_out = jnp.where(col4 == k, ex[k] * inv, g_out)
        r_out = jnp.where(col4 == k, rank.astype(jnp.int32), r_out)
    e_ref[...] = e_out
    g_ref[...] = g_out
    r_ref[...] = r_out
    carry_ref[...] = carry
    cnt_ref[...] = carry


def _router(x, w, b):
    t, d = x.shape
    tm = ROW_TILE
    row = lambda i: (i, 0)
    const = lambda i: (0, 0)
    k_spec = pl.BlockSpec((tm, TOP_K), row)
    return pl.pallas_call(
        _router_kernel,
        grid=(t // tm,),
        in_specs=[pl.BlockSpec((tm, d), row), pl.BlockSpec((d, N_EXPERTS), const),
                  pl.BlockSpec((1, N_EXPERTS), const)],
        out_specs=[k_spec, k_spec, k_spec, pl.BlockSpec((1, N_EXPERTS), const)],
        out_shape=[jax.ShapeDtypeStruct((t, TOP_K), jnp.int32), jax.ShapeDtypeStruct((t, TOP_K), F32),
                   jax.ShapeDtypeStruct((t, TOP_K), jnp.int32), jax.ShapeDtypeStruct((1, N_EXPERTS), F32)],
        scratch_shapes=[pltpu.VMEM((1, N_EXPERTS), F32)],
        compiler_params=_cparams(("arbitrary",), 32 << 20),
        name="router_topk",
    )(x, w, b)


def _route_plan(e, rank, counts, bm, tm):
    t = e.shape[0]
    nb = TOP_K * t // bm + N_EXPERTS
    counts = counts.reshape(N_EXPERTS).astype(jnp.int32)
    padded = (counts + bm - 1) // bm * bm
    pend = jnp.cumsum(padded)
    pstart = pend - padded
    onehot = e[..., None] == jnp.arange(N_EXPERTS, dtype=jnp.int32)
    dest = jnp.sum(jnp.where(onehot, pstart, 0), axis=-1) + rank
    blk_row0 = jnp.arange(nb, dtype=jnp.int32) * bm
    blk_exp = jnp.minimum(jnp.sum((pend[None, :] <= blk_row0[:, None]).astype(jnp.int32), axis=1),
                          N_EXPERTS - 1)
    n_used = pend[-1:] // bm
    return dest.reshape(t // tm, tm * TOP_K), blk_exp, n_used, pstart + counts, pend, padded // bm


ROW_SUBLANES = D_MODEL // LANES


def _store_row_tiles(ref, base, x):
    for c in range(ROW_SUBLANES):
        ref[pl.ds(base + c, x.shape[0], stride=ROW_SUBLANES), :] = x[:, c * LANES:(c + 1) * LANES]


def _load_row_tiles(ref, base, n):
    return jnp.concatenate([ref[pl.ds(base + c, n, stride=ROW_SUBLANES), :] for c in range(ROW_SUBLANES)], axis=1)


def _tile_rows(ref, row):
    return ref.at[pl.ds(pl.multiple_of(row * ROW_SUBLANES, ROW_SUBLANES), ROW_SUBLANES), :]


def _dispatch_kernel(padlo_ref, pend_ref, nused_ref, dest_hbm, x_ref, xs_hbm, idx0, idx1, pk, zbuf, sems):
    i = pl.program_id(0)
    n = pl.num_programs(0)
    tm = x_ref.shape[0]
    bm = zbuf.shape[0] // ROW_SUBLANES
    dump = xs_hbm.shape[0] // ROW_SUBLANES - bm
    idx = (idx0, idx1)
    slot_rows = tm * ROW_SUBLANES

    def idx_copy(step, s):
        return pltpu.make_async_copy(dest_hbm.at[step], idx[s], sems.at[s])

    def rows_wait(s):
        for _ in range(TOP_K):
            pltpu.make_async_copy(pk.at[pl.ds(s * slot_rows, slot_rows), :], xs_hbm.at[pl.ds(0, slot_rows), :],
                                  sems.at[2 + s]).wait()

    @pl.when(i == 0)
    def _():
        idx_copy(0, 0).start()
        zbuf[...] = jnp.zeros_like(zbuf)

        def zero_block(b, c):
            cp = pltpu.make_async_copy(zbuf, xs_hbm.at[pl.ds(pl.multiple_of(b * zbuf.shape[0], ROW_SUBLANES),
                                                             zbuf.shape[0]), :], sems.at[4])
            cp.start()
            cp.wait()
            return c
        lax.fori_loop(nused_ref[0], dump // bm + 1, zero_block, 0)
        for e in range(N_EXPERTS):
            def zero_row(jj, c, e=e):
                for u in range(DMA_QUEUES):
                    j = jj * DMA_QUEUES + u
                    r = padlo_ref[e] + j
                    r = jnp.where(r < pend_ref[e], r, dump + j)
                    pltpu.make_async_copy(_tile_rows(zbuf, j), _tile_rows(xs_hbm, r), sems.at[4]).start(priority=u)
                return c
            lax.fori_loop(0, bm // DMA_QUEUES, zero_row, 0, unroll=DMA_UNROLL // DMA_QUEUES)
            pltpu.make_async_copy(zbuf, xs_hbm.at[pl.ds(0, zbuf.shape[0]), :], sems.at[4]).wait()

    def step(s):
        idx_copy(i, s).wait()

        @pl.when(i + 1 < n)
        def _():
            idx_copy(i + 1, 1 - s).start()

        @pl.when(i >= 2)
        def _():
            rows_wait(s)

        _store_row_tiles(pk, s * slot_rows, x_ref[...])

        def issue(j, c):
            for k in range(TOP_K):
                pltpu.make_async_copy(_tile_rows(pk, s * tm + j), _tile_rows(xs_hbm, idx[s][j * TOP_K + k]),
                                      sems.at[2 + s]).start(priority=k % DMA_QUEUES)
            return c
        lax.fori_loop(0, tm, issue, 0, unroll=DMA_UNROLL)

        @pl.when(i == n - 1)
        def _():
            rows_wait(s)

        @pl.when(jnp.logical_and(i == n - 1, n >= 2))
        def _():
            rows_wait(1 - s)

    for s in range(2):
        pl.when(i % 2 == s)(functools.partial(step, s))


def _dispatch(x, dest, pad_lo, pend, n_used, bm):
    t, d = x.shape
    nblk, per = dest.shape
    tm = per // TOP_K
    nb = TOP_K * t // bm + N_EXPERTS
    grid_spec = pltpu.PrefetchScalarGridSpec(
        num_scalar_prefetch=3,
        grid=(nblk,),
        in_specs=[pl.BlockSpec(memory_space=pl.ANY), pl.BlockSpec((tm, d), lambda i, a, b, c: (i, 0))],
        out_specs=pl.BlockSpec(memory_space=pl.ANY),
        scratch_shapes=[pltpu.SMEM((per,), jnp.int32), pltpu.SMEM((per,), jnp.int32),
                        pltpu.VMEM((2 * tm * ROW_SUBLANES, LANES), F32),
                        pltpu.VMEM((bm * ROW_SUBLANES, LANES), F32),
                        pltpu.SemaphoreType.DMA((5,))])
    return pl.pallas_call(
        _dispatch_kernel,
        grid_spec=grid_spec,
        out_shape=jax.ShapeDtypeStruct(((nb * bm + bm) * ROW_SUBLANES, LANES), F32),
        compiler_params=_cparams(("arbitrary",), 32 << 20),
        name="moe_dispatch",
    )(pad_lo, pend, n_used, dest, x)


def _expert_kernel(be_ref, nused_ref, gblk_ref, xs_ref, w1_hbm, b1g_ref, b1l_ref, w2_hbm, b2_ref, y_ref,
                   w1gt, w1lt, w2b, tbuf, w1s, w2s, slot_ref, sems, *, layer):
    i = pl.program_id(0)
    d, de = w2b.shape[1], w2b.shape[0]
    nt = (((1,), (1,)), ((), ()))

    def weight_copies(e, s):
        return (pltpu.make_async_copy(w1_hbm.at[layer, e], w1s.at[s], sems.at[0, s]),
                pltpu.make_async_copy(w2_hbm.at[layer, e], w2s.at[s], sems.at[1, s]))

    @pl.when(i < nused_ref[0])
    def _():
        e = be_ref[i]
        changed = jnp.logical_or(i == 0, e != be_ref[jnp.maximum(i - 1, 0)])

        @pl.when(changed)
        def _():
            @pl.when(i == 0)
            def _():
                slot_ref[0] = 0
                for cp in weight_copies(e, 0):
                    cp.start()

            s = slot_ref[0]
            for cp in weight_copies(e, s):
                cp.wait()
            j = i + gblk_ref[e]

            @pl.when(j < nused_ref[0])
            def _():
                for cp in weight_copies(be_ref[jnp.minimum(j, be_ref.shape[0] - 1)], 1 - s):
                    cp.start()

            for a in range(d // LANES):
                cols = slice(a * LANES, (a + 1) * LANES)
                tbuf[...] = w1s[s, cols, :].T
                w1gt[:, cols] = tbuf[pl.ds(0, de, stride=2), :].astype(BF16)
                w1lt[:, cols] = tbuf[pl.ds(1, de, stride=2), :].astype(BF16)
            w2b[...] = w2s[s].astype(BF16)
            slot_ref[0] = 1 - s

        bm = xs_ref.shape[0] // ROW_SUBLANES
        x = _load_row_tiles(xs_ref, 0, bm).astype(BF16)
        hg = lax.dot_general(x, w1gt[...], nt, preferred_element_type=F32) + b1g_ref[...]
        hl = lax.dot_general(x, w1lt[...], nt, preferred_element_type=F32) + b1l_ref[...]
        hg = jnp.minimum(hg, SWIGLU_LIMIT)
        hl = jnp.clip(hl, -SWIGLU_LIMIT, SWIGLU_LIMIT)
        act = hg * (1.0 / (1.0 + jnp.exp(-SWIGLU_ALPHA * hg))) * (hl + 1.0)
        y = jnp.dot(act.astype(BF16), w2b[...], preferred_element_type=F32) + b2_ref[...]
        _store_row_tiles(y_ref, 0, y)

    @pl.when(i >= nused_ref[0])
    def _():
        y_ref[...] = jnp.zeros_like(y_ref)


def _experts(xs, blk_exp, n_used, grp_blocks, layer, w1, b1g, b1l, w2, b2, bm):
    d = ROW_SUBLANES * LANES
    nb = blk_exp.shape[0]
    de = w2.shape[2]
    blk = (bm * ROW_SUBLANES, LANES)
    bmap = lambda i, be, nu, gb: (layer, be[i], 0, 0)
    any_spec = pl.BlockSpec(memory_space=pl.ANY)
    grid_spec = pltpu.PrefetchScalarGridSpec(
        num_scalar_prefetch=3,
        grid=(nb,),
        in_specs=[pl.BlockSpec(blk, lambda i, be, nu, gb: (jnp.minimum(i, nu[0] - 1), 0)),
                  any_spec,
                  pl.BlockSpec((None, None, 1, de), bmap), pl.BlockSpec((None, None, 1, de), bmap),
                  any_spec,
                  pl.BlockSpec((None, None, 1, d), bmap)],
        out_specs=pl.BlockSpec(blk, lambda i, be, nu, gb: (i, 0)),
        scratch_shapes=[pltpu.VMEM((de, d), BF16), pltpu.VMEM((de, d), BF16), pltpu.VMEM((de, d), BF16),
                        pltpu.VMEM((2 * de, LANES), F32),
                        pltpu.VMEM((2, d, 2 * de), F32), pltpu.VMEM((2, de, d), F32),
                        pltpu.SMEM((1,), jnp.int32), pltpu.SemaphoreType.DMA((2, 2))])
    return pl.pallas_call(
        functools.partial(_expert_kernel, layer=layer),
        grid_spec=grid_spec,
        out_shape=jax.ShapeDtypeStruct((nb * blk[0], LANES), F32),
        compiler_params=_cparams(("arbitrary",), 54 << 20),
        name="moe_experts",
    )(blk_exp, n_used, grp_blocks, xs, w1, b1g, b1l, w2, b2)


def _combine_kernel(dest_hbm, gate_ref, x_ref, g_ref, b_ref, y_hbm, o_ref, idx0, idx1, ybuf, sems, *, blk_off):
    i = pl.program_id(0)
    n = pl.num_programs(0)
    tm = x_ref.shape[0]
    idx = (idx0, idx1)
    slab_rows = tm * ROW_SUBLANES

    def slab(s, k):
        return (s * TOP_K + k) * tm

    def idx_copy(step, s):
        return pltpu.make_async_copy(dest_hbm.at[blk_off + step], idx[s], sems.at[s])

    def issue_rows(s):
        def body(j, c):
            for k in range(TOP_K):
                pltpu.make_async_copy(_tile_rows(y_hbm, idx[s][j * TOP_K + k]), _tile_rows(ybuf, slab(s, k) + j),
                                      sems.at[2 + s]).start(priority=k % DMA_QUEUES)
            return c
        lax.fori_loop(0, tm, body, 0, unroll=DMA_UNROLL)

    @pl.when(i == 0)
    def _():
        cp = idx_copy(0, 0)
        cp.start()
        cp.wait()
        issue_rows(0)

    @pl.when(jnp.logical_and(i == 0, n >= 2))
    def _():
        idx_copy(1, 1).start()

    def step(s):
        @pl.when(i + 1 < n)
        def _():
            idx_copy(i + 1, 1 - s).wait()
            issue_rows(1 - s)

        @pl.when(i + 2 < n)
        def _():
            idx_copy(i + 2, s).start()

        for k in range(TOP_K):
            pltpu.make_async_copy(y_hbm.at[pl.ds(0, slab_rows), :],
                                  ybuf.at[pl.ds(slab(s, k) * ROW_SUBLANES, slab_rows), :], sems.at[2 + s]).wait()
        gate = gate_ref[...]
        h = None
        for k in range(TOP_K):
            term = gate[:, k:k + 1] * _load_row_tiles(ybuf, slab(s, k) * ROW_SUBLANES, tm)
            h = term if h is None else h + term
        o_ref[...] = _layer_norm(DEEPNORM_ALPHA * x_ref[...] + h, g_ref[...], b_ref[...])

    for s in range(2):
        pl.when(i % 2 == s)(functools.partial(step, s))


def _combine(y, dest, gate, x, ln_g, ln_b, row_lo, rows):
    _, d = x.shape
    per = dest.shape[1]
    tm = per // TOP_K
    off = row_lo // tm
    vec = pl.BlockSpec((1, d), lambda i: (0, 0))
    return pl.pallas_call(
        functools.partial(_combine_kernel, blk_off=off),
        grid=(rows // tm,),
        in_specs=[pl.BlockSpec(memory_space=pl.ANY),
                  pl.BlockSpec((tm, TOP_K), lambda i: (off + i, 0)),
                  pl.BlockSpec((tm, d), lambda i: (off + i, 0)), vec, vec,
                  pl.BlockSpec(memory_space=pl.ANY)],
        out_specs=pl.BlockSpec((tm, d), lambda i: (i, 0)),
        out_shape=jax.ShapeDtypeStruct((rows, d), F32),
        scratch_shapes=[pltpu.SMEM((per,), jnp.int32), pltpu.SMEM((per,), jnp.int32),
                        pltpu.VMEM((2 * TOP_K * tm * ROW_SUBLANES, LANES), F32),
                        pltpu.SemaphoreType.DMA((4,))],
        compiler_params=_cparams(("arbitrary",), 40 << 20),
        name="moe_combine_ln",
    )(dest, gate, x, ln_g, ln_b, y)


def _moe_block(x, layer, router_w, router_b, w1, b1g, b1l, w2, b2):
    e, gate, rank, counts = _router(x, router_w, router_b)
    dest, blk_exp, n_used, pad_lo, pend, grp_blocks = _route_plan(e, rank, counts, EXPERT_ROWS, ROW_TILE)
    xs = _dispatch(x, dest, pad_lo, pend, n_used, EXPERT_ROWS)
    y = _experts(xs, blk_exp, n_used, grp_blocks, layer, w1, b1g, b1l, w2, b2, EXPERT_ROWS)
    return y, dest, gate


def _gelu(z):
    return 0.5 * z * (1.0 + jnp.tanh(0.7978845608028654 * (z + 0.044715 * (z * z * z))))


def _sgu_kernel(x_ref, win_ref, bin_ref, ng_ref, nb_ref, ws_ref, bs_ref, wout_ref, bout_ref,
                g_ref, b_ref, y_ref):
    tm = x_ref.shape[0]
    x = x_ref[...]
    xb = x.astype(BF16)
    dot = functools.partial(jnp.dot, preferred_element_type=F32)
    v = _gelu(dot(xb, win_ref[:, SGU_HALF:]) + bin_ref[:, SGU_HALF:])
    v = _layer_norm(v, ng_ref[...], nb_ref[...]).astype(BF16)
    acc = jnp.zeros((tm, D_MODEL), F32)
    for g in range(SGU_GROUPS):
        cols = slice(g * SGU_GROUP_DIM, (g + 1) * SGU_GROUP_DIM)
        u = _gelu(dot(xb, win_ref[:, cols]) + bin_ref[:, cols])
        mixed = [dot(ws_ref[g], v[c * SGU_CHUNK:(c + 1) * SGU_CHUNK, cols]) + bs_ref[:, g:g + 1]
                 for c in range(tm // SGU_CHUNK)]
        mixed = jnp.concatenate(mixed, axis=0)
        acc = acc + dot((u * mixed).astype(BF16), wout_ref[cols, :])
    h = acc + bout_ref[...]
    y_ref[...] = _layer_norm(DEEPNORM_ALPHA * x + h, g_ref[...], b_ref[...])


def _sgu_block(x, w_in, b_in, norm_g, norm_b, w_s, b_s_t, w_out, b_out, ln_g, ln_b):
    t, d = x.shape
    tm = SGU_ROWS
    row = lambda i: (i, 0)
    c2 = lambda i: (0, 0)
    c3 = lambda i: (0, 0, 0)
    once = dict(pipeline_mode=pl.Buffered(1))
    vec = pl.BlockSpec((1, d), c2)
    return pl.pallas_call(
        _sgu_kernel,
        grid=(t // tm,),
        in_specs=[pl.BlockSpec((tm, d), row),
                  pl.BlockSpec((d, 2 * SGU_HALF), c2, **once), pl.BlockSpec((1, 2 * SGU_HALF), c2),
                  pl.BlockSpec((1, SGU_HALF), c2), pl.BlockSpec((1, SGU_HALF), c2),
                  pl.BlockSpec((SGU_GROUPS, SGU_CHUNK, SGU_CHUNK), c3),
                  pl.BlockSpec((SGU_CHUNK, SGU_GROUPS), c2),
                  pl.BlockSpec((SGU_HALF, d), c2, **once), vec, vec, vec],
        out_specs=pl.BlockSpec((tm, d), row),
        out_shape=jax.ShapeDtypeStruct((t, d), F32),
        compiler_params=_cparams(("parallel",), 52 << 20),
        name="sgu_block",
    )(x, w_in, b_in, norm_g, norm_b, w_s, b_s_t, w_out, b_out, ln_g, ln_b)


def _attention_block(x, views, w_qkv, w_o, b_o, ln_g, ln_b, seq_rows):
    n_groups = N_GROUPS * HEADS
    slopes = (2.0 ** (-8.0 * jnp.arange(1, n_groups + 1, dtype=F32) / n_groups)).reshape(N_GROUPS, HEADS)
    scale = jnp.concatenate([jnp.full((GROUP_DIM,), HEAD_DIM ** -0.5, F32), jnp.ones((2 * GROUP_DIM,), F32)])
    os_, lses = [], []
    for g, (_, dil) in enumerate(DIL_CONFIGS):
        cols = [w_qkv[:, (j * N_GROUPS + g) * GROUP_DIM:(j * N_GROUPS + g + 1) * GROUP_DIM] for j in range(3)]
        w_g = (jnp.concatenate(cols, axis=1) * scale).astype(BF16)
        qkv = _qkv_project(views[dil], w_g, dil)
        segs = tuple((n // dil, length // dil) for n, length in seq_rows)
        o, lse = _attention(qkv, _band_bias(slopes[g], dil), dil, segs)
        os_.append(o)
        lses.append(lse)
    dils = tuple(dil for _, dil in DIL_CONFIGS)
    return _merge_project(os_, lses, dils, x, w_o.astype(BF16), b_o[None], ln_g[None], ln_b[None])


def kernel(x_prompt, x_sample, attn_w_qkv, attn_w_o, attn_b_o, sgu_w_in, sgu_b_in, sgu_norm_g, sgu_norm_b,
           sgu_w_s, sgu_b_s, sgu_w_out, sgu_b_out, router_w, router_b, exp_w1, exp_b1, exp_w2, exp_b2,
           ln_mix_g, ln_mix_b, ln_ffn_g, ln_ffn_b):
    d = x_prompt.shape[-1]
    tp = x_prompt.shape[0] * x_prompt.shape[1]
    ts = x_sample.shape[0] * x_sample.shape[1]
    seq_rows = ((tp, x_prompt.shape[1]), (ts, x_sample.shape[1]))
    dils = tuple(dil for _, dil in DIL_CONFIGS if dil > 1)
    x, *xviews = _token_views(x_prompt.reshape(tp, d), x_sample.reshape(ts, d), dils)
    views = dict(zip(dils, xviews))
    views[1] = x

    b1g = exp_b1[:, :, None, 0::2]
    b1l = exp_b1[:, :, None, 1::2]
    b2 = exp_b2[:, :, None, :]

    outs = None
    for i in range(DEPTH):
        j = i // 2
        if i % 2 == 0:
            x = _attention_block(x, views, attn_w_qkv[j], attn_w_o[j], attn_b_o[j], ln_mix_g[i], ln_mix_b[i],
                                 seq_rows)
        else:
            x = _sgu_block(x, sgu_w_in[j].astype(BF16), sgu_b_in[j][None], sgu_norm_g[j][None], sgu_norm_b[j][None],
                           sgu_w_s[j].astype(BF16), jnp.transpose(sgu_b_s[j]), sgu_w_out[j].astype(BF16),
                           sgu_b_out[j][None], ln_mix_g[i][None], ln_mix_b[i][None])
        y, dest, gate = _moe_block(x, i, router_w[i], router_b[i][None], exp_w1, b1g, b1l, exp_w2, b2)
        g_ln, b_ln = ln_ffn_g[i][None], ln_ffn_b[i][None]
        if i + 1 < DEPTH:
            x = _combine(y, dest, gate, x, g_ln, b_ln, 0, tp + ts)
        else:
            outs = (_combine(y, dest, gate, x, g_ln, b_ln, 0, tp).reshape(x_prompt.shape),
                    _combine(y, dest, gate, x, g_ln, b_ln, tp, ts).reshape(x_sample.shape))
    return outs
```

```python
import functools

import jax
import jax.numpy as jnp
from jax import lax
from jax.experimental import pallas as pl
from jax.experimental.pallas import tpu as pltpu

F32 = jnp.float32
BF16 = jnp.bfloat16

LANES = 128
SUBLANES = 8
VMEM_LIMIT_CAP = 56 * 1024 * 1024

D_MODEL = 1024
DIL_CONFIGS = ((128, 1), (512, 4), (2048, 16))
N_GROUPS = len(DIL_CONFIGS)
HEADS = 8
HEAD_DIM = 64
GROUP_DIM = HEADS * HEAD_DIM
N_SIDE = 64
Q_BLOCK = 128
KV_BLOCK = N_SIDE
KV_WIN = Q_BLOCK + 2 * N_SIDE
ATTN_Q_STEPS = (8, 4, 2, 1)
SGU_CHUNK = 128
SGU_HALF = 3 * D_MODEL
SGU_GROUPS = 8
SGU_GROUP_DIM = SGU_HALF // SGU_GROUPS
N_EXPERTS = 32
TOP_K = 4
D_EXPERT = D_MODEL
SWIGLU_ALPHA = 1.702
SWIGLU_LIMIT = 7.0
LN_EPS = 1e-5
DEPTH = 2
DEEPNORM_ALPHA = (2 * DEPTH) ** 0.25
NEG = -1e30

ROW_TILE = 512
QKV_ROWS = 1024
RANK_CHUNK = 256
SGU_ROWS = 512
EXPERT_ROWS = 512
DMA_UNROLL = 8
DMA_QUEUES = 2


def _cparams(semantics, vmem_bytes):
    return pltpu.CompilerParams(dimension_semantics=semantics,
                                vmem_limit_bytes=min(int(vmem_bytes), VMEM_LIMIT_CAP))


def _layer_norm(y, g, b):
    mu = jnp.mean(y, axis=-1, keepdims=True)
    yc = y - mu
    var = jnp.mean(yc * yc, axis=-1, keepdims=True)
    return yc * lax.rsqrt(var + LN_EPS) * g + b


def _views_kernel(xa_ref, xb_ref, *refs, na, dils):
    x_ref, view_refs, slab = refs[0], refs[1:-1], refs[-1]
    tm, d = x_ref.shape
    x = jnp.where(pl.program_id(0) < na, xa_ref[...], xb_ref[...])
    x_ref[...] = x
    for c in range(d // LANES):
        slab[c] = x[:, c * LANES:(c + 1) * LANES]
    for dil, v_ref in zip(dils, view_refs):
        for r in range(dil):
            for c in range(d // LANES):
                v_ref[:, r * d + c * LANES:r * d + (c + 1) * LANES] = slab[c, pl.ds(r, tm // dil, stride=dil), :]


def _token_views(xa, xb, dils):
    ta, d = xa.shape
    tb = xb.shape[0]
    t = ta + tb
    tm = ROW_TILE
    na, nb = ta // tm, tb // tm
    out_shapes = [jax.ShapeDtypeStruct((t, d), F32)] + [jax.ShapeDtypeStruct((t // dil, dil * d), F32) for dil in dils]
    out_specs = [pl.BlockSpec((tm, d), lambda i: (i, 0))] + [pl.BlockSpec((tm // dil, dil * d), lambda i: (i, 0))
                                                             for dil in dils]
    return pl.pallas_call(
        functools.partial(_views_kernel, na=na, dils=dils),
        grid=(na + nb,),
        in_specs=[pl.BlockSpec((tm, d), lambda i: (jnp.minimum(i, na - 1), 0)),
                  pl.BlockSpec((tm, d), lambda i: (jnp.maximum(i - na, 0), 0))],
        out_specs=out_specs,
        out_shape=out_shapes,
        scratch_shapes=[pltpu.VMEM((d // LANES, tm, LANES), F32)],
        compiler_params=_cparams(("arbitrary",), 40 << 20),
        name="token_views",
    )(xa, xb)


def _qkv_kernel(x_ref, w_ref, o_ref):
    o_ref[...] = jnp.dot(x_ref[...].astype(BF16), w_ref[...],
                         preferred_element_type=F32).astype(BF16)


def _qkv_project(xv, w, dil):
    n = xv.shape[0]
    d = xv.shape[1] // dil
    tm = min(QKV_ROWS, n)
    width = w.shape[1]
    return pl.pallas_call(
        _qkv_kernel,
        grid=(dil, n // tm),
        in_specs=[pl.BlockSpec((tm, d), lambda r, i: (i, r)),
                  pl.BlockSpec((d, width), lambda r, i: (0, 0))],
        out_specs=pl.BlockSpec((None, tm, width), lambda r, i: (r, i, 0)),
        out_shape=jax.ShapeDtypeStruct((dil, n, width), BF16),
        compiler_params=_cparams(("parallel", "parallel"),
                                 2 * (tm * d * 4 + d * width * 2 + tm * width * 2) + tm * width * 4
                                 + (8 << 20)),
        name=f"qkv_project_d{dil}",
    )(xv, w)


def _attn_kernel(q_ref, kp_ref, kb_ref, kn_ref, vp_ref, vb_ref, vn_ref, bias_ref, o_ref, lse_ref, *, segs):
    (n_a, len_a), (_, len_b) = segs
    step_rows = q_ref.shape[0]
    lane = lax.broadcasted_iota(jnp.int32, (1, LANES), 1)
    first_head = lane < HEAD_DIM
    nt = (((1,), (1,)), ((), ()))

    def window(prev, body, nxt, first, sl):
        parts = []
        if first < 0:
            parts.append(prev[:, sl])
        parts.append(body[max(first, 0):min(first + KV_WIN, step_rows), sl])
        if first + KV_WIN > step_rows:
            parts.append(nxt[:, sl])
        return jnp.concatenate(parts, axis=0)

    for sub in range(step_rows // Q_BLOCK):
        rows = slice(sub * Q_BLOCK, (sub + 1) * Q_BLOCK)
        row0 = pl.program_id(1) * step_rows + sub * Q_BLOCK
        in_a = row0 < n_a
        lo = jnp.where(in_a, (row0 // len_a) * len_a, n_a + ((row0 - n_a) // len_b) * len_b)
        hi = lo + jnp.where(in_a, len_a, len_b)
        krow = lax.broadcasted_iota(jnp.int32, (1, KV_WIN), 1) + (row0 - N_SIDE)
        col_bias = jnp.where((krow >= lo) & (krow < hi), 0.0, NEG).astype(F32)
        lse_tile = jnp.zeros((Q_BLOCK, LANES), F32)
        for p in range(HEADS // 2):
            sl = slice(p * LANES, (p + 1) * LANES)
            q = q_ref[rows, sl]
            k = window(kp_ref, kb_ref, kn_ref, sub * Q_BLOCK - N_SIDE, sl)
            v = window(vp_ref, vb_ref, vn_ref, sub * Q_BLOCK - N_SIDE, sl)
            zero = jnp.zeros_like(q)
            outs, inv_ls = [], []
            for j, qh in enumerate((jnp.where(first_head, q, zero), jnp.where(first_head, zero, q))):
                h = 2 * p + j
                s = lax.dot_general(qh, k, nt, preferred_element_type=F32)
                s = s + bias_ref[h] + col_bias
                m = jnp.max(s, axis=-1, keepdims=True)
                e = jnp.exp(s - m)
                l = jnp.sum(e, axis=-1, keepdims=True)
                outs.append(jnp.dot(e.astype(BF16), v, preferred_element_type=F32))
                inv_ls.append(1.0 / l)
                lse_tile = jnp.where(lane == h, m + jnp.log(l), lse_tile)
            o = jnp.where(first_head, outs[0] * inv_ls[0], outs[1] * inv_ls[1])
            o_ref[rows, sl] = o.astype(BF16)
        lse_ref[rows, :] = lse_tile


def _band_bias(slopes, dil):
    qi = jnp.arange(Q_BLOCK)[:, None]
    kc = jnp.arange(KV_WIN)[None, :]
    dist = jnp.abs(kc - N_SIDE - qi)
    pen = -slopes[:, None, None] * (dist * dil).astype(F32)[None]
    return jnp.where((dist <= N_SIDE)[None], pen, NEG).astype(F32)


def _attention(qkv, bias, dil, segs):
    _, n, _ = qkv.shape
    nkb = n // KV_BLOCK
    q_step = max(s for s in ATTN_Q_STEPS if (n // Q_BLOCK) % s == 0)
    step_rows = q_step * Q_BLOCK
    per_step = step_rows // KV_BLOCK

    def body_spec(col):
        return pl.BlockSpec((None, step_rows, GROUP_DIM), lambda r, i: (r, i, col))

    def edge_spec(col, after):
        def imap(r, i):
            return (r, jnp.clip((i + after) * per_step - 1 + after, 0, nkb - 1), col)
        return pl.BlockSpec((None, KV_BLOCK, GROUP_DIM), imap)

    in_specs = ([body_spec(0)]
                + [edge_spec(1, 0), body_spec(1), edge_spec(1, 1)]
                + [edge_spec(2, 0), body_spec(2), edge_spec(2, 1)]
                + [pl.BlockSpec((HEADS, Q_BLOCK, KV_WIN), lambda r, i: (0, 0, 0))])
    return pl.pallas_call(
        functools.partial(_attn_kernel, segs=segs),
        grid=(dil, n // step_rows),
        in_specs=in_specs,
        out_specs=[pl.BlockSpec((step_rows, GROUP_DIM), lambda r, i: (i, r)),
                   pl.BlockSpec((step_rows, LANES), lambda r, i: (i, r))],
        out_shape=[jax.ShapeDtypeStruct((n, dil * GROUP_DIM), BF16),
                   jax.ShapeDtypeStruct((n, dil * LANES), F32)],
        compiler_params=_cparams(("parallel", "parallel"), 32 << 20),
        name=f"band_attention_d{dil}",
    )(*([qkv] * 7), bias)


def _merge_kernel(o0, o1, o2, l0, l1, l2, x_ref, wo_ref, bo_ref, g_ref, b_ref, y_ref, o_tok, l_tok, *, dils):
    tm = x_ref.shape[0]
    n_slabs = GROUP_DIM // LANES
    for g, (dil, o_ref, l_ref) in enumerate(zip(dils, (o0, o1, o2), (l0, l1, l2))):
        for r in range(dil):
            rows = pl.ds(r, tm // dil, stride=dil)
            l_tok[g, rows, :] = l_ref[:, r * LANES:(r + 1) * LANES]
            for c in range(n_slabs):
                lo = r * GROUP_DIM + c * LANES
                o_tok[g, c, rows, :] = o_ref[:, lo:lo + LANES].astype(F32)
    lses = [l_tok[g] for g in range(len(dils))]
    m = jnp.maximum(jnp.maximum(lses[0], lses[1]), lses[2])
    es = [jnp.exp(l - m) for l in lses]
    inv = 1.0 / (es[0] + es[1] + es[2])
    ws = [e * inv for e in es]
    lane = lax.broadcasted_iota(jnp.int32, (1, LANES), 1)
    first_head = lane < HEAD_DIM
    parts = []
    for p in range(HEADS // 2):
        acc = None
        for g, w in enumerate(ws):
            wp = jnp.where(first_head, w[:, 2 * p:2 * p + 1], w[:, 2 * p + 1:2 * p + 2])
            term = wp * o_tok[g, p]
            acc = term if acc is None else acc + term
        parts.append(acc)
    merged = jnp.concatenate(parts, axis=-1).astype(BF16)
    h = jnp.dot(merged, wo_ref[...], preferred_element_type=F32) + bo_ref[...]
    y_ref[...] = _layer_norm(DEEPNORM_ALPHA * x_ref[...] + h, g_ref[...], b_ref[...])


def _merge_project(os_, lses, dils, x, w_o, b_o, ln_g, ln_b):
    t, d = x.shape
    tm = ROW_TILE
    row = lambda i: (i, 0)
    const = lambda i: (0, 0)
    vec = pl.BlockSpec((1, d), const)
    return pl.pallas_call(
        functools.partial(_merge_kernel, dils=dils),
        grid=(t // tm,),
        in_specs=([pl.BlockSpec((tm // dil, dil * GROUP_DIM), row) for dil in dils]
                  + [pl.BlockSpec((tm // dil, dil * LANES), row) for dil in dils]
                  + [pl.BlockSpec((tm, d), row), pl.BlockSpec((GROUP_DIM, d), const), vec, vec, vec]),
        out_specs=pl.BlockSpec((tm, d), row),
        out_shape=jax.ShapeDtypeStruct((t, d), F32),
        scratch_shapes=[pltpu.VMEM((len(dils), GROUP_DIM // LANES, tm, LANES), F32),
                        pltpu.VMEM((len(dils), tm, LANES), F32)],
        compiler_params=_cparams(("parallel",), 40 << 20),
        name="merge_outproj_ln",
    )(*os_, *lses, x, w_o, b_o, ln_g, ln_b)


def _split_bf16(a):
    hi = a.astype(BF16)
    return hi, (a - hi.astype(F32)).astype(BF16)


def _router_kernel(x_ref, w_ref, b_ref, e_ref, g_ref, r_ref, cnt_ref, carry_ref):
    tm = x_ref.shape[0]

    @pl.when(pl.program_id(0) == 0)
    def _():
        carry_ref[...] = jnp.zeros_like(carry_ref)

    xh, xl = _split_bf16(x_ref[...])
    wh = w_ref[...].astype(BF16)
    dot = functools.partial(jnp.dot, preferred_element_type=F32)
    logits = dot(xh, wh) + dot(xl, wh) + b_ref[...]
    lane = lax.broadcasted_iota(jnp.int32, logits.shape, 1).astype(F32)
    col4 = lax.broadcasted_iota(jnp.int32, (tm, TOP_K), 1)
    work = logits
    picks, vals, idxs = [], [], []
    for _k in range(TOP_K):
        mx = jnp.max(work, axis=-1, keepdims=True)
        idx = jnp.min(jnp.where(work == mx, lane, float(N_EXPERTS)), axis=-1, keepdims=True)
        pick = lane == idx
        work = jnp.where(pick, -jnp.inf, work)
        picks.append(pick)
        vals.append(mx)
        idxs.append(idx)
    ex = [jnp.exp(v - vals[0]) for v in vals]
    inv = 1.0 / (ex[0] + ex[1] + ex[2] + ex[3])
    chosen = jnp.zeros(logits.shape, F32)
    for pick in picks:
        chosen = chosen + pick.astype(F32)
    ch = min(RANK_CHUNK, tm)
    r_i = lax.broadcasted_iota(jnp.int32, (ch, ch), 0)
    c_i = lax.broadcasted_iota(jnp.int32, (ch, ch), 1)
    tri = (c_i < r_i).astype(BF16)
    carry = carry_ref[...]
    before = []
    for c in range(tm // ch):
        part = chosen[c * ch:(c + 1) * ch]
        before.append(dot(tri, part.astype(BF16)) + carry)
        carry = carry + jnp.sum(part, axis=0, keepdims=True)
    before = jnp.concatenate(before, axis=0)
    e_out = jnp.zeros((tm, TOP_K), jnp.int32)
    g_out = jnp.zeros((tm, TOP_K), F32)
    r_out = jnp.zeros((tm, TOP_K), jnp.int32)
    for k in range(TOP_K):
        rank = jnp.sum(jnp.where(picks[k], before, 0.0), axis=-1, keepdims=True)
        e_out = jnp.where(col4 == k, idxs[k].astype(jnp.int32), e_out)
        g_out = jnp.where(col4 == k, ex[k] * inv, g_out)
        r_out = jnp.where(col4 == k, rank.astype(jnp.int32), r_out)
    e_ref[...] = e_out
    g_ref[...] = g_out
    r_ref[...] = r_out
    carry_ref[...] = carry
    cnt_ref[...] = carry


def _router(x, w, b):
    t, d = x.shape
    tm = ROW_TILE
    row = lambda i: (i, 0)
    const = lambda i: (0, 0)
    k_spec = pl.BlockSpec((tm, TOP_K), row)
    return pl.pallas_call(
        _router_kernel,
        grid=(t // tm,),
        in_specs=[pl.BlockSpec((tm, d), row), pl.BlockSpec((d, N_EXPERTS), const),
                  pl.BlockSpec((1, N_EXPERTS), const)],
        out_specs=[k_spec, k_spec, k_spec, pl.BlockSpec((1, N_EXPERTS), const)],
        out_shape=[jax.ShapeDtypeStruct((t, TOP_K), jnp.int32), jax.ShapeDtypeStruct((t, TOP_K), F32),
                   jax.ShapeDtypeStruct((t, TOP_K), jnp.int32), jax.ShapeDtypeStruct((1, N_EXPERTS), F32)],
        scratch_shapes=[pltpu.VMEM((1, N_EXPERTS), F32)],
        compiler_params=_cparams(("arbitrary",), 32 << 20),
        name="router_topk",
    )(x, w, b)


def _route_plan(e, rank, counts, bm, tm):
    t = e.shape[0]
    nb = TOP_K * t // bm + N_EXPERTS
    counts = counts.reshape(N_EXPERTS).astype(jnp.int32)
    padded = (counts + bm - 1) // bm * bm
    pend = jnp.cumsum(padded)
    pstart = pend - padded
    onehot = e[..., None] == jnp.arange(N_EXPERTS, dtype=jnp.int32)
    dest = jnp.sum(jnp.where(onehot, pstart, 0), axis=-1) + rank
    blk_row0 = jnp.arange(nb, dtype=jnp.int32) * bm
    blk_exp = jnp.minimum(jnp.sum((pend[None, :] <= blk_row0[:, None]).astype(jnp.int32), axis=1),
                          N_EXPERTS - 1)
    n_used = pend[-1:] // bm
    return dest.reshape(t // tm, tm * TOP_K), blk_exp, n_used, pstart + counts, pend, padded // bm


ROW_SUBLANES = D_MODEL // LANES


def _store_row_tiles(ref, base, x):
    for c in range(ROW_SUBLANES):
        ref[pl.ds(base + c, x.shape[0], stride=ROW_SUBLANES), :] = x[:, c * LANES:(c + 1) * LANES]


def _load_row_tiles(ref, base, n):
    return jnp.concatenate([ref[pl.ds(base + c, n, stride=ROW_SUBLANES), :] for c in range(ROW_SUBLANES)], axis=1)


def _tile_rows(ref, row):
    return ref.at[pl.ds(pl.multiple_of(row * ROW_SUBLANES, ROW_SUBLANES), ROW_SUBLANES), :]


def _dispatch_kernel(padlo_ref, pend_ref, nused_ref, dest_hbm, x_ref, xs_hbm, idx0, idx1, pk, zbuf, sems):
    i = pl.program_id(0)
    n = pl.num_programs(0)
    tm = x_ref.shape[0]
    bm = zbuf.shape[0] // ROW_SUBLANES
    dump = xs_hbm.shape[0] // ROW_SUBLANES - bm
    idx = (idx0, idx1)
    slot_rows = tm * ROW_SUBLANES

    def idx_copy(step, s):
        return pltpu.make_async_copy(dest_hbm.at[step], idx[s], sems.at[s])

    def rows_wait(s):
        for _ in range(TOP_K):
            pltpu.make_async_copy(pk.at[pl.ds(s * slot_rows, slot_rows), :], xs_hbm.at[pl.ds(0, slot_rows), :],
                                  sems.at[2 + s]).wait()

    @pl.when(i == 0)
    def _():
        idx_copy(0, 0).start()
        zbuf[...] = jnp.zeros_like(zbuf)

        def zero_block(b, c):
            cp = pltpu.make_async_copy(zbuf, xs_hbm.at[pl.ds(pl.multiple_of(b * zbuf.shape[0], ROW_SUBLANES),
                                                             zbuf.shape[0]), :], sems.at[4])
            cp.start()
            cp.wait()
            return c
        lax.fori_loop(nused_ref[0], dump // bm + 1, zero_block, 0)
        for e in range(N_EXPERTS):
            def zero_row(jj, c, e=e):
                for u in range(DMA_QUEUES):
                    j = jj * DMA_QUEUES + u
                    r = padlo_ref[e] + j
                    r = jnp.where(r < pend_ref[e], r, dump + j)
                    pltpu.make_async_copy(_tile_rows(zbuf, j), _tile_rows(xs_hbm, r), sems.at[4]).start(priority=u)
                return c
            lax.fori_loop(0, bm // DMA_QUEUES, zero_row, 0, unroll=DMA_UNROLL // DMA_QUEUES)
            pltpu.make_async_copy(zbuf, xs_hbm.at[pl.ds(0, zbuf.shape[0]), :], sems.at[4]).wait()

    def step(s):
        idx_copy(i, s).wait()

        @pl.when(i + 1 < n)
        def _():
            idx_copy(i + 1, 1 - s).start()

        @pl.when(i >= 2)
        def _():
            rows_wait(s)

        _store_row_tiles(pk, s * slot_rows, x_ref[...])

        def issue(j, c):
            for k in range(TOP_K):
                pltpu.make_async_copy(_tile_rows(pk, s * tm + j), _tile_rows(xs_hbm, idx[s][j * TOP_K + k]),
                                      sems.at[2 + s]).start(priority=k % DMA_QUEUES)
            return c
        lax.fori_loop(0, tm, issue, 0, unroll=DMA_UNROLL)

        @pl.when(i == n - 1)
        def _():
            rows_wait(s)

        @pl.when(jnp.logical_and(i == n - 1, n >= 2))
        def _():
            rows_wait(1 - s)

    for s in range(2):
        pl.when(i % 2 == s)(functools.partial(step, s))


def _dispatch(x, dest, pad_lo, pend, n_used, bm):
    t, d = x.shape
    nblk, per = dest.shape
    tm = per // TOP_K
    nb = TOP_K * t // bm + N_EXPERTS
    grid_spec = pltpu.PrefetchScalarGridSpec(
        num_scalar_prefetch=3,
        grid=(nblk,),
        in_specs=[pl.BlockSpec(memory_space=pl.ANY), pl.BlockSpec((tm, d), lambda i, a, b, c: (i, 0))],
        out_specs=pl.BlockSpec(memory_space=pl.ANY),
        scratch_shapes=[pltpu.SMEM((per,), jnp.int32), pltpu.SMEM((per,), jnp.int32),
                        pltpu.VMEM((2 * tm * ROW_SUBLANES, LANES), F32),
                        pltpu.VMEM((bm * ROW_SUBLANES, LANES), F32),
                        pltpu.SemaphoreType.DMA((5,))])
    return pl.pallas_call(
        _dispatch_kernel,
        grid_spec=grid_spec,
        out_shape=jax.ShapeDtypeStruct(((nb * bm + bm) * ROW_SUBLANES, LANES), F32),
        compiler_params=_cparams(("arbitrary",), 32 << 20),
        name="moe_dispatch",
    )(pad_lo, pend, n_used, dest, x)


def _expert_kernel(be_ref, nused_ref, gblk_ref, xs_ref, w1_hbm, b1g_ref, b1l_ref, w2_hbm, b2_ref, y_ref,
                   w1gt, w1lt, w2b, tbuf, w1s, w2s, slot_ref, sems, *, layer):
    i = pl.program_id(0)
    d, de = w2b.shape[1], w2b.shape[0]
    nt = (((1,), (1,)), ((), ()))

    def weight_copies(e, s):
        return (pltpu.make_async_copy(w1_hbm.at[layer, e], w1s.at[s], sems.at[0, s]),
                pltpu.make_async_copy(w2_hbm.at[layer, e], w2s.at[s], sems.at[1, s]))

    @pl.when(i < nused_ref[0])
    def _():
        e = be_ref[i]
        changed = jnp.logical_or(i == 0, e != be_ref[jnp.maximum(i - 1, 0)])

        @pl.when(changed)
        def _():
            @pl.when(i == 0)
            def _():
                slot_ref[0] = 0
                for cp in weight_copies(e, 0):
                    cp.start()

            s = slot_ref[0]
            for cp in weight_copies(e, s):
                cp.wait()
            j = i + gblk_ref[e]

            @pl.when(j < nused_ref[0])
            def _():
                for cp in weight_copies(be_ref[jnp.minimum(j, be_ref.shape[0] - 1)], 1 - s):
                    cp.start()

            for a in range(d // LANES):
                cols = slice(a * LANES, (a + 1) * LANES)
                tbuf[...] = w1s[s, cols, :].T
                w1gt[:, cols] = tbuf[pl.ds(0, de, stride=2), :].astype(BF16)
                w1lt[:, cols] = tbuf[pl.ds(1, de, stride=2), :].astype(BF16)
            w2b[...] = w2s[s].astype(BF16)
            slot_ref[0] = 1 - s

        bm = xs_ref.shape[0] // ROW_SUBLANES
        x = _load_row_tiles(xs_ref, 0, bm).astype(BF16)
        hg = lax.dot_general(x, w1gt[...], nt, preferred_element_type=F32) + b1g_ref[...]
        hl = lax.dot_general(x, w1lt[...], nt, preferred_element_type=F32) + b1l_ref[...]
        hg = jnp.minimum(hg, SWIGLU_LIMIT)
        hl = jnp.clip(hl, -SWIGLU_LIMIT, SWIGLU_LIMIT)
        act = hg * (1.0 / (1.0 + jnp.exp(-SWIGLU_ALPHA * hg))) * (hl + 1.0)
        y = jnp.dot(act.astype(BF16), w2b[...], preferred_element_type=F32) + b2_ref[...]
        _store_row_tiles(y_ref, 0, y)

    @pl.when(i >= nused_ref[0])
    def _():
        y_ref[...] = jnp.zeros_like(y_ref)


def _experts(xs, blk_exp, n_used, grp_blocks, layer, w1, b1g, b1l, w2, b2, bm):
    d = ROW_SUBLANES * LANES
    nb = blk_exp.shape[0]
    de = w2.shape[2]
    blk = (bm * ROW_SUBLANES, LANES)
    bmap = lambda i, be, nu, gb: (layer, be[i], 0, 0)
    any_spec = pl.BlockSpec(memory_space=pl.ANY)
    grid_spec = pltpu.PrefetchScalarGridSpec(
        num_scalar_prefetch=3,
        grid=(nb,),
        in_specs=[pl.BlockSpec(blk, lambda i, be, nu, gb: (jnp.minimum(i, nu[0] - 1), 0)),
                  any_spec,
                  pl.BlockSpec((None, None, 1, de), bmap), pl.BlockSpec((None, None, 1, de), bmap),
                  any_spec,
                  pl.BlockSpec((None, None, 1, d), bmap)],
        out_specs=pl.BlockSpec(blk, lambda i, be, nu, gb: (i, 0)),
        scratch_shapes=[pltpu.VMEM((de, d), BF16), pltpu.VMEM((de, d), BF16), pltpu.VMEM((de, d), BF16),
                        pltpu.VMEM((2 * de, LANES), F32),
                        pltpu.VMEM((2, d, 2 * de), F32), pltpu.VMEM((2, de, d), F32),
                        pltpu.SMEM((1,), jnp.int32), pltpu.SemaphoreType.DMA((2, 2))])
    return pl.pallas_call(
        functools.partial(_expert_kernel, layer=layer),
        grid_spec=grid_spec,
        out_shape=jax.ShapeDtypeStruct((nb * blk[0], LANES), F32),
        compiler_params=_cparams(("arbitrary",), 54 << 20),
        name="moe_experts",
    )(blk_exp, n_used, grp_blocks, xs, w1, b1g, b1l, w2, b2)


def _combine_kernel(dest_hbm, gate_ref, x_ref, g_ref, b_ref, y_hbm, o_ref, idx0, idx1, ybuf, sems, *, blk_off):
    i = pl.program_id(0)
    n = pl.num_programs(0)
    tm = x_ref.shape[0]
    idx = (idx0, idx1)
    slab_rows = tm * ROW_SUBLANES

    def slab(s, k):
        return (s * TOP_K + k) * tm

    def idx_copy(step, s):
        return pltpu.make_async_copy(dest_hbm.at[blk_off + step], idx[s], sems.at[s])

    def issue_rows(s, lo=0, hi=None):
        def body(j, c):
            for k in range(TOP_K):
                pltpu.make_async_copy(_tile_rows(y_hbm, idx[s][j * TOP_K + k]), _tile_rows(ybuf, slab(s, k) + j),
                                      sems.at[2 + s]).start(priority=k % DMA_QUEUES)
            return c
        lax.fori_loop(lo, tm if hi is None else hi, body, 0, unroll=DMA_UNROLL)

    @pl.when(i == 0)
    def _():
        cp = idx_copy(0, 0)
        cp.start()
        cp.wait()
        issue_rows(0)

    @pl.when(jnp.logical_and(i == 0, n >= 2))
    def _():
        idx_copy(1, 1).start()

    def step(s):
        @pl.when(i + 1 < n)
        def _():
            idx_copy(i + 1, 1 - s).wait()
            issue_rows(1 - s, 0, tm // 2)

        @pl.when(i + 2 < n)
        def _():
            idx_copy(i + 2, s).start()

        for k in range(TOP_K):
            pltpu.make_async_copy(y_hbm.at[pl.ds(0, slab_rows), :],
                                  ybuf.at[pl.ds(slab(s, k) * ROW_SUBLANES, slab_rows), :], sems.at[2 + s]).wait()
        gate = gate_ref[...]
        h = None
        for k in range(TOP_K):
            term = gate[:, k:k + 1] * _load_row_tiles(ybuf, slab(s, k) * ROW_SUBLANES, tm)
            h = term if h is None else h + term
        o_ref[...] = _layer_norm(DEEPNORM_ALPHA * x_ref[...] + h, g_ref[...], b_ref[...])

        @pl.when(i + 1 < n)
        def _():
            issue_rows(1 - s, tm // 2, tm)

    for s in range(2):
        pl.when(i % 2 == s)(functools.partial(step, s))


def _combine(y, dest, gate, x, ln_g, ln_b, row_lo, rows):
    _, d = x.shape
    per = dest.shape[1]
    tm = per // TOP_K
    off = row_lo // tm
    vec = pl.BlockSpec((1, d), lambda i: (0, 0))
    return pl.pallas_call(
        functools.partial(_combine_kernel, blk_off=off),
        grid=(rows // tm,),
        in_specs=[pl.BlockSpec(memory_space=pl.ANY),
                  pl.BlockSpec((tm, TOP_K), lambda i: (off + i, 0)),
                  pl.BlockSpec((tm, d), lambda i: (off + i, 0)), vec, vec,
                  pl.BlockSpec(memory_space=pl.ANY)],
        out_specs=pl.BlockSpec((tm, d), lambda i: (i, 0)),
        out_shape=jax.ShapeDtypeStruct((rows, d), F32),
        scratch_shapes=[pltpu.SMEM((per,), jnp.int32), pltpu.SMEM((per,), jnp.int32),
                        pltpu.VMEM((2 * TOP_K * tm * ROW_SUBLANES, LANES), F32),
                        pltpu.SemaphoreType.DMA((4,))],
        compiler_params=_cparams(("arbitrary",), 40 << 20),
        name="moe_combine_ln",
    )(dest, gate, x, ln_g, ln_b, y)


def _moe_block(x, layer, router_w, router_b, w1, b1g, b1l, w2, b2):
    e, gate, rank, counts = _router(x, router_w, router_b)
    dest, blk_exp, n_used, pad_lo, pend, grp_blocks = _route_plan(e, rank, counts, EXPERT_ROWS, ROW_TILE)
    xs = _dispatch(x, dest, pad_lo, pend, n_used, EXPERT_ROWS)
    y = _experts(xs, blk_exp, n_used, grp_blocks, layer, w1, b1g, b1l, w2, b2, EXPERT_ROWS)
    return y, dest, gate


def _gelu(z):
    return 0.5 * z * (1.0 + jnp.tanh(0.7978845608028654 * (z + 0.044715 * (z * z * z))))


def _sgu_kernel(x_ref, win_ref, bin_ref, ng_ref, nb_ref, ws_ref, bs_ref, wout_ref, bout_ref,
                g_ref, b_ref, y_ref):
    tm = x_ref.shape[0]
    x = x_ref[...]
    xb = x.astype(BF16)
    dot = functools.partial(jnp.dot, preferred_element_type=F32)
    v = _gelu(dot(xb, win_ref[:, SGU_HALF:]) + bin_ref[:, SGU_HALF:])
    v = _layer_norm(v, ng_ref[...], nb_ref[...]).astype(BF16)
    acc = jnp.zeros((tm, D_MODEL), F32)
    for g in range(SGU_GROUPS):
        cols = slice(g * SGU_GROUP_DIM, (g + 1) * SGU_GROUP_DIM)
        u = _gelu(dot(xb, win_ref[:, cols]) + bin_ref[:, cols])
        mixed = [dot(ws_ref[g], v[c * SGU_CHUNK:(c + 1) * SGU_CHUNK, cols]) + bs_ref[:, g:g + 1]
                 for c in range(tm // SGU_CHUNK)]
        mixed = jnp.concatenate(mixed, axis=0)
        acc = acc + dot((u * mixed).astype(BF16), wout_ref[cols, :])
    h = acc + bout_ref[...]
    y_ref[...] = _layer_norm(DEEPNORM_ALPHA * x + h, g_ref[...], b_ref[...])


def _sgu_block(x, w_in, b_in, norm_g, norm_b, w_s, b_s_t, w_out, b_out, ln_g, ln_b):
    t, d = x.shape
    tm = SGU_ROWS
    row = lambda i: (i, 0)
    c2 = lambda i: (0, 0)
    c3 = lambda i: (0, 0, 0)
    once = dict(pipeline_mode=pl.Buffered(1))
    vec = pl.BlockSpec((1, d), c2)
    return pl.pallas_call(
        _sgu_kernel,
        grid=(t // tm,),
        in_specs=[pl.BlockSpec((tm, d), row),
                  pl.BlockSpec((d, 2 * SGU_HALF), c2, **once), pl.BlockSpec((1, 2 * SGU_HALF), c2),
                  pl.BlockSpec((1, SGU_HALF), c2), pl.BlockSpec((1, SGU_HALF), c2),
                  pl.BlockSpec((SGU_GROUPS, SGU_CHUNK, SGU_CHUNK), c3),
                  pl.BlockSpec((SGU_CHUNK, SGU_GROUPS), c2),
                  pl.BlockSpec((SGU_HALF, d), c2, **once), vec, vec, vec],
        out_specs=pl.BlockSpec((tm, d), row),
        out_shape=jax.ShapeDtypeStruct((t, d), F32),
        compiler_params=_cparams(("parallel",), 52 << 20),
        name="sgu_block",
    )(x, w_in, b_in, norm_g, norm_b, w_s, b_s_t, w_out, b_out, ln_g, ln_b)


def _attention_block(x, views, w_qkv, w_o, b_o, ln_g, ln_b, seq_rows):
    n_groups = N_GROUPS * HEADS
    slopes = (2.0 ** (-8.0 * jnp.arange(1, n_groups + 1, dtype=F32) / n_groups)).reshape(N_GROUPS, HEADS)
    scale = jnp.concatenate([jnp.full((GROUP_DIM,), HEAD_DIM ** -0.5, F32), jnp.ones((2 * GROUP_DIM,), F32)])
    os_, lses = [], []
    for g, (_, dil) in enumerate(DIL_CONFIGS):
        cols = [w_qkv[:, (j * N_GROUPS + g) * GROUP_DIM:(j * N_GROUPS + g + 1) * GROUP_DIM] for j in range(3)]
        w_g = (jnp.concatenate(cols, axis=1) * scale).astype(BF16)
        qkv = _qkv_project(views[dil], w_g, dil)
        segs = tuple((n // dil, length // dil) for n, length in seq_rows)
        o, lse = _attention(qkv, _band_bias(slopes[g], dil), dil, segs)
        os_.append(o)
        lses.append(lse)
    dils = tuple(dil for _, dil in DIL_CONFIGS)
    return _merge_project(os_, lses, dils, x, w_o.astype(BF16), b_o[None], ln_g[None], ln_b[None])


def kernel(x_prompt, x_sample, attn_w_qkv, attn_w_o, attn_b_o, sgu_w_in, sgu_b_in, sgu_norm_g, sgu_norm_b,
           sgu_w_s, sgu_b_s, sgu_w_out, sgu_b_out, router_w, router_b, exp_w1, exp_b1, exp_w2, exp_b2,
           ln_mix_g, ln_mix_b, ln_ffn_g, ln_ffn_b):
    d = x_prompt.shape[-1]
    tp = x_prompt.shape[0] * x_prompt.shape[1]
    ts = x_sample.shape[0] * x_sample.shape[1]
    seq_rows = ((tp, x_prompt.shape[1]), (ts, x_sample.shape[1]))
    dils = tuple(dil for _, dil in DIL_CONFIGS if dil > 1)
    x, *xviews = _token_views(x_prompt.reshape(tp, d), x_sample.reshape(ts, d), dils)
    views = dict(zip(dils, xviews))
    views[1] = x

    b1g = exp_b1[:, :, None, 0::2]
    b1l = exp_b1[:, :, None, 1::2]
    b2 = exp_b2[:, :, None, :]

    outs = None
    for i in range(DEPTH):
        j = i // 2
        if i % 2 == 0:
            x = _attention_block(x, views, attn_w_qkv[j], attn_w_o[j], attn_b_o[j], ln_mix_g[i], ln_mix_b[i],
                                 seq_rows)
        else:
            x = _sgu_block(x, sgu_w_in[j].astype(BF16), sgu_b_in[j][None], sgu_norm_g[j][None], sgu_norm_b[j][None],
                           sgu_w_s[j].astype(BF16), jnp.transpose(sgu_b_s[j]), sgu_w_out[j].astype(BF16),
                           sgu_b_out[j][None], ln_mix_g[i][None], ln_mix_b[i][None])
        y, dest, gate = _moe_block(x, i, router_w[i], router_b[i][None], exp_w1, b1g, b1l, exp_w2, b2)
        g_ln, b_ln = ln_ffn_g[i][None], ln_ffn_b[i][None]
        if i + 1 < DEPTH:
            x = _combine(y, dest, gate, x, g_ln, b_ln, 0, tp + ts)
        else:
            outs = (_combine(y, dest, gate, x, g_ln, b_ln, 0, tp).reshape(x_prompt.shape),
                    _combine(y, dest, gate, x, g_ln, b_ln, tp, ts).reshape(x_sample.shape))
    return outs
```
